```python
import math
import jax, jax.numpy as jnp
from jax import lax
import numpy as np

D_MODEL = 1024
BATCH = 4
SEQ = 8192
DEPTH = 1

CHUNK = 64
Q_BLOCK = 128
DA_HEADS = 4
DA_HEAD_DIM = 64
DA_WIDTH = DA_HEADS * 2 * DA_HEAD_DIM
SB_HEADS = 8
SB_HEAD_DIM = 64
SB_WIDTH = SB_HEADS * SB_HEAD_DIM
MIX_WIDTH = DA_WIDTH + SB_WIDTH
IN_WIDTH = 3 * DA_WIDTH + 3 * SB_WIDTH
REL_BUCKETS = 32
REL_MAX_DIST = 128
N_GROUPS = 4
EXPERTS_PER_GROUP = 8
N_EXPERTS = N_GROUPS * EXPERTS_PER_GROUP
TOP_K = 2
EXPERT_HIDDEN = 512
MOE_BLOCK = 256
EPS = 1e-6

kernel_name = "hymba_diff_stickbreak_hmoe_block"


def rms_norm(x, g):
    xf = x.astype(jnp.float32)
    y = xf * lax.rsqrt(jnp.mean(xf * xf, axis=-1, keepdims=True) + EPS)
    return (y * g.astype(jnp.float32)).astype(x.dtype)


def rel_bucket(rel):
    nb = REL_BUCKETS // 2
    max_exact = nb // 2
    base = jnp.where(rel > 0, nb, 0)
    n = jnp.abs(rel)
    nf = jnp.maximum(n, 1).astype(jnp.float32)
    large = max_exact + (jnp.log(nf / max_exact) / math.log(REL_MAX_DIST / max_exact)
                         * (nb - max_exact)).astype(jnp.int32)
    large = jnp.minimum(large, nb - 1)
    return base + jnp.where(n < max_exact, n, large)


def to_blocks(t):
    b, h, s, d = t.shape
    return jnp.transpose(t.reshape(b, h, s // Q_BLOCK, Q_BLOCK, d), (2, 0, 1, 3, 4))


def from_blocks(t):
    nblk, b, h, qb, d = t.shape
    return jnp.transpose(t, (1, 2, 0, 3, 4)).reshape(b, h, nblk * qb, d)


def diff_attention(q1, q2, k1, k2, v, rel_bias, lam):
    s_len = q1.shape[2]
    scale = q1.shape[-1] ** -0.5
    pos_k = jnp.arange(s_len, dtype=jnp.int32)

    def block(args):
        i, q1b, q2b = args
        pos_q = i * Q_BLOCK + jnp.arange(Q_BLOCK, dtype=jnp.int32)
        mask = (pos_k[None, :] // CHUNK) <= (pos_q[:, None] // CHUNK)
        bias = rel_bias[rel_bucket(pos_k[None, :] - pos_q[:, None])]
        bias = jnp.transpose(bias, (2, 0, 1)).astype(jnp.float32)

        def probs(qb, k):
            s = jnp.einsum('bhqd,bhkd->bhqk', qb, k).astype(jnp.float32) * scale + bias
            return jax.nn.softmax(jnp.where(mask, s, -jnp.inf), axis=-1)

        w = probs(q1b, k1) - lam * probs(q2b, k2)
        return jnp.einsum('bhqk,bhkd->bhqd', w.astype(v.dtype), v)

    nblk = s_len // Q_BLOCK
    out = lax.map(block, (jnp.arange(nblk, dtype=jnp.int32), to_blocks(q1), to_blocks(q2)))
    return from_blocks(out)


def stick_breaking(q, k, v):
    s_len = q.shape[2]
    scale = q.shape[-1] ** -0.5
    pos_k = jnp.arange(s_len, dtype=jnp.int32)

    def block(args):
        i, qb = args
        pos_q = i * Q_BLOCK + jnp.arange(Q_BLOCK, dtype=jnp.int32)
        causal = pos_k[None, :] < pos_q[:, None]
        z = jnp.einsum('bhqd,bhkd->bhqk', qb, k).astype(jnp.float32) * scale
        log_1m = jnp.where(causal, jax.nn.log_sigmoid(-z), 0.0)
        between = lax.cumsum(log_1m, axis=3, reverse=True) - log_1m
        a = jnp.where(causal, jnp.exp(jax.nn.log_sigmoid(z) + between), 0.0)
        return jnp.einsum('bhqk,bhkd->bhqd', a.astype(v.dtype), v)

    nblk = s_len // Q_BLOCK
    out = lax.map(block, (jnp.arange(nblk, dtype=jnp.int32), to_blocks(q)))
    return from_blocks(out)


def hier_moe(h, wg, bg, we, be, w1, w3, w2):
    bsz, s_len, d = h.shape
    n = bsz * s_len
    hf = h.reshape(n, d)
    pg = jax.nn.softmax((hf @ wg + bg).astype(jnp.float32), axis=-1)
    gsel = jnp.argmax(pg, axis=-1).astype(jnp.int32)
    pg_sel = jnp.max(pg, axis=-1)
    le = (hf @ we + be).astype(jnp.float32).reshape(n, N_GROUPS, EXPERTS_PER_GROUP)
    le = jnp.take_along_axis(le, gsel[:, None, None], axis=1)[:, 0]
    p_top, e_loc = lax.top_k(jax.nn.softmax(le, axis=-1), TOP_K)
    gate = pg_sel[:, None] * p_top / jnp.sum(p_top, axis=-1, keepdims=True)

    eid = (gsel[:, None] * EXPERTS_PER_GROUP + e_loc.astype(jnp.int32)).reshape(-1)
    tok = jnp.repeat(jnp.arange(n, dtype=jnp.int32), TOP_K)
    gflat = gate.reshape(-1)
    order = jnp.argsort(eid)
    s_eid, s_tok, s_gate = eid[order], tok[order], gflat[order]
    counts = jnp.zeros((N_EXPERTS,), jnp.int32).at[eid].add(1)
    starts = jnp.cumsum(counts) - counts
    padded = (counts + MOE_BLOCK - 1) // MOE_BLOCK * MOE_BLOCK
    pad_ends = jnp.cumsum(padded)
    pad_starts = pad_ends - padded
    dest = pad_starts[s_eid] + (jnp.arange(n * TOP_K, dtype=jnp.int32) - starts[s_eid])
    cap = n * TOP_K + N_EXPERTS * MOE_BLOCK
    tok_buf = jnp.full((cap,), n, jnp.int32).at[dest].set(s_tok)
    gate_buf = jnp.zeros((cap,), jnp.float32).at[dest].set(s_gate)
    nb = cap // MOE_BLOCK
    blk_e = jnp.minimum(jnp.searchsorted(pad_ends, jnp.arange(nb, dtype=jnp.int32) * MOE_BLOCK,
                                         side='right'), N_EXPERTS - 1).astype(jnp.int32)
    h_pad = jnp.concatenate([hf, jnp.zeros((1, d), hf.dtype)], axis=0)
    xs = h_pad[tok_buf].reshape(nb, MOE_BLOCK, d)

    def expert_block(args):
        xb, e = args
        return (jax.nn.silu(xb @ w1[e]) * (xb @ w3[e])) @ w2[e]

    ys = lax.map(expert_block, (xs, blk_e)).reshape(cap, d)
    ys = ys * gate_buf[:, None].astype(ys.dtype)
    out = jnp.zeros((n + 1, d), ys.dtype).at[tok_buf].add(ys)[:n]
    return out.reshape(bsz, s_len, d)


def setup_inputs(seed: int = 0) -> dict:
    key = jax.random.key(seed)
    ks = jax.random.split(key, 24)
    f32 = jnp.float32
    nrm = lambda k, shape, s: jax.random.normal(k, shape, f32) * s
    gain = lambda k, shape: 1.0 + 0.02 * jax.random.normal(k, shape, f32)
    return {
        "x": jax.random.normal(ks[0], (BATCH, SEQ, D_MODEL), f32),
        "g_attn": gain(ks[1], (DEPTH, D_MODEL)),
        "w_in": nrm(ks[2], (DEPTH, D_MODEL, IN_WIDTH), D_MODEL ** -0.5),
        "qn_g": gain(ks[3], (DEPTH, DA_HEAD_DIM)),
        "kn_g": gain(ks[4], (DEPTH, DA_HEAD_DIM)),
        "lam_q1": nrm(ks[5], (DEPTH, DA_HEAD_DIM), 0.1),
        "lam_k1": nrm(ks[6], (DEPTH, DA_HEAD_DIM), 0.1),
        "lam_q2": nrm(ks[7], (DEPTH, DA_HEAD_DIM), 0.1),
        "lam_k2": nrm(ks[8], (DEPTH, DA_HEAD_DIM), 0.1),
        "subln_g": gain(ks[9], (DEPTH, 2 * DA_HEAD_DIM)),
        "sb_out_g": gain(ks[10], (DEPTH, SB_HEAD_DIM)),
        "rel_bias": nrm(ks[11], (REL_BUCKETS, DA_HEADS), 0.5),
        "w_o": nrm(ks[12], (DEPTH, MIX_WIDTH, D_MODEL), MIX_WIDTH ** -0.5),
        "g_ffn": gain(ks[13], (DEPTH, D_MODEL)),
        "w_router_g": nrm(ks[14], (DEPTH, D_MODEL, N_GROUPS), D_MODEL ** -0.5),
        "b_router_g": nrm(ks[15], (DEPTH, N_GROUPS), 0.01),
        "w_router_e": nrm(ks[16], (DEPTH, D_MODEL, N_EXPERTS), D_MODEL ** -0.5),
        "b_router_e": nrm(ks[17], (DEPTH, N_EXPERTS), 0.01),
        "w1": nrm(ks[18], (DEPTH, N_EXPERTS, D_MODEL, EXPERT_HIDDEN), D_MODEL ** -0.5),
        "w3": nrm(ks[19], (DEPTH, N_EXPERTS, D_MODEL, EXPERT_HIDDEN), D_MODEL ** -0.5),
        "w2": nrm(ks[20], (DEPTH, N_EXPERTS, EXPERT_HIDDEN, D_MODEL), EXPERT_HIDDEN ** -0.5),
    }


def reference(x, g_attn, w_in, qn_g, kn_g, lam_q1, lam_k1, lam_q2, lam_k2, subln_g,
              sb_out_g, rel_bias, w_o, g_ffn, w_router_g, b_router_g, w_router_e,
              b_router_e, w1, w3, w2):
    bsz, s_len, _ = x.shape
    cut = np.cumsum([DA_WIDTH, DA_WIDTH, DA_WIDTH, SB_WIDTH, SB_WIDTH]).tolist()
    for l in range(DEPTH):
        lambda_init = 0.8 - 0.6 * math.exp(-0.3 * l)
        h = rms_norm(x, g_attn[l])
        proj = h @ w_in[l]
        dq, dk, dv, sq, sk, sv = jnp.split(proj, cut, axis=-1)

        dq = jnp.transpose(dq.reshape(bsz, s_len, DA_HEADS, 2, DA_HEAD_DIM), (3, 0, 2, 1, 4))
        dk = jnp.transpose(dk.reshape(bsz, s_len, DA_HEADS, 2, DA_HEAD_DIM), (3, 0, 2, 1, 4))
        dq = rms_norm(dq, qn_g[l])
        dk = rms_norm(dk, kn_g[l])
        dv = jnp.transpose(dv.reshape(bsz, s_len, DA_HEADS, 2 * DA_HEAD_DIM), (0, 2, 1, 3))
        lam = (jnp.exp(jnp.sum(lam_q1[l] * lam_k1[l]).astype(jnp.float32))
               - jnp.exp(jnp.sum(lam_q2[l] * lam_k2[l]).astype(jnp.float32)) + lambda_init)
        o_da = diff_attention(dq[0], dq[1], dk[0], dk[1], dv, rel_bias, lam)
        o_da = rms_norm(o_da, subln_g[l]) * (1.0 - lambda_init)
        o_da = jnp.transpose(o_da, (0, 2, 1, 3)).reshape(bsz, s_len, DA_WIDTH)

        to_heads = lambda t: jnp.transpose(t.reshape(bsz, s_len, SB_HEADS, SB_HEAD_DIM), (0, 2, 1, 3))
        o_sb = stick_breaking(to_heads(sq), to_heads(sk), to_heads(sv))
        o_sb = rms_norm(o_sb, sb_out_g[l])
        o_sb = jnp.transpose(o_sb, (0, 2, 1, 3)).reshape(bsz, s_len, SB_WIDTH)

        x = x + jnp.concatenate([o_da, o_sb], axis=-1) @ w_o[l]

        h2 = rms_norm(x, g_ffn[l])
        x = x + hier_moe(h2, w_router_g[l], b_router_g[l], w_router_e[l], b_router_e[l],
                         w1[l], w3[l], w2[l])
    return x
```

```python
import functools
import math

import jax
import jax.numpy as jnp
import numpy as np
from jax import lax
from jax.experimental import pallas as pl
from jax.experimental.pallas import tpu as pltpu

F32 = jnp.float32
BF16 = jnp.bfloat16

EPS = 1e-6
LANES = 128
DA_HEADS = 4
DA_HEAD_DIM = 64
SB_HEADS = 8
SB_HEAD_DIM = 64
CHUNK = 64
REL_BUCKETS = 32
REL_MAX_DIST = 128
N_GROUPS = 4
EXPERTS_PER_GROUP = 8
N_EXPERTS = N_GROUPS * EXPERTS_PER_GROUP
NEG_BIG = -1e30
SB_DEAD = -87.5

VMEM_LIMIT = 48 * 1024 * 1024

PROJ_ROWS = 512
DA_BLOCK = 256
SB_BLOCK = 128
MOE_BLOCK = 256
TOK_TILE = 256


def _cparams(sem):
    return pltpu.CompilerParams(dimension_semantics=sem, vmem_limit_bytes=VMEM_LIMIT)


def _in_proj_kernel(x_ref, g_ref, w_ref, gain_ref, bd_ref, o_ref):
    x = x_ref[...]
    ms = jnp.mean(x * x, axis=-1, keepdims=True)
    h = (x * lax.rsqrt(ms + EPS) * g_ref[...]).astype(BF16)
    n_chunks = o_ref.shape[1] // 512
    for c in range(n_chunks):
        cols = slice(c * 512, (c + 1) * 512)
        acc = jnp.dot(h, w_ref[:, cols], preferred_element_type=F32)
        if c < 2:
            sq = (acc * acc).astype(BF16)
            parts = []
            for s in range(2):
                ss = jnp.dot(sq[:, s * 256:(s + 1) * 256], bd_ref[...], preferred_element_type=F32)
                parts.append(ss)
            ss = jnp.concatenate(parts, axis=1)
            acc = acc * lax.rsqrt(ss * (1.0 / DA_HEAD_DIM) + EPS)
        o_ref[:, cols] = (acc * gain_ref[:, cols]).astype(BF16)


def _in_proj(x2, g_attn, w_in_bf, gain_row, bd):
    n, d = x2.shape
    width = w_in_bf.shape[1]
    return pl.pallas_call(
        _in_proj_kernel,
        grid=(n // PROJ_ROWS,),
        in_specs=[
            pl.BlockSpec((PROJ_ROWS, d), lambda i: (i, 0)),
            pl.BlockSpec((1, d), lambda i: (0, 0)),
            pl.BlockSpec((d, width), lambda i: (0, 0)),
            pl.BlockSpec((1, width), lambda i: (0, 0)),
            pl.BlockSpec((256, 256), lambda i: (0, 0)),
        ],
        out_specs=pl.BlockSpec((PROJ_ROWS, width), lambda i: (i, 0)),
        out_shape=jax.ShapeDtypeStruct((n, width), BF16),
        compiler_params=_cparams(("parallel",)),
        name="in_proj",
    )(x2, g_attn, w_in_bf, gain_row, bd)


def _da_kernel(lam_ref, q_ref, k_ref, v_ref, bias_ref, g_ref, o_ref, m_scr, l_scr, acc_scr, *, out_scale):
    t = DA_BLOCK
    i = pl.program_id(2)
    q = q_ref[0]
    lane = lax.broadcasted_iota(jnp.int32, (1, LANES), 1)
    zero = jnp.zeros_like(q)
    qq = jnp.concatenate([jnp.where(lane < DA_HEAD_DIM, q, zero),
                          jnp.where(lane >= DA_HEAD_DIM, q, zero)], axis=0)

    m_scr[...] = jnp.full(m_scr.shape, NEG_BIG, F32)
    l_scr[...] = jnp.zeros(l_scr.shape, F32)
    acc_scr[...] = jnp.zeros(acc_scr.shape, F32)

    def step(j, bias=None, mask=None):
        start = pl.multiple_of(j * t, t)
        kb = k_ref[0, pl.ds(start, t), :]
        vb = v_ref[0, pl.ds(start, t), :]
        s = lax.dot_general(qq, kb, (((1,), (1,)), ((), ())), preferred_element_type=F32)
        if bias is not None:
            s = s + jnp.concatenate([bias, bias], axis=0)
        if mask is not None:
            s = jnp.where(jnp.concatenate([mask, mask], axis=0), s, NEG_BIG)
        m_prev = m_scr[...]
        m_next = jnp.maximum(m_prev, jnp.max(s, axis=1, keepdims=True))
        alpha = jnp.exp(m_prev - m_next)
        p = jnp.exp(s - m_next)
        l_scr[...] = alpha * l_scr[...] + jnp.sum(p, axis=1, keepdims=True)
        acc_scr[...] = alpha * acc_scr[...] + jnp.dot(p.astype(BF16), vb, preferred_element_type=F32)
        m_scr[...] = m_next

    def far_body(j, carry):
        step(j)
        return carry

    lax.fori_loop(0, jnp.maximum(i - 1, 0), far_body, 0)

    @pl.when(i > 0)
    def _():
        step(i - 1, bias=bias_ref[0, 1])

    row = lax.broadcasted_iota(jnp.int32, (t, t), 0)
    col = lax.broadcasted_iota(jnp.int32, (t, t), 1)
    step(i, bias=bias_ref[0, 0], mask=(col // CHUNK) <= (row // CHUNK))

    inv_l = 1.0 / l_scr[...]
    o = acc_scr[0:t] * inv_l[0:t] - lam_ref[0] * (acc_scr[t:2 * t] * inv_l[t:2 * t])
    ms = jnp.mean(o * o, axis=-1, keepdims=True)
    o_ref[0] = (o * lax.rsqrt(ms + EPS) * (g_ref[...] * out_scale)).astype(BF16)


def _diff_attention(proj3, lam, bias_tiles, subln_row, out_scale):
    bsz, s_len, _ = proj3.shape
    t = DA_BLOCK
    nq = s_len // t
    kern = functools.partial(_da_kernel, out_scale=out_scale)
    return pl.pallas_call(
        kern,
        grid=(bsz, DA_HEADS, nq),
        in_specs=[
            pl.BlockSpec(memory_space=pltpu.SMEM),
            pl.BlockSpec((1, t, LANES), lambda b, h, i: (b, i, h)),
            pl.BlockSpec((1, s_len, LANES), lambda b, h, i: (b, 0, DA_HEADS + h)),
            pl.BlockSpec((1, s_len, LANES), lambda b, h, i: (b, 0, 2 * DA_HEADS + h)),
            pl.BlockSpec((1, 2, t, t), lambda b, h, i: (h, 0, 0, 0)),
            pl.BlockSpec((1, LANES), lambda b, h, i: (0, 0)),
        ],
        out_specs=pl.BlockSpec((1, t, LANES), lambda b, h, i: (b, i, h)),
        out_shape=jax.ShapeDtypeStruct((bsz, s_len, DA_HEADS * LANES), BF16),
        scratch_shapes=[
            pltpu.VMEM((2 * t, 1), F32),
            pltpu.VMEM((2 * t, 1), F32),
            pltpu.VMEM((2 * t, LANES), F32),
        ],
        compiler_params=_cparams(("parallel", "parallel", "arbitrary")),
        name="diff_attention",
    )(lam, proj3, proj3, proj3, bias_tiles, subln_row)


def _sb_kernel(q_ref, k_ref, v_ref, g_ref, tri_ref, o_ref, acc_scr, carry_scr):
    t = SB_BLOCK
    i = pl.program_id(2)
    q = q_ref[0]
    lane = lax.broadcasted_iota(jnp.int32, (1, LANES), 1)
    zero = jnp.zeros_like(q)
    row = lax.broadcasted_iota(jnp.int32, (t, t), 0)
    col = lax.broadcasted_iota(jnp.int32, (t, t), 1)
    strictly_before = col < row
    tri = tri_ref[...]

    outs = []
    for hh in range(2):
        qm = jnp.where((lane >= hh * SB_HEAD_DIM) & (lane < (hh + 1) * SB_HEAD_DIM), q, zero)
        acc_scr[...] = jnp.zeros(acc_scr.shape, F32)
        carry_scr[...] = jnp.zeros(carry_scr.shape, F32)

        def cond(state):
            j, alive = state
            return (j >= 0) & (alive > SB_DEAD)

        def body(state):
            j, _ = state
            start = pl.multiple_of(j * t, t)
            kb = k_ref[0, pl.ds(start, t), :]
            vb = v_ref[0, pl.ds(start, t), :]
            z = lax.dot_general(qm, kb, (((1,), (1,)), ((), ())), preferred_element_type=F32)
            log_1m = -(jnp.maximum(z, 0.0) + jnp.log(1.0 + jnp.exp(-jnp.abs(z))))
            causal = strictly_before | (j < i)
            lm = jnp.where(causal, log_1m, 0.0)
            hi = lm.astype(BF16)
            r1 = lm - hi.astype(F32)
            mid = r1.astype(BF16)
            lo = (r1 - mid.astype(F32)).astype(BF16)
            inner = (jnp.dot(hi, tri, preferred_element_type=F32)
                     + jnp.dot(mid, tri, preferred_element_type=F32)
                     + jnp.dot(lo, tri, preferred_element_type=F32))
            carry = carry_scr[...]
            a = jnp.where(causal, jnp.exp(z + log_1m + inner + carry), 0.0)
            acc_scr[...] += jnp.dot(a.astype(BF16), vb, preferred_element_type=F32)
            carry = carry + jnp.sum(lm, axis=1, keepdims=True)
            carry_scr[...] = carry
            return j - 1, jnp.max(carry)

        lax.while_loop(cond, body, (i, jnp.float32(0.0)))
        outs.append(acc_scr[...])

    o = jnp.where(lane < SB_HEAD_DIM, outs[0], outs[1])
    sq = o * o
    ss0 = jnp.sum(jnp.where(lane < SB_HEAD_DIM, sq, 0.0), axis=1, keepdims=True)
    ss1 = jnp.sum(jnp.where(lane >= SB_HEAD_DIM, sq, 0.0), axis=1, keepdims=True)
    ms = jnp.where(lane < SB_HEAD_DIM, ss0, ss1) * (1.0 / SB_HEAD_DIM)
    o_ref[0] = (o * lax.rsqrt(ms + EPS) * g_ref[...]).astype(BF16)


def _stick_breaking(proj3, sb_row, tri):
    bsz, s_len, _ = proj3.shape
    t = SB_BLOCK
    nq = s_len // t
    pairs = SB_HEADS // 2
    q_blk = 3 * DA_HEADS
    return pl.pallas_call(
        _sb_kernel,
        grid=(bsz, pairs, nq),
        in_specs=[
            pl.BlockSpec((1, t, LANES), lambda b, h, i: (b, i, q_blk + h)),
            pl.BlockSpec((1, s_len, LANES), lambda b, h, i: (b, 0, q_blk + pairs + h)),
            pl.BlockSpec((1, s_len, LANES), lambda b, h, i: (b, 0, q_blk + 2 * pairs + h)),
            pl.BlockSpec((1, LANES), lambda b, h, i: (0, 0)),
            pl.BlockSpec((t, t), lambda b, h, i: (0, 0)),
        ],
        out_specs=pl.BlockSpec((1, t, LANES), lambda b, h, i: (b, i, h)),
        out_shape=jax.ShapeDtypeStruct((bsz, s_len, pairs * LANES), BF16),
        scratch_shapes=[
            pltpu.VMEM((t, LANES), F32),
            pltpu.VMEM((t, 1), F32),
        ],
        compiler_params=_cparams(("parallel", "parallel", "arbitrary")),
        name="stick_breaking",
    )(proj3, proj3, proj3, sb_row, tri)


def _split3(a):
    hi = a.astype(BF16)
    r1 = a - hi.astype(F32)
    mid = r1.astype(BF16)
    lo = (r1 - mid.astype(F32)).astype(BF16)
    return hi, mid, lo


def _out_proj_kernel(x_ref, oda_ref, osb_ref, wo_ref, g_ref, wr_ref, br_ref, x1_ref, h2_ref, ids_ref, gates_ref):
    half = oda_ref.shape[1]
    x1 = (x_ref[...]
          + jnp.dot(oda_ref[...], wo_ref[0:half, :], preferred_element_type=F32)
          + jnp.dot(osb_ref[...], wo_ref[half:2 * half, :], preferred_element_type=F32))
    x1_ref[...] = x1
    ms = jnp.mean(x1 * x1, axis=-1, keepdims=True)
    h2 = x1 * lax.rsqrt(ms + EPS) * g_ref[...]
    h2_ref[...] = h2

    a_hi, a_mid, a_lo = _split3(h2)
    w_hi, w_mid, w_lo = wr_ref[0], wr_ref[1], wr_ref[2]
    dot = lambda a, b: jnp.dot(a, b, preferred_element_type=F32)
    logits = (dot(a_hi, w_hi) + (dot(a_hi, w_mid) + dot(a_mid, w_hi))
              + (dot(a_hi, w_lo) + dot(a_mid, w_mid) + dot(a_lo, w_hi))) + br_ref[...]

    rows = logits.shape[0]
    lane = lax.broadcasted_iota(jnp.int32, (rows, LANES), 1)
    big = jnp.int32(LANES)
    is_g = lane < N_GROUPS
    lg = jnp.where(is_g, logits, NEG_BIG)
    mg = jnp.max(lg, axis=1, keepdims=True)
    gsel = jnp.min(jnp.where(lg == mg, lane, big), axis=1, keepdims=True)
    pg_sel = 1.0 / jnp.sum(jnp.where(is_g, jnp.exp(lg - mg), 0.0), axis=1, keepdims=True)
    lo_lane = N_GROUPS + gsel * EXPERTS_PER_GROUP
    in_grp = (lane >= lo_lane) & (lane < lo_lane + EXPERTS_PER_GROUP)
    le = jnp.where(in_grp, logits, NEG_BIG)
    l1 = jnp.max(le, axis=1, keepdims=True)
    i1 = jnp.min(jnp.where(le == l1, lane, big), axis=1, keepdims=True)
    le2 = jnp.where(lane == i1, NEG_BIG, le)
    l2 = jnp.max(le2, axis=1, keepdims=True)
    i2 = jnp.min(jnp.where(le2 == l2, lane, big), axis=1, keepdims=True)
    e2 = jnp.exp(l2 - l1)
    g1 = pg_sel / (1.0 + e2)
    g2 = pg_sel * e2 / (1.0 + e2)
    ids_ref[...] = jnp.where(lane == 0, i1 - N_GROUPS, jnp.where(lane == 1, i2 - N_GROUPS, 0))
    gates_ref[...] = jnp.where(lane == 0, g1, jnp.where(lane == 1, g2, 0.0))


def _out_proj(x2, o_da, o_sb, wo_bf, g_ffn, wr3, br):
    n, d = x2.shape
    half = o_da.shape[1]
    tm = PROJ_ROWS
    return pl.pallas_call(
        _out_proj_kernel,
        grid=(n // tm,),
        in_specs=[
            pl.BlockSpec((tm, d), lambda i: (i, 0)),
            pl.BlockSpec((tm, half), lambda i: (i, 0)),
            pl.BlockSpec((tm, half), lambda i: (i, 0)),
            pl.BlockSpec((2 * half, d), lambda i: (0, 0)),
            pl.BlockSpec((1, d), lambda i: (0, 0)),
            pl.BlockSpec((3, d, LANES), lambda i: (0, 0, 0)),
            pl.BlockSpec((1, LANES), lambda i: (0, 0)),
        ],
        out_specs=[
            pl.BlockSpec((tm, d), lambda i: (i, 0)),
            pl.BlockSpec((tm, d), lambda i: (i, 0)),
            pl.BlockSpec((tm, LANES), lambda i: (i, 0)),
            pl.BlockSpec((tm, LANES), lambda i: (i, 0)),
        ],
        out_shape=[
            jax.ShapeDtypeStruct((n, d), F32),
            jax.ShapeDtypeStruct((n, d), F32),
            jax.ShapeDtypeStruct((n, LANES), jnp.int32),
            jax.ShapeDtypeStruct((n, LANES), F32),
        ],
        compiler_params=_cparams(("parallel",)),
        name="out_proj_router",
    )(x2, o_da, o_sb, wo_bf, g_ffn, wr3, br)


def _dispatch_kernel(dest_ref, h_ref, xs_in_ref, xs_ref, sem):
    del xs_in_ref
    tt = h_ref.shape[0]

    def copy(r, k):
        return pltpu.make_async_copy(h_ref.at[pl.ds(r, 1)], xs_ref.at[pl.ds(dest_ref[0, 0, 2 * r + k], 1)], sem)

    def issue(r, c):
        copy(r, 0).start()
        copy(r, 1).start()
        return c

    lax.fori_loop(0, tt, issue, 0)

    def drain(r, c):
        copy(r, 0).wait()
        copy(r, 1).wait()
        return c

    lax.fori_loop(0, tt, drain, 0)


def _dispatch(dest2, h2, xs_init):
    n, d = h2.shape
    tt = TOK_TILE
    return pl.pallas_call(
        _dispatch_kernel,
        grid=(n // tt,),
        in_specs=[
            pl.BlockSpec((1, 1, 2 * tt), lambda i: (i, 0, 0), memory_space=pltpu.SMEM),
            pl.BlockSpec((tt, d), lambda i: (i, 0)),
            pl.BlockSpec(memory_space=pl.ANY),
        ],
        out_specs=pl.BlockSpec(memory_space=pl.ANY),
        out_shape=jax.ShapeDtypeStruct(xs_init.shape, xs_init.dtype),
        scratch_shapes=[pltpu.SemaphoreType.DMA(())],
        input_output_aliases={2: 0},
        compiler_params=_cparams(("arbitrary",)),
        name="moe_dispatch",
    )(dest2, h2, xs_init)


def _expert_kernel(blk_e_ref, n_used_ref, xs_ref, w1_ref, w3_ref, w2_ref, ys_ref):
    i = pl.program_id(0)

    @pl.when(i < n_used_ref[0])
    def _():
        xb = xs_ref[...].astype(BF16)
        a = jnp.dot(xb, w1_ref[0], preferred_element_type=F32)
        b = jnp.dot(xb, w3_ref[0], preferred_element_type=F32)
        hmid = (a * jax.nn.sigmoid(a) * b).astype(BF16)
        ys_ref[...] = jnp.dot(hmid, w2_ref[0], preferred_element_type=F32)

    @pl.when(i >= n_used_ref[0])
    def _():
        ys_ref[...] = jnp.zeros(ys_ref.shape, F32)


def _experts(blk_e, n_used, xs, w1_bf, w3_bf, w2_bf):
    cap, d = xs.shape
    hid = w1_bf.shape[2]
    nb = cap // MOE_BLOCK

    def row_map(i, blk_e_ref, n_used_ref):
        return (jnp.minimum(i, n_used_ref[0] - 1), 0)

    def w_map(i, blk_e_ref, n_used_ref):
        return (blk_e_ref[jnp.minimum(i, n_used_ref[0] - 1)], 0, 0)

    grid_spec = pltpu.PrefetchScalarGridSpec(
        num_scalar_prefetch=2,
        grid=(nb,),
        in_specs=[
            pl.BlockSpec((MOE_BLOCK, d), row_map),
            pl.BlockSpec((1, d, hid), w_map),
            pl.BlockSpec((1, d, hid), w_map),
            pl.BlockSpec((1, hid, d), w_map),
        ],
        out_specs=pl.BlockSpec((MOE_BLOCK, d), lambda i, blk_e_ref, n_used_ref: (i, 0)),
    )
    return pl.pallas_call(
        _expert_kernel,
        grid_spec=grid_spec,
        out_shape=jax.ShapeDtypeStruct((cap, d), F32),
        compiler_params=_cparams(("arbitrary",)),
        name="moe_experts",
    )(blk_e, n_used, xs, w1_bf, w3_bf, w2_bf)


def _combine_kernel(dest_ref, x1_ref, gates_ref, ys_ref, o_ref, y0_scr, y1_scr, sem):
    tt = x1_ref.shape[0]

    def copy(r, k):
        dst = y0_scr if k == 0 else y1_scr
        return pltpu.make_async_copy(ys_ref.at[pl.ds(dest_ref[0, 0, 2 * r + k], 1)], dst.at[pl.ds(r, 1)], sem)

    def issue(r, c):
        copy(r, 0).start()
        copy(r, 1).start()
        return c

    lax.fori_loop(0, tt, issue, 0)

    def drain(r, c):
        copy(r, 0).wait()
        copy(r, 1).wait()
        return c

    lax.fori_loop(0, tt, drain, 0)
    g = gates_ref[...]
    o_ref[...] = x1_ref[...] + y0_scr[...] * g[:, 0:1] + y1_scr[...] * g[:, 1:2]


def _combine(dest2, x1, gates, ys):
    n, d = x1.shape
    tt = TOK_TILE
    return pl.pallas_call(
        _combine_kernel,
        grid=(n // tt,),
        in_specs=[
            pl.BlockSpec((1, 1, 2 * tt), lambda i: (i, 0, 0), memory_space=pltpu.SMEM),
            pl.BlockSpec((tt, d), lambda i: (i, 0)),
            pl.BlockSpec((tt, LANES), lambda i: (i, 0)),
            pl.BlockSpec(memory_space=pl.ANY),
        ],
        out_specs=pl.BlockSpec((tt, d), lambda i: (i, 0)),
        out_shape=jax.ShapeDtypeStruct((n, d), F32),
        scratch_shapes=[
            pltpu.VMEM((tt, d), F32),
            pltpu.VMEM((tt, d), F32),
            pltpu.SemaphoreType.DMA(()),
        ],
        compiler_params=_cparams(("arbitrary",)),
        name="moe_combine",
    )(dest2, x1, gates, ys)


def _rel_bucket_np(rel):
    nb = REL_BUCKETS // 2
    max_exact = nb // 2
    base = np.where(rel > 0, nb, 0)
    n = np.abs(rel)
    nf = np.maximum(n, 1).astype(np.float64)
    large = max_exact + (np.log(nf / max_exact) / math.log(REL_MAX_DIST / max_exact) * (nb - max_exact)).astype(np.int64)
    large = np.minimum(large, nb - 1)
    return (base + np.where(n < max_exact, n, large)).astype(np.int32)


def _bias_bucket_tiles(t):
    r = np.arange(t)[:, None]
    c = np.arange(t)[None, :]
    far = _rel_bucket_np(-np.arange(t + 1, 4 * t))
    far_bucket = int(far[0])
    assert (far == far_bucket).all()
    return np.stack([_rel_bucket_np(c - r), _rel_bucket_np(c - r - t)]), far_bucket


def kernel(x, g_attn, w_in, qn_g, kn_g, lam_q1, lam_k1, lam_q2, lam_k2, subln_g, sb_out_g, rel_bias, w_o,
           g_ffn, w_router_g, b_router_g, w_router_e, b_router_e, w1, w3, w2):
    bsz, s_len, d = x.shape
    n = bsz * s_len
    depth = g_attn.shape[0]
    da_width = DA_HEADS * 2 * DA_HEAD_DIM
    sb_width = SB_HEADS * SB_HEAD_DIM

    bd = jnp.asarray(np.kron(np.eye(256 // DA_HEAD_DIM), np.ones((DA_HEAD_DIM, DA_HEAD_DIM))), BF16)
    tri = jnp.asarray(np.tril(np.ones((SB_BLOCK, SB_BLOCK)), -1), BF16)
    buckets, far_bucket = _bias_bucket_tiles(DA_BLOCK)

    for l in range(depth):
        lambda_init = 0.8 - 0.6 * math.exp(-0.3 * l)
        x2 = x.reshape(n, d)

        q_scale = DA_HEAD_DIM ** -0.5
        gain_row = jnp.concatenate([
            jnp.tile(qn_g[l] * q_scale, 2 * DA_HEADS),
            jnp.tile(kn_g[l], 2 * DA_HEADS),
            jnp.ones((da_width,), F32),
            jnp.full((sb_width,), SB_HEAD_DIM ** -0.5, F32),
            jnp.ones((2 * sb_width,), F32),
        ])[None, :]
        proj = _in_proj(x2, g_attn[l][None, :], w_in[l].astype(BF16), gain_row, bd)
        proj3 = proj.reshape(bsz, s_len, -1)

        lam = (jnp.exp(jnp.sum(lam_q1[l] * lam_k1[l])) - jnp.exp(jnp.sum(lam_q2[l] * lam_k2[l]))
               + lambda_init).astype(F32).reshape(1)
        rb = rel_bias - rel_bias[far_bucket][None, :]
        bias_tiles = jnp.transpose(rb[buckets], (3, 0, 1, 2)).astype(F32)
        o_da = _diff_attention(proj3, lam, bias_tiles, subln_g[l][None, :], 1.0 - lambda_init)
        o_sb = _stick_breaking(proj3, jnp.tile(sb_out_g[l], 2)[None, :], tri)

        wr = jnp.concatenate([w_router_g[l], w_router_e[l],
                              jnp.zeros((d, LANES - N_GROUPS - N_EXPERTS), F32)], axis=1)
        wr_hi = wr.astype(BF16)
        wr_r1 = wr - wr_hi.astype(F32)
        wr_mid = wr_r1.astype(BF16)
        wr_lo = (wr_r1 - wr_mid.astype(F32)).astype(BF16)
        br = jnp.concatenate([b_router_g[l], b_router_e[l],
                              jnp.zeros((LANES - N_GROUPS - N_EXPERTS,), F32)])[None, :]
        x1, h2, ids, gates = _out_proj(x2, o_da.reshape(n, -1), o_sb.reshape(n, -1), w_o[l].astype(BF16),
                                       g_ffn[l][None, :], jnp.stack([wr_hi, wr_mid, wr_lo]), br)

        eid = ids[:, 0:2].reshape(-1)
        onehot = (eid[:, None] == jnp.arange(N_EXPERTS, dtype=jnp.int32)[None, :]).astype(jnp.int32)
        csum = jnp.cumsum(onehot, axis=0)
        counts = csum[-1]
        padded = (counts + MOE_BLOCK - 1) // MOE_BLOCK * MOE_BLOCK
        pad_ends = jnp.cumsum(padded)
        pad_starts = pad_ends - padded
        dest = jnp.sum(onehot * (csum - 1 + pad_starts[None, :]), axis=1).astype(jnp.int32)
        cap = 2 * n + N_EXPERTS * MOE_BLOCK
        nb = cap // MOE_BLOCK
        blk_e = jnp.minimum(jnp.searchsorted(pad_ends, jnp.arange(nb, dtype=jnp.int32) * MOE_BLOCK, side='right'),
                            N_EXPERTS - 1).astype(jnp.int32)
        n_used = (pad_ends[-1] // MOE_BLOCK).astype(jnp.int32).reshape(1)
        dest2 = dest.reshape(n // TOK_TILE, 1, 2 * TOK_TILE)

        xs = _dispatch(dest2, h2, jnp.zeros((cap, d), F32))
        ys = _experts(blk_e, n_used, xs, w1[l].astype(BF16), w3[l].astype(BF16), w2[l].astype(BF16))
        x = _combine(dest2, x1, gates, ys).reshape(bsz, s_len, d)
    return x
```

```python
import functools
import math

import jax
import jax.numpy as jnp
import numpy as np
from jax import lax
from jax.experimental import pallas as pl
from jax.experimental.pallas import tpu as pltpu

F32 = jnp.float32
BF16 = jnp.bfloat16

EPS = 1e-6
LANES = 128
DA_HEADS = 4
DA_HEAD_DIM = 64
SB_HEADS = 8
SB_HEAD_DIM = 64
CHUNK = 64
REL_BUCKETS = 32
REL_MAX_DIST = 128
N_GROUPS = 4
EXPERTS_PER_GROUP = 8
N_EXPERTS = N_GROUPS * EXPERTS_PER_GROUP
NEG_BIG = -1e30
LOG2E = math.log2(math.e)
SB_DEAD = -87.5

VMEM_LIMIT = 48 * 1024 * 1024

PROJ_ROWS = 512
DA_BLOCK = 256
SB_SUB = 128
SB_ROWS = 256
SB_WIN = 256
MOE_BLOCK = 256
TOK_TILE = 256


def _cparams(sem):
    return pltpu.CompilerParams(dimension_semantics=sem, vmem_limit_bytes=VMEM_LIMIT)


def _in_proj_kernel(x_ref, g_ref, w_ref, gain_ref, bd_ref, o_ref):
    x = x_ref[...]
    ms = jnp.mean(x * x, axis=-1, keepdims=True)
    h = (x * lax.rsqrt(ms + EPS) * g_ref[...]).astype(BF16)
    n_chunks = o_ref.shape[1] // 512
    for c in range(n_chunks):
        cols = slice(c * 512, (c + 1) * 512)
        acc = jnp.dot(h, w_ref[:, cols], preferred_element_type=F32)
        if c < 2:
            sq = (acc * acc).astype(BF16)
            parts = []
            for s in range(2):
                ss = jnp.dot(sq[:, s * 256:(s + 1) * 256], bd_ref[...], preferred_element_type=F32)
                parts.append(ss)
            ss = jnp.concatenate(parts, axis=1)
            acc = acc * lax.rsqrt(ss * (1.0 / DA_HEAD_DIM) + EPS)
        o_ref[:, cols] = (acc * gain_ref[:, cols]).astype(BF16)


def _in_proj(x2, g_attn, w_in_bf, gain_row, bd):
    n, d = x2.shape
    width = w_in_bf.shape[1]
    return pl.pallas_call(
        _in_proj_kernel,
        grid=(n // PROJ_ROWS,),
        in_specs=[
            pl.BlockSpec((PROJ_ROWS, d), lambda i: (i, 0)),
            pl.BlockSpec((1, d), lambda i: (0, 0)),
            pl.BlockSpec((d, width), lambda i: (0, 0)),
            pl.BlockSpec((1, width), lambda i: (0, 0)),
            pl.BlockSpec((256, 256), lambda i: (0, 0)),
        ],
        out_specs=pl.BlockSpec((PROJ_ROWS, width), lambda i: (i, 0)),
        out_shape=jax.ShapeDtypeStruct((n, width), BF16),
        compiler_params=_cparams(("parallel",)),
        name="in_proj",
    )(x2, g_attn, w_in_bf, gain_row, bd)


def _da_kernel(lam_ref, q_ref, k_ref, v_ref, bias_ref, g_ref, o_ref, m_scr, acc_scr, *, out_scale):
    t = DA_BLOCK
    i = pl.program_id(2)
    q = q_ref[0]
    lane = lax.broadcasted_iota(jnp.int32, (1, LANES), 1)
    zero = jnp.zeros_like(q)
    qq = jnp.concatenate([jnp.where(lane < DA_HEAD_DIM, q, zero),
                          jnp.where(lane >= DA_HEAD_DIM, q, zero)], axis=0)

    ones = jnp.ones((t, LANES), BF16)

    m_scr[...] = jnp.full(m_scr.shape, NEG_BIG, F32)
    acc_scr[...] = jnp.zeros(acc_scr.shape, F32)

    def step(j, bias=None, mask=None):
        start = pl.multiple_of(j * t, t)
        kb = k_ref[0, pl.ds(start, t), :]
        vb = jnp.concatenate([v_ref[0, pl.ds(start, t), :], ones], axis=1)
        s = lax.dot_general(qq, kb, (((1,), (1,)), ((), ())), preferred_element_type=F32)
        if bias is not None:
            s = s + jnp.concatenate([bias, bias], axis=0)
        if mask is not None:
            s = jnp.where(jnp.concatenate([mask, mask], axis=0), s, NEG_BIG)
        m_prev = m_scr[...]
        m_next = jnp.maximum(m_prev, jnp.max(s, axis=1, keepdims=True))
        alpha = jnp.exp2(m_prev - m_next)
        p = jnp.exp2(s - jnp.concatenate([m_next] * (t // LANES), axis=1))
        pv = jnp.dot(p.astype(BF16), vb, preferred_element_type=F32)
        acc_scr[...] = jnp.concatenate([alpha, alpha], axis=1) * acc_scr[...] + pv
        m_scr[...] = m_next

    def far_body(j, carry):
        step(j)
        return carry

    lax.fori_loop(0, jnp.maximum(i - 1, 0), far_body, 0)

    @pl.when(i > 0)
    def _():
        step(i - 1, bias=bias_ref[0, 1])

    row = lax.broadcasted_iota(jnp.int32, (t, t), 0)
    col = lax.broadcasted_iota(jnp.int32, (t, t), 1)
    step(i, bias=bias_ref[0, 0], mask=(col // CHUNK) <= (row // CHUNK))

    acc = acc_scr[...]
    o_all = acc[:, 0:LANES] / acc[:, LANES:2 * LANES]
    o = o_all[0:t] - lam_ref[0] * o_all[t:2 * t]
    ms = jnp.mean(o * o, axis=-1, keepdims=True)
    o_ref[0] = (o * lax.rsqrt(ms + EPS) * (g_ref[...] * out_scale)).astype(BF16)


def _diff_attention(proj3, lam, bias_tiles, subln_row, out_scale):
    bsz, s_len, _ = proj3.shape
    t = DA_BLOCK
    nq = s_len // t
    kern = functools.partial(_da_kernel, out_scale=out_scale)
    return pl.pallas_call(
        kern,
        grid=(bsz, DA_HEADS, nq),
        in_specs=[
            pl.BlockSpec(memory_space=pltpu.SMEM),
            pl.BlockSpec((1, t, LANES), lambda b, h, i: (b, i, h)),
            pl.BlockSpec((1, s_len, LANES), lambda b, h, i: (b, 0, DA_HEADS + h)),
            pl.BlockSpec((1, s_len, LANES), lambda b, h, i: (b, 0, 2 * DA_HEADS + h)),
            pl.BlockSpec((1, 2, t, t), lambda b, h, i: (h, 0, 0, 0)),
            pl.BlockSpec((1, LANES), lambda b, h, i: (0, 0)),
        ],
        out_specs=pl.BlockSpec((1, t, LANES), lambda b, h, i: (b, i, h)),
        out_shape=jax.ShapeDtypeStruct((bsz, s_len, DA_HEADS * LANES), BF16),
        scratch_shapes=[
            pltpu.VMEM((2 * t, LANES), F32),
            pltpu.VMEM((2 * t, 2 * LANES), F32),
        ],
        compiler_params=_cparams(("parallel", "parallel", "arbitrary")),
        name="diff_attention",
    )(lam, proj3, proj3, proj3, bias_tiles, subln_row)


def _sb_tile(qm, kb, vb, causal, tri, carry):
    z = lax.dot_general(qm, kb, (((1,), (1,)), ((), ())), preferred_element_type=F32)
    log_1m = -(jnp.maximum(z, 0.0) + jnp.log(1.0 + jnp.exp(-jnp.abs(z))))
    lm = log_1m if causal is None else jnp.where(causal, log_1m, 0.0)
    hi = lm.astype(BF16)
    mid = (lm - hi.astype(F32)).astype(BF16)
    inner = jnp.dot(hi, tri, preferred_element_type=F32) + jnp.dot(mid, tri, preferred_element_type=F32)
    logit = z + log_1m + inner
    if carry is not None:
        logit = logit + carry
    a = jnp.exp(logit)
    if causal is not None:
        a = jnp.where(causal, a, 0.0)
    pv = jnp.dot(a.astype(BF16), vb, preferred_element_type=F32)
    return pv, jnp.sum(lm, axis=1, keepdims=True)


def _sb_kernel(q_ref, k_ref, v_ref, g_ref, tri_ref, o_ref, acc_scr, carry_scr):
    sub, win = SB_SUB, SB_WIN
    i = pl.program_id(2)
    lane = lax.broadcasted_iota(jnp.int32, (1, LANES), 1)
    tri = tri_ref[...]
    row = lax.broadcasted_iota(jnp.int32, (sub, win), 0)
    col = lax.broadcasted_iota(jnp.int32, (sub, win), 1)
    n_sub = SB_ROWS // sub

    chains = []
    for u in range(n_sub):
        r0 = i * SB_ROWS + u * sub
        start = pl.multiple_of(jnp.maximum(r0 - sub, 0), sub)
        kw = k_ref[0, pl.ds(start, win), :]
        vw = v_ref[0, pl.ds(start, win), :]
        causal = (start + col) < (r0 + row)
        qu = q_ref[0, u * sub:(u + 1) * sub, :]
        for hh in range(2):
            c = 2 * u + hh
            qm = jnp.where((lane >= hh * SB_HEAD_DIM) & (lane < (hh + 1) * SB_HEAD_DIM), qu, jnp.zeros_like(qu))
            pv, total = _sb_tile(qm, kw, vw, causal, tri, None)
            acc_scr[c] = pv
            carry_scr[c] = jnp.broadcast_to(total, (sub, LANES))
            chains.append((c, qm, start // sub - 1, jnp.max(total)))

    tri_sub = tri[0:sub, 0:sub]
    for c, qm, j0, alive0 in chains:
        def cond(state):
            j, alive = state
            return (j >= 0) & (alive > SB_DEAD)

        def body(state, c=c, qm=qm):
            j, _ = state
            s0 = pl.multiple_of(j * sub, sub)
            kb = k_ref[0, pl.ds(s0, sub), :]
            vb = v_ref[0, pl.ds(s0, sub), :]
            carry = carry_scr[c]
            pv, total = _sb_tile(qm, kb, vb, None, tri_sub, carry)
            acc_scr[c] += pv
            carry = carry + total
            carry_scr[c] = carry
            return j - 1, jnp.max(carry)

        lax.while_loop(cond, body, (j0, alive0))

    for u in range(n_sub):
        o = jnp.where(lane < SB_HEAD_DIM, acc_scr[2 * u], acc_scr[2 * u + 1])
        sq = o * o
        ss0 = jnp.sum(jnp.where(lane < SB_HEAD_DIM, sq, 0.0), axis=1, keepdims=True)
        ss1 = jnp.sum(jnp.where(lane >= SB_HEAD_DIM, sq, 0.0), axis=1, keepdims=True)
        ms = jnp.where(lane < SB_HEAD_DIM, ss0, ss1) * (1.0 / SB_HEAD_DIM)
        o_ref[0, u * sub:(u + 1) * sub, :] = (o * lax.rsqrt(ms + EPS) * g_ref[...]).astype(BF16)


def _stick_breaking(proj3, sb_row, tri):
    bsz, s_len, _ = proj3.shape
    t = SB_ROWS
    nq = s_len // t
    pairs = SB_HEADS // 2
    q_blk = 3 * DA_HEADS
    n_chains = 2 * (SB_ROWS // SB_SUB)
    return pl.pallas_call(
        _sb_kernel,
        grid=(bsz, pairs, nq),
        in_specs=[
            pl.BlockSpec((1, t, LANES), lambda b, h, i: (b, i, q_blk + h)),
            pl.BlockSpec((1, s_len, LANES), lambda b, h, i: (b, 0, q_blk + pairs + h)),
            pl.BlockSpec((1, s_len, LANES), lambda b, h, i: (b, 0, q_blk + 2 * pairs + h)),
            pl.BlockSpec((1, LANES), lambda b, h, i: (0, 0)),
            pl.BlockSpec((SB_WIN, SB_WIN), lambda b, h, i: (0, 0)),
        ],
        out_specs=pl.BlockSpec((1, t, LANES), lambda b, h, i: (b, i, h)),
        out_shape=jax.ShapeDtypeStruct((bsz, s_len, pairs * LANES), BF16),
        scratch_shapes=[
            pltpu.VMEM((n_chains, SB_SUB, LANES), F32),
            pltpu.VMEM((n_chains, SB_SUB, LANES), F32),
        ],
        compiler_params=_cparams(("parallel", "parallel", "arbitrary")),
        name="stick_breaking",
    )(proj3, proj3, proj3, sb_row, tri)


def _split3(a):
    hi = a.astype(BF16)
    r1 = a - hi.astype(F32)
    mid = r1.astype(BF16)
    lo = (r1 - mid.astype(F32)).astype(BF16)
    return hi, mid, lo


def _out_proj_kernel(x_ref, oda_ref, osb_ref, wo_ref, g_ref, wr_ref, br_ref, x1_ref, h2_ref, ids_ref, gates_ref):
    half = oda_ref.shape[1]
    x1 = (x_ref[...]
          + jnp.dot(oda_ref[...], wo_ref[0:half, :], preferred_element_type=F32)
          + jnp.dot(osb_ref[...], wo_ref[half:2 * half, :], preferred_element_type=F32))
    x1_ref[...] = x1
    ms = jnp.mean(x1 * x1, axis=-1, keepdims=True)
    h2 = x1 * lax.rsqrt(ms + EPS) * g_ref[...]
    h2_ref[...] = h2

    a_hi, a_mid, a_lo = _split3(h2)
    w_hi, w_mid, w_lo = wr_ref[0], wr_ref[1], wr_ref[2]
    dot = lambda a, b: jnp.dot(a, b, preferred_element_type=F32)
    logits = (dot(a_hi, w_hi) + (dot(a_hi, w_mid) + dot(a_mid, w_hi))
              + (dot(a_hi, w_lo) + dot(a_mid, w_mid) + dot(a_lo, w_hi))) + br_ref[...]

    rows = logits.shape[0]
    lane = lax.broadcasted_iota(jnp.int32, (rows, LANES), 1)
    big = jnp.int32(LANES)
    is_g = lane < N_GROUPS
    lg = jnp.where(is_g, logits, NEG_BIG)
    mg = jnp.max(lg, axis=1, keepdims=True)
    gsel = jnp.min(jnp.where(lg == mg, lane, big), axis=1, keepdims=True)
    pg_sel = 1.0 / jnp.sum(jnp.where(is_g, jnp.exp(lg - mg), 0.0), axis=1, keepdims=True)
    lo_lane = N_GROUPS + gsel * EXPERTS_PER_GROUP
    in_grp = (lane >= lo_lane) & (lane < lo_lane + EXPERTS_PER_GROUP)
    le = jnp.where(in_grp, logits, NEG_BIG)
    l1 = jnp.max(le, axis=1, keepdims=True)
    i1 = jnp.min(jnp.where(le == l1, lane, big), axis=1, keepdims=True)
    le2 = jnp.where(lane == i1, NEG_BIG, le)
    l2 = jnp.max(le2, axis=1, keepdims=True)
    i2 = jnp.min(jnp.where(le2 == l2, lane, big), axis=1, keepdims=True)
    e2 = jnp.exp(l2 - l1)
    g1 = pg_sel / (1.0 + e2)
    g2 = pg_sel * e2 / (1.0 + e2)
    ids_ref[...] = jnp.where(lane == 0, i1 - N_GROUPS, jnp.where(lane == 1, i2 - N_GROUPS, 0))
    gates_ref[...] = jnp.where(lane == 0, g1, jnp.where(lane == 1, g2, 0.0))


def _out_proj(x2, o_da, o_sb, wo_bf, g_ffn, wr3, br):
    n, d = x2.shape
    half = o_da.shape[1]
    tm = PROJ_ROWS
    return pl.pallas_call(
        _out_proj_kernel,
        grid=(n // tm,),
        in_specs=[
            pl.BlockSpec((tm, d), lambda i: (i, 0)),
            pl.BlockSpec((tm, half), lambda i: (i, 0)),
            pl.BlockSpec((tm, half), lambda i: (i, 0)),
            pl.BlockSpec((2 * half, d), lambda i: (0, 0)),
            pl.BlockSpec((1, d), lambda i: (0, 0)),
            pl.BlockSpec((3, d, LANES), lambda i: (0, 0, 0)),
            pl.BlockSpec((1, LANES), lambda i: (0, 0)),
        ],
        out_specs=[
            pl.BlockSpec((tm, d), lambda i: (i, 0)),
            pl.BlockSpec((tm, d), lambda i: (i, 0)),
            pl.BlockSpec((tm, LANES), lambda i: (i, 0)),
            pl.BlockSpec((tm, LANES), lambda i: (i, 0)),
        ],
        out_shape=[
            jax.ShapeDtypeStruct((n, d), F32),
            jax.ShapeDtypeStruct((n, d), F32),
            jax.ShapeDtypeStruct((n, LANES), jnp.int32),
            jax.ShapeDtypeStruct((n, LANES), F32),
        ],
        compiler_params=_cparams(("parallel",)),
        name="out_proj_router",
    )(x2, o_da, o_sb, wo_bf, g_ffn, wr3, br)


def _dispatch_kernel(dest_ref, h_ref, xs_in_ref, xs_ref, sem):
    del xs_in_ref
    tt = h_ref.shape[0]

    def copy(r, k):
        return pltpu.make_async_copy(h_ref.at[pl.ds(r, 1)], xs_ref.at[pl.ds(dest_ref[0, 0, 2 * r + k], 1)], sem)

    def issue(r, c):
        copy(r, 0).start()
        copy(r, 1).start()
        return c

    lax.fori_loop(0, tt, issue, 0)

    def drain(r, c):
        copy(r, 0).wait()
        copy(r, 1).wait()
        return c

    lax.fori_loop(0, tt, drain, 0)


def _dispatch(dest2, h2, xs_init):
    n, d = h2.shape
    tt = TOK_TILE
    return pl.pallas_call(
        _dispatch_kernel,
        grid=(n // tt,),
        in_specs=[
            pl.BlockSpec((1, 1, 2 * tt), lambda i: (i, 0, 0), memory_space=pltpu.SMEM),
            pl.BlockSpec((tt, d), lambda i: (i, 0)),
            pl.BlockSpec(memory_space=pl.ANY),
        ],
        out_specs=pl.BlockSpec(memory_space=pl.ANY),
        out_shape=jax.ShapeDtypeStruct(xs_init.shape, xs_init.dtype),
        scratch_shapes=[pltpu.SemaphoreType.DMA(())],
        input_output_aliases={2: 0},
        compiler_params=_cparams(("arbitrary",)),
        name="moe_dispatch",
    )(dest2, h2, xs_init)


def _expert_kernel(blk_e_ref, n_used_ref, xs_ref, w1_ref, w3_ref, w2_ref, ys_ref):
    i = pl.program_id(0)

    @pl.when(i < n_used_ref[0])
    def _():
        xb = xs_ref[...].astype(BF16)
        a = jnp.dot(xb, w1_ref[0], preferred_element_type=F32)
        b = jnp.dot(xb, w3_ref[0], preferred_element_type=F32)
        hmid = (a * jax.nn.sigmoid(a) * b).astype(BF16)
        ys_ref[...] = jnp.dot(hmid, w2_ref[0], preferred_element_type=F32)

    @pl.when(i >= n_used_ref[0])
    def _():
        ys_ref[...] = jnp.zeros(ys_ref.shape, F32)


def _experts(blk_e, n_used, xs, w1_bf, w3_bf, w2_bf):
    cap, d = xs.shape
    hid = w1_bf.shape[2]
    nb = cap // MOE_BLOCK

    def row_map(i, blk_e_ref, n_used_ref):
        return (jnp.minimum(i, n_used_ref[0] - 1), 0)

    def w_map(i, blk_e_ref, n_used_ref):
        return (blk_e_ref[jnp.minimum(i, n_used_ref[0] - 1)], 0, 0)

    grid_spec = pltpu.PrefetchScalarGridSpec(
        num_scalar_prefetch=2,
        grid=(nb,),
        in_specs=[
            pl.BlockSpec((MOE_BLOCK, d), row_map),
            pl.BlockSpec((1, d, hid), w_map),
            pl.BlockSpec((1, d, hid), w_map),
            pl.BlockSpec((1, hid, d), w_map),
        ],
        out_specs=pl.BlockSpec((MOE_BLOCK, d), lambda i, blk_e_ref, n_used_ref: (i, 0)),
    )
    return pl.pallas_call(
        _expert_kernel,
        grid_spec=grid_spec,
        out_shape=jax.ShapeDtypeStruct((cap, d), F32),
        compiler_params=_cparams(("arbitrary",)),
        name="moe_experts",
    )(blk_e, n_used, xs, w1_bf, w3_bf, w2_bf)


def _combine_kernel(dest_ref, x1_ref, gates_ref, ys_ref, o_ref, y0_scr, y1_scr, sem):
    tt = x1_ref.shape[0]

    def copy(r, k):
        dst = y0_scr if k == 0 else y1_scr
        return pltpu.make_async_copy(ys_ref.at[pl.ds(dest_ref[0, 0, 2 * r + k], 1)], dst.at[pl.ds(r, 1)], sem)

    def issue(r, c):
        copy(r, 0).start()
        copy(r, 1).start()
        return c

    lax.fori_loop(0, tt, issue, 0)

    def drain(r, c):
        copy(r, 0).wait()
        copy(r, 1).wait()
        return c

    lax.fori_loop(0, tt, drain, 0)
    g = gates_ref[...]
    o_ref[...] = x1_ref[...] + y0_scr[...] * g[:, 0:1] + y1_scr[...] * g[:, 1:2]


def _combine(dest2, x1, gates, ys):
    n, d = x1.shape
    tt = TOK_TILE
    return pl.pallas_call(
        _combine_kernel,
        grid=(n // tt,),
        in_specs=[
            pl.BlockSpec((1, 1, 2 * tt), lambda i: (i, 0, 0), memory_space=pltpu.SMEM),
            pl.BlockSpec((tt, d), lambda i: (i, 0)),
            pl.BlockSpec((tt, LANES), lambda i: (i, 0)),
            pl.BlockSpec(memory_space=pl.ANY),
        ],
        out_specs=pl.BlockSpec((tt, d), lambda i: (i, 0)),
        out_shape=jax.ShapeDtypeStruct((n, d), F32),
        scratch_shapes=[
            pltpu.VMEM((tt, d), F32),
            pltpu.VMEM((tt, d), F32),
            pltpu.SemaphoreType.DMA(()),
        ],
        compiler_params=_cparams(("arbitrary",)),
        name="moe_combine",
    )(dest2, x1, gates, ys)


def _rel_bucket_np(rel):
    nb = REL_BUCKETS // 2
    max_exact = nb // 2
    base = np.where(rel > 0, nb, 0)
    n = np.abs(rel)
    nf = np.maximum(n, 1).astype(np.float64)
    large = max_exact + (np.log(nf / max_exact) / math.log(REL_MAX_DIST / max_exact) * (nb - max_exact)).astype(np.int64)
    large = np.minimum(large, nb - 1)
    return (base + np.where(n < max_exact, n, large)).astype(np.int32)


def _bias_bucket_tiles(t):
    r = np.arange(t)[:, None]
    c = np.arange(t)[None, :]
    far = _rel_bucket_np(-np.arange(t + 1, 4 * t))
    far_bucket = int(far[0])
    assert (far == far_bucket).all()
    return np.stack([_rel_bucket_np(c - r), _rel_bucket_np(c - r - t)]), far_bucket


def kernel(x, g_attn, w_in, qn_g, kn_g, lam_q1, lam_k1, lam_q2, lam_k2, subln_g, sb_out_g, rel_bias, w_o,
           g_ffn, w_router_g, b_router_g, w_router_e, b_router_e, w1, w3, w2):
    bsz, s_len, d = x.shape
    n = bsz * s_len
    depth = g_attn.shape[0]
    da_width = DA_HEADS * 2 * DA_HEAD_DIM
    sb_width = SB_HEADS * SB_HEAD_DIM

    bd = jnp.asarray(np.kron(np.eye(256 // DA_HEAD_DIM), np.ones((DA_HEAD_DIM, DA_HEAD_DIM))), BF16)
    tri = jnp.asarray(np.tril(np.ones((SB_WIN, SB_WIN)), -1), BF16)
    buckets, far_bucket = _bias_bucket_tiles(DA_BLOCK)

    for l in range(depth):
        lambda_init = 0.8 - 0.6 * math.exp(-0.3 * l)
        x2 = x.reshape(n, d)

        q_scale = DA_HEAD_DIM ** -0.5 * LOG2E
        gain_row = jnp.concatenate([
            jnp.tile(qn_g[l] * q_scale, 2 * DA_HEADS),
            jnp.tile(kn_g[l], 2 * DA_HEADS),
            jnp.ones((da_width,), F32),
            jnp.full((sb_width,), SB_HEAD_DIM ** -0.5, F32),
            jnp.ones((2 * sb_width,), F32),
        ])[None, :]
        proj = _in_proj(x2, g_attn[l][None, :], w_in[l].astype(BF16), gain_row, bd)
        proj3 = proj.reshape(bsz, s_len, -1)

        lam = (jnp.exp(jnp.sum(lam_q1[l] * lam_k1[l])) - jnp.exp(jnp.sum(lam_q2[l] * lam_k2[l]))
               + lambda_init).astype(F32).reshape(1)
        rb = (rel_bias - rel_bias[far_bucket][None, :]) * LOG2E
        bias_tiles = jnp.transpose(rb[buckets], (3, 0, 1, 2)).astype(F32)
        o_da = _diff_attention(proj3, lam, bias_tiles, subln_g[l][None, :], 1.0 - lambda_init)
        o_sb = _stick_breaking(proj3, jnp.tile(sb_out_g[l], 2)[None, :], tri)

        wr = jnp.concatenate([w_router_g[l], w_router_e[l],
                              jnp.zeros((d, LANES - N_GROUPS - N_EXPERTS), F32)], axis=1)
        wr_hi = wr.astype(BF16)
        wr_r1 = wr - wr_hi.astype(F32)
        wr_mid = wr_r1.astype(BF16)
        wr_lo = (wr_r1 - wr_mid.astype(F32)).astype(BF16)
        br = jnp.concatenate([b_router_g[l], b_router_e[l],
                              jnp.zeros((LANES - N_GROUPS - N_EXPERTS,), F32)])[None, :]
        x1, h2, ids, gates = _out_proj(x2, o_da.reshape(n, -1), o_sb.reshape(n, -1), w_o[l].astype(BF16),
                                       g_ffn[l][None, :], jnp.stack([wr_hi, wr_mid, wr_lo]), br)

        eid = ids[:, 0:2].reshape(-1)
        onehot = (eid[:, None] == jnp.arange(N_EXPERTS, dtype=jnp.int32)[None, :]).astype(jnp.int32)
        csum = jnp.cumsum(onehot, axis=0)
        counts = csum[-1]
        padded = (counts + MOE_BLOCK - 1) // MOE_BLOCK * MOE_BLOCK
        pad_ends = jnp.cumsum(padded)
        pad_starts = pad_ends - padded
        dest = jnp.sum(onehot * (csum - 1 + pad_starts[None, :]), axis=1).astype(jnp.int32)
        cap = 2 * n + N_EXPERTS * MOE_BLOCK
        nb = cap // MOE_BLOCK
        blk_e = jnp.minimum(jnp.searchsorted(pad_ends, jnp.arange(nb, dtype=jnp.int32) * MOE_BLOCK, side='right'),
                            N_EXPERTS - 1).astype(jnp.int32)
        n_used = (pad_ends[-1] // MOE_BLOCK).astype(jnp.int32).reshape(1)
        dest2 = dest.reshape(n // TOK_TILE, 1, 2 * TOK_TILE)

        xs = _dispatch(dest2, h2, jnp.zeros((cap, d), F32))
        ys = _experts(blk_e, n_used, xs, w1[l].astype(BF16), w3[l].astype(BF16), w2[l].astype(BF16))
        x = _combine(dest2, x1, gates, ys).reshape(bsz, s_len, d)
    return x
```

```python
import functools
import math

import jax
import jax.numpy as jnp
import numpy as np
from jax import lax
from jax.experimental import pallas as pl
from jax.experimental.pallas import tpu as pltpu

F32 = jnp.float32
BF16 = jnp.bfloat16

EPS = 1e-6
LANES = 128
DA_HEADS = 4
DA_HEAD_DIM = 64
SB_HEADS = 8
SB_HEAD_DIM = 64
CHUNK = 64
REL_BUCKETS = 32
REL_MAX_DIST = 128
N_GROUPS = 4
EXPERTS_PER_GROUP = 8
N_EXPERTS = N_GROUPS * EXPERTS_PER_GROUP
NEG_BIG = -1e30
LOG2E = math.log2(math.e)
SB_DEAD = -87.5

VMEM_LIMIT = 48 * 1024 * 1024

PROJ_ROWS = 512
DA_BLOCK = 256
DA_UNROLL = 4
DA_SAFE_RANGE = 100.0
SB_SUB = 128
SB_ROWS = 256
SB_WIN = 256
MOE_BLOCK = 256
TOK_TILE = 256


def _cparams(sem):
    return pltpu.CompilerParams(dimension_semantics=sem, vmem_limit_bytes=VMEM_LIMIT)


def _in_proj_kernel(x_ref, g_ref, w_ref, gain_ref, bd_ref, o_ref):
    x = x_ref[...]
    ms = jnp.mean(x * x, axis=-1, keepdims=True)
    h = (x * lax.rsqrt(ms + EPS) * g_ref[...]).astype(BF16)
    n_chunks = o_ref.shape[1] // 512
    for c in range(n_chunks):
        cols = slice(c * 512, (c + 1) * 512)
        acc = jnp.dot(h, w_ref[:, cols], preferred_element_type=F32)
        if c < 2:
            sq = (acc * acc).astype(BF16)
            parts = []
            for s in range(2):
                ss = jnp.dot(sq[:, s * 256:(s + 1) * 256], bd_ref[...], preferred_element_type=F32)
                parts.append(ss)
            ss = jnp.concatenate(parts, axis=1)
            acc = acc * lax.rsqrt(ss * (1.0 / DA_HEAD_DIM) + EPS)
        o_ref[:, cols] = (acc * gain_ref[:, cols]).astype(BF16)


def _in_proj(x2, g_attn, w_in_bf, gain_row, bd):
    n, d = x2.shape
    width = w_in_bf.shape[1]
    return pl.pallas_call(
        _in_proj_kernel,
        grid=(n // PROJ_ROWS,),
        in_specs=[
            pl.BlockSpec((PROJ_ROWS, d), lambda i: (i, 0)),
            pl.BlockSpec((1, d), lambda i: (0, 0)),
            pl.BlockSpec((d, width), lambda i: (0, 0)),
            pl.BlockSpec((1, width), lambda i: (0, 0)),
            pl.BlockSpec((256, 256), lambda i: (0, 0)),
        ],
        out_specs=pl.BlockSpec((PROJ_ROWS, width), lambda i: (i, 0)),
        out_shape=jax.ShapeDtypeStruct((n, width), BF16),
        compiler_params=_cparams(("parallel",)),
        name="in_proj",
    )(x2, g_attn, w_in_bf, gain_row, bd)


def _da_kernel(lam_ref, brange_ref, q_ref, k_ref, v_ref, bias_ref, g_ref, o_ref, m_scr, acc_scr, kmax_scr, *,
               out_scale):
    t = DA_BLOCK
    h = pl.program_id(1)
    i = pl.program_id(2)
    s_len = k_ref.shape[1]
    ones_col = jnp.ones((LANES, LANES), BF16)

    @pl.when(i == 0)
    def _():
        lane_k = lax.broadcasted_iota(jnp.int32, (1, LANES), 1)
        rows = 512

        def kbody(c, best):
            kc = k_ref[0, pl.ds(pl.multiple_of(c * rows, rows), rows), :].astype(F32)
            sq = kc * kc
            n1 = jnp.dot(jnp.where(lane_k < DA_HEAD_DIM, sq, 0.0).astype(BF16), ones_col, preferred_element_type=F32)
            n2 = jnp.dot(jnp.where(lane_k >= DA_HEAD_DIM, sq, 0.0).astype(BF16), ones_col, preferred_element_type=F32)
            return (jnp.maximum(best[0], jnp.max(n1, axis=0, keepdims=True)),
                    jnp.maximum(best[1], jnp.max(n2, axis=0, keepdims=True)))

        zero_row = jnp.zeros((1, LANES), F32)
        k1, k2 = lax.fori_loop(0, s_len // rows, kbody, (zero_row, zero_row))
        kmax_scr[0:8, :] = jnp.broadcast_to(k1, (8, LANES))
        kmax_scr[8:16, :] = jnp.broadcast_to(k2, (8, LANES))

    q = q_ref[0]
    lane = lax.broadcasted_iota(jnp.int32, (1, LANES), 1)
    zero = jnp.zeros_like(q)
    qq = jnp.concatenate([jnp.where(lane < DA_HEAD_DIM, q, zero),
                          jnp.where(lane >= DA_HEAD_DIM, q, zero)], axis=0)
    ones = jnp.ones((t, LANES), BF16)
    row = lax.broadcasted_iota(jnp.int32, (t, t), 0)
    col = lax.broadcasted_iota(jnp.int32, (t, t), 1)
    chunk_mask = (col // CHUNK) <= (row // CHUNK)
    n_far = jnp.maximum(i - 1, 0)

    def scores(j, bias, mask):
        start = pl.multiple_of(j * t, t)
        kb = k_ref[0, pl.ds(start, t), :]
        vb = jnp.concatenate([v_ref[0, pl.ds(start, t), :], ones], axis=1)
        s = lax.dot_general(qq, kb, (((1,), (1,)), ((), ())), preferred_element_type=F32)
        if bias is not None:
            s = s + jnp.concatenate([bias, bias], axis=0)
        if mask is not None:
            s = jnp.where(jnp.concatenate([mask, mask], axis=0), s, NEG_BIG)
        return s, vb

    qf = qq.astype(F32)
    qn2 = jnp.dot((qf * qf).astype(BF16), ones_col, preferred_element_type=F32)
    kmax2 = jnp.concatenate([jnp.broadcast_to(kmax_scr[0:1, :], (t, LANES)),
                             jnp.broadcast_to(kmax_scr[8:9, :], (t, LANES))], axis=0)
    reach = jnp.sqrt(qn2 * kmax2) * 1.02
    b_hi = brange_ref[0, h]
    b_lo = brange_ref[1, h]
    fits = jnp.max(2.0 * reach + (b_hi - b_lo)) <= DA_SAFE_RANGE

    @pl.when(fits)
    def _():
        stab = reach + b_hi
        stab = jnp.concatenate([stab] * (t // LANES), axis=1)

        def block(j, bias=None, mask=None):
            s, vb = scores(j, bias, mask)
            return jnp.dot(jnp.exp2(s - stab).astype(BF16), vb, preferred_element_type=F32)

        def group_body(g, carry):
            tot = block(g * DA_UNROLL)
            for u in range(1, DA_UNROLL):
                tot = tot + block(g * DA_UNROLL + u)
            acc_scr[...] += tot
            return carry

        def single_body(j, carry):
            acc_scr[...] += block(j)
            return carry

        acc_scr[...] = jnp.zeros(acc_scr.shape, F32)
        n_grp = n_far // DA_UNROLL
        lax.fori_loop(0, n_grp, group_body, 0)
        lax.fori_loop(n_grp * DA_UNROLL, n_far, single_body, 0)
        near = block(n_far, bias=bias_ref[0, 1], mask=jnp.broadcast_to(i > 0, (t, t)))
        acc_scr[...] += near + block(i, bias=bias_ref[0, 0], mask=chunk_mask)

    @pl.when(jnp.logical_not(fits))
    def _():
        m_scr[...] = jnp.full(m_scr.shape, NEG_BIG, F32)
        acc_scr[...] = jnp.zeros(acc_scr.shape, F32)

        def step(j, bias=None, mask=None):
            s, vb = scores(j, bias, mask)
            m_prev = m_scr[...]
            m_next = jnp.maximum(m_prev, jnp.max(s, axis=1, keepdims=True))
            alpha = jnp.exp2(m_prev - m_next)
            p = jnp.exp2(s - jnp.concatenate([m_next] * (t // LANES), axis=1))
            pv = jnp.dot(p.astype(BF16), vb, preferred_element_type=F32)
            acc_scr[...] = jnp.concatenate([alpha, alpha], axis=1) * acc_scr[...] + pv
            m_scr[...] = m_next

        def far_body(j, carry):
            step(j)
            return carry

        lax.fori_loop(0, n_far, far_body, 0)

        @pl.when(i > 0)
        def _():
            step(i - 1, bias=bias_ref[0, 1])

        step(i, bias=bias_ref[0, 0], mask=chunk_mask)

    acc = acc_scr[...]
    o_all = acc[:, 0:LANES] / acc[:, LANES:2 * LANES]
    o = o_all[0:t] - lam_ref[0] * o_all[t:2 * t]
    ms = jnp.mean(o * o, axis=-1, keepdims=True)
    o_ref[0] = (o * lax.rsqrt(ms + EPS) * (g_ref[...] * out_scale)).astype(BF16)


def _diff_attention(proj3, lam, bias_range, bias_tiles, subln_row, out_scale):
    bsz, s_len, _ = proj3.shape
    t = DA_BLOCK
    nq = s_len // t
    kern = functools.partial(_da_kernel, out_scale=out_scale)
    return pl.pallas_call(
        kern,
        grid=(bsz, DA_HEADS, nq),
        in_specs=[
            pl.BlockSpec(memory_space=pltpu.SMEM),
            pl.BlockSpec(memory_space=pltpu.SMEM),
            pl.BlockSpec((1, t, LANES), lambda b, h, i: (b, i, h)),
            pl.BlockSpec((1, s_len, LANES), lambda b, h, i: (b, 0, DA_HEADS + h)),
            pl.BlockSpec((1, s_len, LANES), lambda b, h, i: (b, 0, 2 * DA_HEADS + h)),
            pl.BlockSpec((1, 2, t, t), lambda b, h, i: (h, 0, 0, 0)),
            pl.BlockSpec((1, LANES), lambda b, h, i: (0, 0)),
        ],
        out_specs=pl.BlockSpec((1, t, LANES), lambda b, h, i: (b, i, h)),
        out_shape=jax.ShapeDtypeStruct((bsz, s_len, DA_HEADS * LANES), BF16),
        scratch_shapes=[
            pltpu.VMEM((2 * t, LANES), F32),
            pltpu.VMEM((2 * t, 2 * LANES), F32),
            pltpu.VMEM((16, LANES), F32),
        ],
        compiler_params=_cparams(("parallel", "parallel", "arbitrary")),
        name="diff_attention",
    )(lam, bias_range, proj3, proj3, proj3, bias_tiles, subln_row)


def _sb_tile(qm, kb, vb, causal, tri, carry):
    z = lax.dot_general(qm, kb, (((1,), (1,)), ((), ())), preferred_element_type=F32)
    log_1m = -(jnp.maximum(z, 0.0) + jnp.log(1.0 + jnp.exp(-jnp.abs(z))))
    lm = log_1m if causal is None else jnp.where(causal, log_1m, 0.0)
    hi = lm.astype(BF16)
    mid = (lm - hi.astype(F32)).astype(BF16)
    inner = jnp.dot(hi, tri, preferred_element_type=F32) + jnp.dot(mid, tri, preferred_element_type=F32)
    logit = z + log_1m + inner
    if carry is not None:
        logit = logit + carry
    a = jnp.exp(logit)
    if causal is not None:
        a = jnp.where(causal, a, 0.0)
    pv = jnp.dot(a.astype(BF16), vb, preferred_element_type=F32)
    return pv, jnp.sum(lm, axis=1, keepdims=True)


def _sb_kernel(q_ref, k_ref, v_ref, g_ref, tri_ref, o_ref, acc_scr, carry_scr):
    sub, win = SB_SUB, SB_WIN
    i = pl.program_id(2)
    lane = lax.broadcasted_iota(jnp.int32, (1, LANES), 1)
    tri = tri_ref[...]
    row = lax.broadcasted_iota(jnp.int32, (sub, win), 0)
    col = lax.broadcasted_iota(jnp.int32, (sub, win), 1)
    n_sub = SB_ROWS // sub

    chains = []
    for u in range(n_sub):
        r0 = i * SB_ROWS + u * sub
        start = pl.multiple_of(jnp.maximum(r0 - sub, 0), sub)
        kw = k_ref[0, pl.ds(start, win), :]
        vw = v_ref[0, pl.ds(start, win), :]
        causal = (start + col) < (r0 + row)
        qu = q_ref[0, u * sub:(u + 1) * sub, :]
        for hh in range(2):
            c = 2 * u + hh
            qm = jnp.where((lane >= hh * SB_HEAD_DIM) & (lane < (hh + 1) * SB_HEAD_DIM), qu, jnp.zeros_like(qu))
            pv, total = _sb_tile(qm, kw, vw, causal, tri, None)
            acc_scr[c] = pv
            carry_scr[c] = jnp.broadcast_to(total, (sub, LANES))
            chains.append((c, qm, start // sub - 1, jnp.max(total)))

    tri_sub = tri[0:sub, 0:sub]
    for c, qm, j0, alive0 in chains:
        def cond(state):
            j, alive = state
            return (j >= 0) & (alive > SB_DEAD)

        def body(state, c=c, qm=qm):
            j, _ = state
            s0 = pl.multiple_of(j * sub, sub)
            kb = k_ref[0, pl.ds(s0, sub), :]
            vb = v_ref[0, pl.ds(s0, sub), :]
            carry = carry_scr[c]
            pv, total = _sb_tile(qm, kb, vb, None, tri_sub, carry)
            acc_scr[c] += pv
            carry = carry + total
            carry_scr[c] = carry
            return j - 1, jnp.max(carry)

        lax.while_loop(cond, body, (j0, alive0))

    for u in range(n_sub):
        o = jnp.where(lane < SB_HEAD_DIM, acc_scr[2 * u], acc_scr[2 * u + 1])
        sq = o * o
        ss0 = jnp.sum(jnp.where(lane < SB_HEAD_DIM, sq, 0.0), axis=1, keepdims=True)
        ss1 = jnp.sum(jnp.where(lane >= SB_HEAD_DIM, sq, 0.0), axis=1, keepdims=True)
        ms = jnp.where(lane < SB_HEAD_DIM, ss0, ss1) * (1.0 / SB_HEAD_DIM)
        o_ref[0, u * sub:(u + 1) * sub, :] = (o * lax.rsqrt(ms + EPS) * g_ref[...]).astype(BF16)


def _stick_breaking(proj3, sb_row, tri):
    bsz, s_len, _ = proj3.shape
    t = SB_ROWS
    nq = s_len // t
    pairs = SB_HEADS // 2
    q_blk = 3 * DA_HEADS
    n_chains = 2 * (SB_ROWS // SB_SUB)
    return pl.pallas_call(
        _sb_kernel,
        grid=(bsz, pairs, nq),
        in_specs=[
            pl.BlockSpec((1, t, LANES), lambda b, h, i: (b, i, q_blk + h)),
            pl.BlockSpec((1, s_len, LANES), lambda b, h, i: (b, 0, q_blk + pairs + h)),
            pl.BlockSpec((1, s_len, LANES), lambda b, h, i: (b, 0, q_blk + 2 * pairs + h)),
            pl.BlockSpec((1, LANES), lambda b, h, i: (0, 0)),
            pl.BlockSpec((SB_WIN, SB_WIN), lambda b, h, i: (0, 0)),
        ],
        out_specs=pl.BlockSpec((1, t, LANES), lambda b, h, i: (b, i, h)),
        out_shape=jax.ShapeDtypeStruct((bsz, s_len, pairs * LANES), BF16),
        scratch_shapes=[
            pltpu.VMEM((n_chains, SB_SUB, LANES), F32),
            pltpu.VMEM((n_chains, SB_SUB, LANES), F32),
        ],
        compiler_params=_cparams(("parallel", "parallel", "arbitrary")),
        name="stick_breaking",
    )(proj3, proj3, proj3, sb_row, tri)


def _split3(a):
    hi = a.astype(BF16)
    r1 = a - hi.astype(F32)
    mid = r1.astype(BF16)
    lo = (r1 - mid.astype(F32)).astype(BF16)
    return hi, mid, lo


def _out_proj_kernel(x_ref, oda_ref, osb_ref, wo_ref, g_ref, wr_ref, br_ref, x1_ref, h2_ref, ids_ref, gates_ref):
    half = oda_ref.shape[1]
    x1 = (x_ref[...]
          + jnp.dot(oda_ref[...], wo_ref[0:half, :], preferred_element_type=F32)
          + jnp.dot(osb_ref[...], wo_ref[half:2 * half, :], preferred_element_type=F32))
    x1_ref[...] = x1
    ms = jnp.mean(x1 * x1, axis=-1, keepdims=True)
    h2 = x1 * lax.rsqrt(ms + EPS) * g_ref[...]
    h2_ref[...] = h2

    a_hi, a_mid, a_lo = _split3(h2)
    w_hi, w_mid, w_lo = wr_ref[0], wr_ref[1], wr_ref[2]
    dot = lambda a, b: jnp.dot(a, b, preferred_element_type=F32)
    logits = (dot(a_hi, w_hi) + (dot(a_hi, w_mid) + dot(a_mid, w_hi))
              + (dot(a_hi, w_lo) + dot(a_mid, w_mid) + dot(a_lo, w_hi))) + br_ref[...]

    rows = logits.shape[0]
    lane = lax.broadcasted_iota(jnp.int32, (rows, LANES), 1)
    big = jnp.int32(LANES)
    is_g = lane < N_GROUPS
    lg = jnp.where(is_g, logits, NEG_BIG)
    mg = jnp.max(lg, axis=1, keepdims=True)
    gsel = jnp.min(jnp.where(lg == mg, lane, big), axis=1, keepdims=True)
    pg_sel = 1.0 / jnp.sum(jnp.where(is_g, jnp.exp(lg - mg), 0.0), axis=1, keepdims=True)
    lo_lane = N_GROUPS + gsel * EXPERTS_PER_GROUP
    in_grp = (lane >= lo_lane) & (lane < lo_lane + EXPERTS_PER_GROUP)
    le = jnp.where(in_grp, logits, NEG_BIG)
    l1 = jnp.max(le, axis=1, keepdims=True)
    i1 = jnp.min(jnp.where(le == l1, lane, big), axis=1, keepdims=True)
    le2 = jnp.where(lane == i1, NEG_BIG, le)
    l2 = jnp.max(le2, axis=1, keepdims=True)
    i2 = jnp.min(jnp.where(le2 == l2, lane, big), axis=1, keepdims=True)
    e2 = jnp.exp(l2 - l1)
    g1 = pg_sel / (1.0 + e2)
    g2 = pg_sel * e2 / (1.0 + e2)
    ids_ref[...] = jnp.where(lane == 0, i1 - N_GROUPS, jnp.where(lane == 1, i2 - N_GROUPS, 0))
    gates_ref[...] = jnp.where(lane == 0, g1, jnp.where(lane == 1, g2, 0.0))


def _out_proj(x2, o_da, o_sb, wo_bf, g_ffn, wr3, br):
    n, d = x2.shape
    half = o_da.shape[1]
    tm = PROJ_ROWS
    return pl.pallas_call(
        _out_proj_kernel,
        grid=(n // tm,),
        in_specs=[
            pl.BlockSpec((tm, d), lambda i: (i, 0)),
            pl.BlockSpec((tm, half), lambda i: (i, 0)),
            pl.BlockSpec((tm, half), lambda i: (i, 0)),
            pl.BlockSpec((2 * half, d), lambda i: (0, 0)),
            pl.BlockSpec((1, d), lambda i: (0, 0)),
            pl.BlockSpec((3, d, LANES), lambda i: (0, 0, 0)),
            pl.BlockSpec((1, LANES), lambda i: (0, 0)),
        ],
        out_specs=[
            pl.BlockSpec((tm, d), lambda i: (i, 0)),
            pl.BlockSpec((tm, d), lambda i: (i, 0)),
            pl.BlockSpec((tm, LANES), lambda i: (i, 0)),
            pl.BlockSpec((tm, LANES), lambda i: (i, 0)),
        ],
        out_shape=[
            jax.ShapeDtypeStruct((n, d), F32),
            jax.ShapeDtypeStruct((n, d), F32),
            jax.ShapeDtypeStruct((n, LANES), jnp.int32),
            jax.ShapeDtypeStruct((n, LANES), F32),
        ],
        compiler_params=_cparams(("parallel",)),
        name="out_proj_router",
    )(x2, o_da, o_sb, wo_bf, g_ffn, wr3, br)


def _dispatch_kernel(dest_ref, h_ref, xs_in_ref, xs_ref, sem):
    del xs_in_ref
    tt = h_ref.shape[0]

    def copy(r, k):
        return pltpu.make_async_copy(h_ref.at[pl.ds(r, 1)], xs_ref.at[pl.ds(dest_ref[0, 0, 2 * r + k], 1)], sem)

    def issue(r, c):
        copy(r, 0).start()
        copy(r, 1).start()
        return c

    lax.fori_loop(0, tt, issue, 0)

    def drain(r, c):
        copy(r, 0).wait()
        copy(r, 1).wait()
        return c

    lax.fori_loop(0, tt, drain, 0)


def _dispatch(dest2, h2, xs_init):
    n, d = h2.shape
    tt = TOK_TILE
    return pl.pallas_call(
        _dispatch_kernel,
        grid=(n // tt,),
        in_specs=[
            pl.BlockSpec((1, 1, 2 * tt), lambda i: (i, 0, 0), memory_space=pltpu.SMEM),
            pl.BlockSpec((tt, d), lambda i: (i, 0)),
            pl.BlockSpec(memory_space=pl.ANY),
        ],
        out_specs=pl.BlockSpec(memory_space=pl.ANY),
        out_shape=jax.ShapeDtypeStruct(xs_init.shape, xs_init.dtype),
        scratch_shapes=[pltpu.SemaphoreType.DMA(())],
        input_output_aliases={2: 0},
        compiler_params=_cparams(("arbitrary",)),
        name="moe_dispatch",
    )(dest2, h2, xs_init)


def _expert_kernel(blk_e_ref, n_used_ref, xs_ref, w1_ref, w3_ref, w2_ref, ys_ref):
    i = pl.program_id(0)

    @pl.when(i < n_used_ref[0])
    def _():
        xb = xs_ref[...].astype(BF16)
        a = jnp.dot(xb, w1_ref[0], preferred_element_type=F32)
        b = jnp.dot(xb, w3_ref[0], preferred_element_type=F32)
        hmid = (a * jax.nn.sigmoid(a) * b).astype(BF16)
        ys_ref[...] = jnp.dot(hmid, w2_ref[0], preferred_element_type=F32)

    @pl.when(i >= n_used_ref[0])
    def _():
        ys_ref[...] = jnp.zeros(ys_ref.shape, F32)


def _experts(blk_e, n_used, xs, w1_bf, w3_bf, w2_bf):
    cap, d = xs.shape
    hid = w1_bf.shape[2]
    nb = cap // MOE_BLOCK

    def row_map(i, blk_e_ref, n_used_ref):
        return (jnp.minimum(i, n_used_ref[0] - 1), 0)

    def w_map(i, blk_e_ref, n_used_ref):
        return (blk_e_ref[jnp.minimum(i, n_used_ref[0] - 1)], 0, 0)

    grid_spec = pltpu.PrefetchScalarGridSpec(
        num_scalar_prefetch=2,
        grid=(nb,),
        in_specs=[
            pl.BlockSpec((MOE_BLOCK, d), row_map),
            pl.BlockSpec((1, d, hid), w_map),
            pl.BlockSpec((1, d, hid), w_map),
            pl.BlockSpec((1, hid, d), w_map),
        ],
        out_specs=pl.BlockSpec((MOE_BLOCK, d), lambda i, blk_e_ref, n_used_ref: (i, 0)),
    )
    return pl.pallas_call(
        _expert_kernel,
        grid_spec=grid_spec,
        out_shape=jax.ShapeDtypeStruct((cap, d), F32),
        compiler_params=_cparams(("arbitrary",)),
        name="moe_experts",
    )(blk_e, n_used, xs, w1_bf, w3_bf, w2_bf)


def _combine_kernel(dest_ref, x1_ref, gates_ref, ys_ref, o_ref, y0_scr, y1_scr, sem):
    tt = x1_ref.shape[0]

    def copy(r, k):
        dst = y0_scr if k == 0 else y1_scr
        return pltpu.make_async_copy(ys_ref.at[pl.ds(dest_ref[0, 0, 2 * r + k], 1)], dst.at[pl.ds(r, 1)], sem)

    def issue(r, c):
        copy(r, 0).start()
        copy(r, 1).start()
        return c

    lax.fori_loop(0, tt, issue, 0)

    def drain(r, c):
        copy(r, 0).wait()
        copy(r, 1).wait()
        return c

    lax.fori_loop(0, tt, drain, 0)
    g = gates_ref[...]
    o_ref[...] = x1_ref[...] + y0_scr[...] * g[:, 0:1] + y1_scr[...] * g[:, 1:2]


def _combine(dest2, x1, gates, ys):
    n, d = x1.shape
    tt = TOK_TILE
    return pl.pallas_call(
        _combine_kernel,
        grid=(n // tt,),
        in_specs=[
            pl.BlockSpec((1, 1, 2 * tt), lambda i: (i, 0, 0), memory_space=pltpu.SMEM),
            pl.BlockSpec((tt, d), lambda i: (i, 0)),
            pl.BlockSpec((tt, LANES), lambda i: (i, 0)),
            pl.BlockSpec(memory_space=pl.ANY),
        ],
        out_specs=pl.BlockSpec((tt, d), lambda i: (i, 0)),
        out_shape=jax.ShapeDtypeStruct((n, d), F32),
        scratch_shapes=[
            pltpu.VMEM((tt, d), F32),
            pltpu.VMEM((tt, d), F32),
            pltpu.SemaphoreType.DMA(()),
        ],
        compiler_params=_cparams(("arbitrary",)),
        name="moe_combine",
    )(dest2, x1, gates, ys)


def _rel_bucket_np(rel):
    nb = REL_BUCKETS // 2
    max_exact = nb // 2
    base = np.where(rel > 0, nb, 0)
    n = np.abs(rel)
    nf = np.maximum(n, 1).astype(np.float64)
    large = max_exact + (np.log(nf / max_exact) / math.log(REL_MAX_DIST / max_exact) * (nb - max_exact)).astype(np.int64)
    large = np.minimum(large, nb - 1)
    return (base + np.where(n < max_exact, n, large)).astype(np.int32)


def _bias_bucket_tiles(t):
    r = np.arange(t)[:, None]
    c = np.arange(t)[None, :]
    far = _rel_bucket_np(-np.arange(t + 1, 4 * t))
    far_bucket = int(far[0])
    assert (far == far_bucket).all()
    return np.stack([_rel_bucket_np(c - r), _rel_bucket_np(c - r - t)]), far_bucket


def kernel(x, g_attn, w_in, qn_g, kn_g, lam_q1, lam_k1, lam_q2, lam_k2, subln_g, sb_out_g, rel_bias, w_o,
           g_ffn, w_router_g, b_router_g, w_router_e, b_router_e, w1, w3, w2):
    bsz, s_len, d = x.shape
    n = bsz * s_len
    depth = g_attn.shape[0]
    da_width = DA_HEADS * 2 * DA_HEAD_DIM
    sb_width = SB_HEADS * SB_HEAD_DIM

    bd = jnp.asarray(np.kron(np.eye(256 // DA_HEAD_DIM), np.ones((DA_HEAD_DIM, DA_HEAD_DIM))), BF16)
    tri = jnp.asarray(np.tril(np.ones((SB_WIN, SB_WIN)), -1), BF16)
    buckets, far_bucket = _bias_bucket_tiles(DA_BLOCK)

    for l in range(depth):
        lambda_init = 0.8 - 0.6 * math.exp(-0.3 * l)
        x2 = x.reshape(n, d)

        q_scale = DA_HEAD_DIM ** -0.5 * LOG2E
        gain_row = jnp.concatenate([
            jnp.tile(qn_g[l] * q_scale, 2 * DA_HEADS),
            jnp.tile(kn_g[l], 2 * DA_HEADS),
            jnp.ones((da_width,), F32),
            jnp.full((sb_width,), SB_HEAD_DIM ** -0.5, F32),
            jnp.ones((2 * sb_width,), F32),
        ])[None, :]
        proj = _in_proj(x2, g_attn[l][None, :], w_in[l].astype(BF16), gain_row, bd)
        proj3 = proj.reshape(bsz, s_len, -1)

        lam = (jnp.exp(jnp.sum(lam_q1[l] * lam_k1[l])) - jnp.exp(jnp.sum(lam_q2[l] * lam_k2[l]))
               + lambda_init).astype(F32).reshape(1)
        rb = (rel_bias - rel_bias[far_bucket][None, :]) * LOG2E
        bias_tiles = jnp.transpose(rb[buckets], (3, 0, 1, 2)).astype(F32)
        bias_range = jnp.stack([jnp.maximum(jnp.max(rb, axis=0), 0.0), jnp.minimum(jnp.min(rb, axis=0), 0.0)])
        o_da = _diff_attention(proj3, lam, bias_range.astype(F32), bias_tiles, subln_g[l][None, :],
                               1.0 - lambda_init)
        o_sb = _stick_breaking(proj3, jnp.tile(sb_out_g[l], 2)[None, :], tri)

        wr = jnp.concatenate([w_router_g[l], w_router_e[l],
                              jnp.zeros((d, LANES - N_GROUPS - N_EXPERTS), F32)], axis=1)
        wr_hi = wr.astype(BF16)
        wr_r1 = wr - wr_hi.astype(F32)
        wr_mid = wr_r1.astype(BF16)
        wr_lo = (wr_r1 - wr_mid.astype(F32)).astype(BF16)
        br = jnp.concatenate([b_router_g[l], b_router_e[l],
                              jnp.zeros((LANES - N_GROUPS - N_EXPERTS,), F32)])[None, :]
        x1, h2, ids, gates = _out_proj(x2, o_da.reshape(n, -1), o_sb.reshape(n, -1), w_o[l].astype(BF16),
                                       g_ffn[l][None, :], jnp.stack([wr_hi, wr_mid, wr_lo]), br)

        eid = ids[:, 0:2].reshape(-1)
        onehot = (eid[:, None] == jnp.arange(N_EXPERTS, dtype=jnp.int32)[None, :]).astype(jnp.int32)
        csum = jnp.cumsum(onehot, axis=0)
        counts = csum[-1]
        padded = (counts + MOE_BLOCK - 1) // MOE_BLOCK * MOE_BLOCK
        pad_ends = jnp.cumsum(padded)
        pad_starts = pad_ends - padded
        dest = jnp.sum(onehot * (csum - 1 + pad_starts[None, :]), axis=1).astype(jnp.int32)
        cap = 2 * n + N_EXPERTS * MOE_BLOCK
        nb = cap // MOE_BLOCK
        blk_e = jnp.minimum(jnp.searchsorted(pad_ends, jnp.arange(nb, dtype=jnp.int32) * MOE_BLOCK, side='right'),
                            N_EXPERTS - 1).astype(jnp.int32)
        n_used = (pad_ends[-1] // MOE_BLOCK).astype(jnp.int32).reshape(1)
        dest2 = dest.reshape(n // TOK_TILE, 1, 2 * TOK_TILE)

        xs = _dispatch(dest2, h2, jnp.zeros((cap, d), F32))
        ys = _experts(blk_e, n_used, xs, w1[l].astype(BF16), w3[l].astype(BF16), w2[l].astype(BF16))
        x = _combine(dest2, x1, gates, ys).reshape(bsz, s_len, d)
    return x
```

```python
import functools
import math

import jax
import jax.numpy as jnp
import numpy as np
from jax import lax
from jax.experimental import pallas as pl
from jax.experimental.pallas import tpu as pltpu

F32 = jnp.float32
BF16 = jnp.bfloat16

EPS = 1e-6
LANES = 128
DA_HEADS = 4
DA_HEAD_DIM = 64
SB_HEADS = 8
SB_HEAD_DIM = 64
CHUNK = 64
REL_BUCKETS = 32
REL_MAX_DIST = 128
N_GROUPS = 4
EXPERTS_PER_GROUP = 8
N_EXPERTS = N_GROUPS * EXPERTS_PER_GROUP
NEG_BIG = -1e30
LOG2E = math.log2(math.e)
SB_DEAD = -87.5

VMEM_LIMIT = 48 * 1024 * 1024

PROJ_ROWS = 512
DA_BLOCK = 256
DA_UNROLL = 4
DA_SAFE_RANGE = 100.0
SB_SUB = 128
SB_ROWS = 256
SB_WIN = 256
MOE_BLOCK = 256
TOK_TILE = 256
ROW_DMA_UNROLL = 8


def _cparams(sem):
    return pltpu.CompilerParams(dimension_semantics=sem, vmem_limit_bytes=VMEM_LIMIT)


def _in_proj_kernel(x_ref, g_ref, w_ref, gain_ref, bd_ref, o_ref):
    x = x_ref[...]
    ms = jnp.mean(x * x, axis=-1, keepdims=True)
    h = (x * lax.rsqrt(ms + EPS) * g_ref[...]).astype(BF16)
    n_chunks = o_ref.shape[1] // 512
    for c in range(n_chunks):
        cols = slice(c * 512, (c + 1) * 512)
        acc = jnp.dot(h, w_ref[:, cols], preferred_element_type=F32)
        if c < 2:
            sq = (acc * acc).astype(BF16)
            parts = []
            for s in range(2):
                ss = jnp.dot(sq[:, s * 256:(s + 1) * 256], bd_ref[...], preferred_element_type=F32)
                parts.append(ss)
            ss = jnp.concatenate(parts, axis=1)
            acc = acc * lax.rsqrt(ss * (1.0 / DA_HEAD_DIM) + EPS)
        o_ref[:, cols] = (acc * gain_ref[:, cols]).astype(BF16)


def _in_proj(x2, g_attn, w_in_bf, gain_row, bd):
    n, d = x2.shape
    width = w_in_bf.shape[1]
    return pl.pallas_call(
        _in_proj_kernel,
        grid=(n // PROJ_ROWS,),
        in_specs=[
            pl.BlockSpec((PROJ_ROWS, d), lambda i: (i, 0)),
            pl.BlockSpec((1, d), lambda i: (0, 0)),
            pl.BlockSpec((d, width), lambda i: (0, 0)),
            pl.BlockSpec((1, width), lambda i: (0, 0)),
            pl.BlockSpec((256, 256), lambda i: (0, 0)),
        ],
        out_specs=pl.BlockSpec((PROJ_ROWS, width), lambda i: (i, 0)),
        out_shape=jax.ShapeDtypeStruct((n, width), BF16),
        compiler_params=_cparams(("parallel",)),
        name="in_proj",
    )(x2, g_attn, w_in_bf, gain_row, bd)


def _da_kernel(lam_ref, brange_ref, q_ref, k_ref, v_ref, bias_ref, g_ref, o_ref, m_scr, acc_scr, kmax_scr, *,
               out_scale):
    t = DA_BLOCK
    h = pl.program_id(1)
    i = pl.program_id(2)
    s_len = k_ref.shape[1]
    ones_col = jnp.ones((LANES, LANES), BF16)

    @pl.when(i == 0)
    def _():
        lane_k = lax.broadcasted_iota(jnp.int32, (1, LANES), 1)
        rows = 512

        def kbody(c, best):
            kc = k_ref[0, pl.ds(pl.multiple_of(c * rows, rows), rows), :].astype(F32)
            sq = kc * kc
            n1 = jnp.dot(jnp.where(lane_k < DA_HEAD_DIM, sq, 0.0).astype(BF16), ones_col, preferred_element_type=F32)
            n2 = jnp.dot(jnp.where(lane_k >= DA_HEAD_DIM, sq, 0.0).astype(BF16), ones_col, preferred_element_type=F32)
            return (jnp.maximum(best[0], jnp.max(n1, axis=0, keepdims=True)),
                    jnp.maximum(best[1], jnp.max(n2, axis=0, keepdims=True)))

        zero_row = jnp.zeros((1, LANES), F32)
        k1, k2 = lax.fori_loop(0, s_len // rows, kbody, (zero_row, zero_row))
        kmax_scr[0:8, :] = jnp.broadcast_to(k1, (8, LANES))
        kmax_scr[8:16, :] = jnp.broadcast_to(k2, (8, LANES))

    q = q_ref[0]
    lane = lax.broadcasted_iota(jnp.int32, (1, LANES), 1)
    zero = jnp.zeros_like(q)
    qq = jnp.concatenate([jnp.where(lane < DA_HEAD_DIM, q, zero),
                          jnp.where(lane >= DA_HEAD_DIM, q, zero)], axis=0)
    ones = jnp.ones((t, LANES), BF16)
    row = lax.broadcasted_iota(jnp.int32, (t, t), 0)
    col = lax.broadcasted_iota(jnp.int32, (t, t), 1)
    chunk_mask = (col // CHUNK) <= (row // CHUNK)
    n_far = jnp.maximum(i - 1, 0)

    def scores(j, bias, mask):
        start = pl.multiple_of(j * t, t)
        kb = k_ref[0, pl.ds(start, t), :]
        vb = jnp.concatenate([v_ref[0, pl.ds(start, t), :], ones], axis=1)
        s = lax.dot_general(qq, kb, (((1,), (1,)), ((), ())), preferred_element_type=F32)
        if bias is not None:
            s = s + jnp.concatenate([bias, bias], axis=0)
        if mask is not None:
            s = jnp.where(jnp.concatenate([mask, mask], axis=0), s, NEG_BIG)
        return s, vb

    qf = qq.astype(F32)
    qn2 = jnp.dot((qf * qf).astype(BF16), ones_col, preferred_element_type=F32)
    kmax2 = jnp.concatenate([jnp.broadcast_to(kmax_scr[0:1, :], (t, LANES)),
                             jnp.broadcast_to(kmax_scr[8:9, :], (t, LANES))], axis=0)
    reach = jnp.sqrt(qn2 * kmax2) * 1.02
    b_hi = brange_ref[0, h]
    b_lo = brange_ref[1, h]
    fits = jnp.max(2.0 * reach + (b_hi - b_lo)) <= DA_SAFE_RANGE

    @pl.when(fits)
    def _():
        stab = reach + b_hi
        stab = jnp.concatenate([stab] * (t // LANES), axis=1)

        def block(j, bias=None, mask=None):
            s, vb = scores(j, bias, mask)
            return jnp.dot(jnp.exp2(s - stab).astype(BF16), vb, preferred_element_type=F32)

        def group_body(g, carry):
            tot = block(g * DA_UNROLL)
            for u in range(1, DA_UNROLL):
                tot = tot + block(g * DA_UNROLL + u)
            acc_scr[...] += tot
            return carry

        def single_body(j, carry):
            acc_scr[...] += block(j)
            return carry

        acc_scr[...] = jnp.zeros(acc_scr.shape, F32)
        n_grp = n_far // DA_UNROLL
        lax.fori_loop(0, n_grp, group_body, 0)
        lax.fori_loop(n_grp * DA_UNROLL, n_far, single_body, 0)
        near = block(n_far, bias=bias_ref[0, 1], mask=jnp.broadcast_to(i > 0, (t, t)))
        acc_scr[...] += near + block(i, bias=bias_ref[0, 0], mask=chunk_mask)

    @pl.when(jnp.logical_not(fits))
    def _():
        m_scr[...] = jnp.full(m_scr.shape, NEG_BIG, F32)
        acc_scr[...] = jnp.zeros(acc_scr.shape, F32)

        def step(j, bias=None, mask=None):
            s, vb = scores(j, bias, mask)
            m_prev = m_scr[...]
            m_next = jnp.maximum(m_prev, jnp.max(s, axis=1, keepdims=True))
            alpha = jnp.exp2(m_prev - m_next)
            p = jnp.exp2(s - jnp.concatenate([m_next] * (t // LANES), axis=1))
            pv = jnp.dot(p.astype(BF16), vb, preferred_element_type=F32)
            acc_scr[...] = jnp.concatenate([alpha, alpha], axis=1) * acc_scr[...] + pv
            m_scr[...] = m_next

        def far_body(j, carry):
            step(j)
            return carry

        lax.fori_loop(0, n_far, far_body, 0)

        @pl.when(i > 0)
        def _():
            step(i - 1, bias=bias_ref[0, 1])

        step(i, bias=bias_ref[0, 0], mask=chunk_mask)

    acc = acc_scr[...]
    o_all = acc[:, 0:LANES] / acc[:, LANES:2 * LANES]
    o = o_all[0:t] - lam_ref[0] * o_all[t:2 * t]
    ms = jnp.mean(o * o, axis=-1, keepdims=True)
    o_ref[0] = (o * lax.rsqrt(ms + EPS) * (g_ref[...] * out_scale)).astype(BF16)


def _diff_attention(proj3, lam, bias_range, bias_tiles, subln_row, out_scale):
    bsz, s_len, _ = proj3.shape
    t = DA_BLOCK
    nq = s_len // t
    kern = functools.partial(_da_kernel, out_scale=out_scale)
    return pl.pallas_call(
        kern,
        grid=(bsz, DA_HEADS, nq),
        in_specs=[
            pl.BlockSpec(memory_space=pltpu.SMEM),
            pl.BlockSpec(memory_space=pltpu.SMEM),
            pl.BlockSpec((1, t, LANES), lambda b, h, i: (b, i, h)),
            pl.BlockSpec((1, s_len, LANES), lambda b, h, i: (b, 0, DA_HEADS + h)),
            pl.BlockSpec((1, s_len, LANES), lambda b, h, i: (b, 0, 2 * DA_HEADS + h)),
            pl.BlockSpec((1, 2, t, t), lambda b, h, i: (h, 0, 0, 0)),
            pl.BlockSpec((1, LANES), lambda b, h, i: (0, 0)),
        ],
        out_specs=pl.BlockSpec((1, t, LANES), lambda b, h, i: (b, i, h)),
        out_shape=jax.ShapeDtypeStruct((bsz, s_len, DA_HEADS * LANES), BF16),
        scratch_shapes=[
            pltpu.VMEM((2 * t, LANES), F32),
            pltpu.VMEM((2 * t, 2 * LANES), F32),
            pltpu.VMEM((16, LANES), F32),
        ],
        compiler_params=_cparams(("parallel", "parallel", "arbitrary")),
        name="diff_attention",
    )(lam, bias_range, proj3, proj3, proj3, bias_tiles, subln_row)


def _sb_tile(qm, kb, vb, causal, tri, carry):
    z = lax.dot_general(qm, kb, (((1,), (1,)), ((), ())), preferred_element_type=F32)
    log_1m = -(jnp.maximum(z, 0.0) + jnp.log(1.0 + jnp.exp(-jnp.abs(z))))
    lm = log_1m if causal is None else jnp.where(causal, log_1m, 0.0)
    hi = lm.astype(BF16)
    mid = (lm - hi.astype(F32)).astype(BF16)
    inner = jnp.dot(hi, tri, preferred_element_type=F32) + jnp.dot(mid, tri, preferred_element_type=F32)
    logit = z + log_1m + inner
    if carry is not None:
        logit = logit + carry
    a = jnp.exp(logit)
    if causal is not None:
        a = jnp.where(causal, a, 0.0)
    pv = jnp.dot(a.astype(BF16), vb, preferred_element_type=F32)
    return pv, jnp.sum(lm, axis=1, keepdims=True)


def _sb_kernel(q_ref, k_ref, v_ref, g_ref, tri_ref, o_ref, acc_scr, carry_scr):
    sub, win = SB_SUB, SB_WIN
    i = pl.program_id(2)
    lane = lax.broadcasted_iota(jnp.int32, (1, LANES), 1)
    tri = tri_ref[...]
    row = lax.broadcasted_iota(jnp.int32, (sub, win), 0)
    col = lax.broadcasted_iota(jnp.int32, (sub, win), 1)
    n_sub = SB_ROWS // sub

    chains = []
    for u in range(n_sub):
        r0 = i * SB_ROWS + u * sub
        start = pl.multiple_of(jnp.maximum(r0 - sub, 0), sub)
        kw = k_ref[0, pl.ds(start, win), :]
        vw = v_ref[0, pl.ds(start, win), :]
        causal = (start + col) < (r0 + row)
        qu = q_ref[0, u * sub:(u + 1) * sub, :]
        for hh in range(2):
            c = 2 * u + hh
            qm = jnp.where((lane >= hh * SB_HEAD_DIM) & (lane < (hh + 1) * SB_HEAD_DIM), qu, jnp.zeros_like(qu))
            pv, total = _sb_tile(qm, kw, vw, causal, tri, None)
            acc_scr[c] = pv
            carry_scr[c] = jnp.broadcast_to(total, (sub, LANES))
            chains.append((c, qm, start // sub - 1, jnp.max(total)))

    tri_sub = tri[0:sub, 0:sub]
    for c, qm, j0, alive0 in chains:
        def cond(state):
            j, alive = state
            return (j >= 0) & (alive > SB_DEAD)

        def body(state, c=c, qm=qm):
            j, _ = state
            s0 = pl.multiple_of(j * sub, sub)
            kb = k_ref[0, pl.ds(s0, sub), :]
            vb = v_ref[0, pl.ds(s0, sub), :]
            carry = carry_scr[c]
            pv, total = _sb_tile(qm, kb, vb, None, tri_sub, carry)
            acc_scr[c] += pv
            carry = carry + total
            carry_scr[c] = carry
            return j - 1, jnp.max(carry)

        lax.while_loop(cond, body, (j0, alive0))

    for u in range(n_sub):
        o = jnp.where(lane < SB_HEAD_DIM, acc_scr[2 * u], acc_scr[2 * u + 1])
        sq = o * o
        ss0 = jnp.sum(jnp.where(lane < SB_HEAD_DIM, sq, 0.0), axis=1, keepdims=True)
        ss1 = jnp.sum(jnp.where(lane >= SB_HEAD_DIM, sq, 0.0), axis=1, keepdims=True)
        ms = jnp.where(lane < SB_HEAD_DIM, ss0, ss1) * (1.0 / SB_HEAD_DIM)
        o_ref[0, u * sub:(u + 1) * sub, :] = (o * lax.rsqrt(ms + EPS) * g_ref[...]).astype(BF16)


def _stick_breaking(proj3, sb_row, tri):
    bsz, s_len, _ = proj3.shape
    t = SB_ROWS
    nq = s_len // t
    pairs = SB_HEADS // 2
    q_blk = 3 * DA_HEADS
    n_chains = 2 * (SB_ROWS // SB_SUB)
    return pl.pallas_call(
        _sb_kernel,
        grid=(bsz, pairs, nq),
        in_specs=[
            pl.BlockSpec((1, t, LANES), lambda b, h, i: (b, i, q_blk + h)),
            pl.BlockSpec((1, s_len, LANES), lambda b, h, i: (b, 0, q_blk + pairs + h)),
            pl.BlockSpec((1, s_len, LANES), lambda b, h, i: (b, 0, q_blk + 2 * pairs + h)),
            pl.BlockSpec((1, LANES), lambda b, h, i: (0, 0)),
            pl.BlockSpec((SB_WIN, SB_WIN), lambda b, h, i: (0, 0)),
        ],
        out_specs=pl.BlockSpec((1, t, LANES), lambda b, h, i: (b, i, h)),
        out_shape=jax.ShapeDtypeStruct((bsz, s_len, pairs * LANES), BF16),
        scratch_shapes=[
            pltpu.VMEM((n_chains, SB_SUB, LANES), F32),
            pltpu.VMEM((n_chains, SB_SUB, LANES), F32),
        ],
        compiler_params=_cparams(("parallel", "parallel", "arbitrary")),
        name="stick_breaking",
    )(proj3, proj3, proj3, sb_row, tri)


def _split3(a):
    hi = a.astype(BF16)
    r1 = a - hi.astype(F32)
    mid = r1.astype(BF16)
    lo = (r1 - mid.astype(F32)).astype(BF16)
    return hi, mid, lo


def _out_proj_kernel(x_ref, oda_ref, osb_ref, wo_ref, g_ref, wr_ref, br_ref, x1_ref, h2_ref, ids_ref, gates_ref):
    half = oda_ref.shape[1]
    x1 = (x_ref[...]
          + jnp.dot(oda_ref[...], wo_ref[0:half, :], preferred_element_type=F32)
          + jnp.dot(osb_ref[...], wo_ref[half:2 * half, :], preferred_element_type=F32))
    x1_ref[...] = x1
    ms = jnp.mean(x1 * x1, axis=-1, keepdims=True)
    h2 = x1 * lax.rsqrt(ms + EPS) * g_ref[...]
    d_half = h2.shape[1] // 2
    hi_bits = lax.bitcast_convert_type(h2[:, :d_half].astype(BF16).astype(F32), jnp.uint32)
    lo_bits = lax.bitcast_convert_type(h2[:, d_half:].astype(BF16).astype(F32), jnp.uint32)
    h2_ref[...] = hi_bits | (lo_bits >> 16)

    a_hi, a_mid, a_lo = _split3(h2)
    w_hi, w_mid, w_lo = wr_ref[0], wr_ref[1], wr_ref[2]
    dot = lambda a, b: jnp.dot(a, b, preferred_element_type=F32)
    logits = (dot(a_hi, w_hi) + (dot(a_hi, w_mid) + dot(a_mid, w_hi))
              + (dot(a_hi, w_lo) + dot(a_mid, w_mid) + dot(a_lo, w_hi))) + br_ref[...]

    rows = logits.shape[0]
    lane = lax.broadcasted_iota(jnp.int32, (rows, LANES), 1)
    big = jnp.int32(LANES)
    is_g = lane < N_GROUPS
    lg = jnp.where(is_g, logits, NEG_BIG)
    mg = jnp.max(lg, axis=1, keepdims=True)
    gsel = jnp.min(jnp.where(lg == mg, lane, big), axis=1, keepdims=True)
    pg_sel = 1.0 / jnp.sum(jnp.where(is_g, jnp.exp(lg - mg), 0.0), axis=1, keepdims=True)
    lo_lane = N_GROUPS + gsel * EXPERTS_PER_GROUP
    in_grp = (lane >= lo_lane) & (lane < lo_lane + EXPERTS_PER_GROUP)
    le = jnp.where(in_grp, logits, NEG_BIG)
    l1 = jnp.max(le, axis=1, keepdims=True)
    i1 = jnp.min(jnp.where(le == l1, lane, big), axis=1, keepdims=True)
    le2 = jnp.where(lane == i1, NEG_BIG, le)
    l2 = jnp.max(le2, axis=1, keepdims=True)
    i2 = jnp.min(jnp.where(le2 == l2, lane, big), axis=1, keepdims=True)
    e2 = jnp.exp(l2 - l1)
    g1 = pg_sel / (1.0 + e2)
    g2 = pg_sel * e2 / (1.0 + e2)
    ids_ref[...] = jnp.where(lane == 0, i1 - N_GROUPS, jnp.where(lane == 1, i2 - N_GROUPS, 0))
    gates_ref[...] = jnp.where(lane == 0, g1, jnp.where(lane == 1, g2, 0.0))


def _out_proj(x2, o_da, o_sb, wo_bf, g_ffn, wr3, br):
    n, d = x2.shape
    half = o_da.shape[1]
    tm = PROJ_ROWS
    return pl.pallas_call(
        _out_proj_kernel,
        grid=(n // tm,),
        in_specs=[
            pl.BlockSpec((tm, d), lambda i: (i, 0)),
            pl.BlockSpec((tm, half), lambda i: (i, 0)),
            pl.BlockSpec((tm, half), lambda i: (i, 0)),
            pl.BlockSpec((2 * half, d), lambda i: (0, 0)),
            pl.BlockSpec((1, d), lambda i: (0, 0)),
            pl.BlockSpec((3, d, LANES), lambda i: (0, 0, 0)),
            pl.BlockSpec((1, LANES), lambda i: (0, 0)),
        ],
        out_specs=[
            pl.BlockSpec((tm, d), lambda i: (i, 0)),
            pl.BlockSpec((tm, d // 2), lambda i: (i, 0)),
            pl.BlockSpec((tm, LANES), lambda i: (i, 0)),
            pl.BlockSpec((tm, LANES), lambda i: (i, 0)),
        ],
        out_shape=[
            jax.ShapeDtypeStruct((n, d), F32),
            jax.ShapeDtypeStruct((n, d // 2), jnp.uint32),
            jax.ShapeDtypeStruct((n, LANES), jnp.int32),
            jax.ShapeDtypeStruct((n, LANES), F32),
        ],
        compiler_params=_cparams(("parallel",)),
        name="out_proj_router",
    )(x2, o_da, o_sb, wo_bf, g_ffn, wr3, br)


def _dispatch_kernel(dest_ref, h_ref, xs_in_ref, xs_ref, sem):
    del xs_in_ref
    tt = h_ref.shape[0]

    def copy(r, k):
        return pltpu.make_async_copy(h_ref.at[pl.ds(r, 1)], xs_ref.at[pl.ds(dest_ref[0, 0, 2 * r + k], 1)], sem)

    def issue(r, c):
        copy(r, 0).start()
        copy(r, 1).start()
        return c

    lax.fori_loop(0, tt, issue, 0, unroll=ROW_DMA_UNROLL)

    def drain(r, c):
        copy(r, 0).wait()
        copy(r, 1).wait()
        return c

    lax.fori_loop(0, tt, drain, 0, unroll=True)


def _dispatch(dest2, h2, xs_init):
    n, d = h2.shape
    tt = TOK_TILE
    return pl.pallas_call(
        _dispatch_kernel,
        grid=(n // tt,),
        in_specs=[
            pl.BlockSpec((1, 1, 2 * tt), lambda i: (i, 0, 0), memory_space=pltpu.SMEM),
            pl.BlockSpec((tt, d), lambda i: (i, 0)),
            pl.BlockSpec(memory_space=pl.ANY),
        ],
        out_specs=pl.BlockSpec(memory_space=pl.ANY),
        out_shape=jax.ShapeDtypeStruct(xs_init.shape, xs_init.dtype),
        scratch_shapes=[pltpu.SemaphoreType.DMA(())],
        input_output_aliases={2: 0},
        compiler_params=_cparams(("arbitrary",)),
        name="moe_dispatch",
    )(dest2, h2, xs_init)


def _expert_kernel(blk_e_ref, n_used_ref, xs_ref, w1_ref, w3_ref, w2_ref, ys_ref):
    i = pl.program_id(0)

    @pl.when(i < n_used_ref[0])
    def _():
        words = xs_ref[...]
        xb = jnp.concatenate([
            lax.bitcast_convert_type(words & jnp.uint32(0xFFFF0000), F32).astype(BF16),
            lax.bitcast_convert_type(words << 16, F32).astype(BF16)], axis=1)
        a = jnp.dot(xb, w1_ref[0], preferred_element_type=F32)
        b = jnp.dot(xb, w3_ref[0], preferred_element_type=F32)
        hmid = (a * jax.nn.sigmoid(a) * b).astype(BF16)
        ys_ref[...] = jnp.dot(hmid, w2_ref[0], preferred_element_type=F32)

    @pl.when(i >= n_used_ref[0])
    def _():
        ys_ref[...] = jnp.zeros(ys_ref.shape, F32)


def _experts(blk_e, n_used, xs, w1_bf, w3_bf, w2_bf):
    cap = xs.shape[0]
    d, hid = w1_bf.shape[1], w1_bf.shape[2]
    nb = cap // MOE_BLOCK

    def row_map(i, blk_e_ref, n_used_ref):
        return (jnp.minimum(i, n_used_ref[0] - 1), 0)

    def w_map(i, blk_e_ref, n_used_ref):
        return (blk_e_ref[jnp.minimum(i, n_used_ref[0] - 1)], 0, 0)

    grid_spec = pltpu.PrefetchScalarGridSpec(
        num_scalar_prefetch=2,
        grid=(nb,),
        in_specs=[
            pl.BlockSpec((MOE_BLOCK, d // 2), row_map),
            pl.BlockSpec((1, d, hid), w_map),
            pl.BlockSpec((1, d, hid), w_map),
            pl.BlockSpec((1, hid, d), w_map),
        ],
        out_specs=pl.BlockSpec((MOE_BLOCK, d), lambda i, blk_e_ref, n_used_ref: (i, 0)),
    )
    return pl.pallas_call(
        _expert_kernel,
        grid_spec=grid_spec,
        out_shape=jax.ShapeDtypeStruct((cap, d), F32),
        compiler_params=_cparams(("arbitrary",)),
        name="moe_experts",
    )(blk_e, n_used, xs, w1_bf, w3_bf, w2_bf)


def _combine_kernel(dest_ref, x1_ref, gates_ref, ys_ref, o_ref, y0_scr, y1_scr, sem):
    tt = x1_ref.shape[0]

    def copy(r, k):
        dst = y0_scr if k == 0 else y1_scr
        return pltpu.make_async_copy(ys_ref.at[pl.ds(dest_ref[0, 0, 2 * r + k], 1)], dst.at[pl.ds(r, 1)], sem)

    def issue(r, c):
        copy(r, 0).start()
        copy(r, 1).start()
        return c

    lax.fori_loop(0, tt, issue, 0, unroll=ROW_DMA_UNROLL)

    def drain(r, c):
        copy(r, 0).wait()
        copy(r, 1).wait()
        return c

    lax.fori_loop(0, tt, drain, 0, unroll=True)
    g = gates_ref[...]
    o_ref[...] = x1_ref[...] + y0_scr[...] * g[:, 0:1] + y1_scr[...] * g[:, 1:2]


def _combine(dest2, x1, gates, ys):
    n, d = x1.shape
    tt = TOK_TILE
    return pl.pallas_call(
        _combine_kernel,
        grid=(n // tt,),
        in_specs=[
            pl.BlockSpec((1, 1, 2 * tt), lambda i: (i, 0, 0), memory_space=pltpu.SMEM),
            pl.BlockSpec((tt, d), lambda i: (i, 0)),
            pl.BlockSpec((tt, LANES), lambda i: (i, 0)),
            pl.BlockSpec(memory_space=pl.ANY),
        ],
        out_specs=pl.BlockSpec((tt, d), lambda i: (i, 0)),
        out_shape=jax.ShapeDtypeStruct((n, d), F32),
        scratch_shapes=[
            pltpu.VMEM((tt, d), F32),
            pltpu.VMEM((tt, d), F32),
            pltpu.SemaphoreType.DMA(()),
        ],
        compiler_params=_cparams(("arbitrary",)),
        name="moe_combine",
    )(dest2, x1, gates, ys)


def _rel_bucket_np(rel):
    nb = REL_BUCKETS // 2
    max_exact = nb // 2
    base = np.where(rel > 0, nb, 0)
    n = np.abs(rel)
    nf = np.maximum(n, 1).astype(np.float64)
    large = max_exact + (np.log(nf / max_exact) / math.log(REL_MAX_DIST / max_exact) * (nb - max_exact)).astype(np.int64)
    large = np.minimum(large, nb - 1)
    return (base + np.where(n < max_exact, n, large)).astype(np.int32)


def _bias_bucket_rows(t):
    far = _rel_bucket_np(-np.arange(t + 1, 4 * t))
    far_bucket = int(far[0])
    assert (far == far_bucket).all()
    m = np.arange(2 * t)
    wrap = np.where(m < t, m, m - 2 * t)
    return np.stack([_rel_bucket_np(wrap), _rel_bucket_np(wrap - t)]), far_bucket


def _toeplitz(rows, t):
    lead = rows.shape[:-1]
    flat = jnp.tile(rows, (1,) * len(lead) + (t,))[..., : t * (2 * t - 1)]
    return flat.reshape(lead + (t, 2 * t - 1))[..., :t]


def kernel(x, g_attn, w_in, qn_g, kn_g, lam_q1, lam_k1, lam_q2, lam_k2, subln_g, sb_out_g, rel_bias, w_o,
           g_ffn, w_router_g, b_router_g, w_router_e, b_router_e, w1, w3, w2):
    bsz, s_len, d = x.shape
    n = bsz * s_len
    depth = g_attn.shape[0]
    da_width = DA_HEADS * 2 * DA_HEAD_DIM
    sb_width = SB_HEADS * SB_HEAD_DIM

    bd = jnp.asarray(np.kron(np.eye(256 // DA_HEAD_DIM), np.ones((DA_HEAD_DIM, DA_HEAD_DIM))), BF16)
    tri = jnp.asarray(np.tril(np.ones((SB_WIN, SB_WIN)), -1), BF16)
    bucket_rows, far_bucket = _bias_bucket_rows(DA_BLOCK)

    for l in range(depth):
        lambda_init = 0.8 - 0.6 * math.exp(-0.3 * l)
        x2 = x.reshape(n, d)

        q_scale = DA_HEAD_DIM ** -0.5 * LOG2E
        gain_row = jnp.concatenate([
            jnp.tile(qn_g[l] * q_scale, 2 * DA_HEADS),
            jnp.tile(kn_g[l], 2 * DA_HEADS),
            jnp.ones((da_width,), F32),
            jnp.full((sb_width,), SB_HEAD_DIM ** -0.5, F32),
            jnp.ones((2 * sb_width,), F32),
        ])[None, :]
        proj = _in_proj(x2, g_attn[l][None, :], w_in[l].astype(BF16), gain_row, bd)
        proj3 = proj.reshape(bsz, s_len, -1)

        lam = (jnp.exp(jnp.sum(lam_q1[l] * lam_k1[l])) - jnp.exp(jnp.sum(lam_q2[l] * lam_k2[l]))
               + lambda_init).astype(F32).reshape(1)
        rb = (rel_bias - rel_bias[far_bucket][None, :]) * LOG2E
        bias_rows = jnp.transpose(rb[bucket_rows], (2, 0, 1)).astype(F32)
        bias_tiles = _toeplitz(bias_rows, DA_BLOCK)
        bias_range = jnp.stack([jnp.maximum(jnp.max(rb, axis=0), 0.0), jnp.minimum(jnp.min(rb, axis=0), 0.0)])
        o_da = _diff_attention(proj3, lam, bias_range.astype(F32), bias_tiles, subln_g[l][None, :],
                               1.0 - lambda_init)
        o_sb = _stick_breaking(proj3, jnp.tile(sb_out_g[l], 2)[None, :], tri)

        wr = jnp.concatenate([w_router_g[l], w_router_e[l],
                              jnp.zeros((d, LANES - N_GROUPS - N_EXPERTS), F32)], axis=1)
        wr_hi = wr.astype(BF16)
        wr_r1 = wr - wr_hi.astype(F32)
        wr_mid = wr_r1.astype(BF16)
        wr_lo = (wr_r1 - wr_mid.astype(F32)).astype(BF16)
        br = jnp.concatenate([b_router_g[l], b_router_e[l],
                              jnp.zeros((LANES - N_GROUPS - N_EXPERTS,), F32)])[None, :]
        x1, h2, ids, gates = _out_proj(x2, o_da.reshape(n, -1), o_sb.reshape(n, -1), w_o[l].astype(BF16),
                                       g_ffn[l][None, :], jnp.stack([wr_hi, wr_mid, wr_lo]), br)

        eid = ids[:, 0:2].reshape(-1)
        onehot = (eid[:, None] == jnp.arange(N_EXPERTS, dtype=jnp.int32)[None, :]).astype(jnp.int32)
        csum = jnp.cumsum(onehot, axis=0)
        counts = csum[-1]
        padded = (counts + MOE_BLOCK - 1) // MOE_BLOCK * MOE_BLOCK
        pad_ends = jnp.cumsum(padded)
        pad_starts = pad_ends - padded
        dest = jnp.sum(onehot * (csum - 1 + pad_starts[None, :]), axis=1).astype(jnp.int32)
        cap = 2 * n + N_EXPERTS * MOE_BLOCK
        nb = cap // MOE_BLOCK
        blk_e = jnp.minimum(jnp.searchsorted(pad_ends, jnp.arange(nb, dtype=jnp.int32) * MOE_BLOCK, side='right'),
                            N_EXPERTS - 1).astype(jnp.int32)
        n_used = (pad_ends[-1] // MOE_BLOCK).astype(jnp.int32).reshape(1)
        dest2 = dest.reshape(n // TOK_TILE, 1, 2 * TOK_TILE)

        xs = _dispatch(dest2, h2, jnp.zeros((cap, d // 2), jnp.uint32))
        ys = _experts(blk_e, n_used, xs, w1[l].astype(BF16), w3[l].astype(BF16), w2[l].astype(BF16))
        x = _combine(dest2, x1, gates, ys).reshape(bsz, s_len, d)
    return x
```

```python
import functools
import math

import jax
import jax.numpy as jnp
import numpy as np
from jax import lax
from jax.experimental import pallas as pl
from jax.experimental.pallas import tpu as pltpu

F32 = jnp.float32
BF16 = jnp.bfloat16

EPS = 1e-6
LANES = 128
DA_HEADS = 4
DA_HEAD_DIM = 64
SB_HEADS = 8
SB_HEAD_DIM = 64
CHUNK = 64
REL_BUCKETS = 32
REL_MAX_DIST = 128
N_GROUPS = 4
EXPERTS_PER_GROUP = 8
N_EXPERTS = N_GROUPS * EXPERTS_PER_GROUP
NEG_BIG = -1e30
LOG2E = math.log2(math.e)
SB_DEAD = -87.5

VMEM_LIMIT = 48 * 1024 * 1024

PROJ_ROWS = 512
ROUTE_ROWS = 48
DA_BLOCK = 256
DA_UNROLL = 4
DA_SAFE_RANGE = 100.0
SB_SUB = 128
SB_ROWS = 512
SB_WIN = 256
MOE_BLOCK = 256
TOK_TILE = 256
ROW_DMA_UNROLL = 8


def _cparams(sem):
    return pltpu.CompilerParams(dimension_semantics=sem, vmem_limit_bytes=VMEM_LIMIT)


def _in_proj_kernel(x_ref, g_ref, w_ref, gain_ref, bd_ref, o_ref):
    x = x_ref[...]
    ms = jnp.mean(x * x, axis=-1, keepdims=True)
    h = (x * lax.rsqrt(ms + EPS) * g_ref[...]).astype(BF16)
    n_chunks = o_ref.shape[1] // 512
    for c in range(n_chunks):
        cols = slice(c * 512, (c + 1) * 512)
        acc = jnp.dot(h, w_ref[:, cols], preferred_element_type=F32)
        if c < 2:
            sq = (acc * acc).astype(BF16)
            parts = []
            for s in range(2):
                ss = jnp.dot(sq[:, s * 256:(s + 1) * 256], bd_ref[...], preferred_element_type=F32)
                parts.append(ss)
            ss = jnp.concatenate(parts, axis=1)
            acc = acc * lax.rsqrt(ss * (1.0 / DA_HEAD_DIM) + EPS)
        o_ref[:, cols] = (acc * gain_ref[:, cols]).astype(BF16)


def _in_proj(x2, g_attn, w_in_bf, gain_row, bd):
    n, d = x2.shape
    width = w_in_bf.shape[1]
    return pl.pallas_call(
        _in_proj_kernel,
        grid=(n // PROJ_ROWS,),
        in_specs=[
            pl.BlockSpec((PROJ_ROWS, d), lambda i: (i, 0)),
            pl.BlockSpec((1, d), lambda i: (0, 0)),
            pl.BlockSpec((d, width), lambda i: (0, 0)),
            pl.BlockSpec((1, width), lambda i: (0, 0)),
            pl.BlockSpec((256, 256), lambda i: (0, 0)),
        ],
        out_specs=pl.BlockSpec((PROJ_ROWS, width), lambda i: (i, 0)),
        out_shape=jax.ShapeDtypeStruct((n, width), BF16),
        compiler_params=_cparams(("parallel",)),
        name="in_proj",
    )(x2, g_attn, w_in_bf, gain_row, bd)


def _da_kernel(lam_ref, brange_ref, q_ref, k_ref, v_ref, bias_ref, g_ref, o_ref, m_scr, acc_scr, kmax_scr,
               pa_scr, pb_scr, *, out_scale):
    t = DA_BLOCK
    h = pl.program_id(1)
    i = pl.program_id(2)
    s_len = k_ref.shape[1]
    ones_col = jnp.ones((LANES, LANES), BF16)

    @pl.when(i == 0)
    def _():
        lane_k = lax.broadcasted_iota(jnp.int32, (1, LANES), 1)
        rows = 512

        def kbody(c, best):
            kc = k_ref[0, pl.ds(pl.multiple_of(c * rows, rows), rows), :].astype(F32)
            sq = kc * kc
            n1 = jnp.dot(jnp.where(lane_k < DA_HEAD_DIM, sq, 0.0).astype(BF16), ones_col, preferred_element_type=F32)
            n2 = jnp.dot(jnp.where(lane_k >= DA_HEAD_DIM, sq, 0.0).astype(BF16), ones_col, preferred_element_type=F32)
            return (jnp.maximum(best[0], jnp.max(n1, axis=0, keepdims=True)),
                    jnp.maximum(best[1], jnp.max(n2, axis=0, keepdims=True)))

        zero_row = jnp.zeros((1, LANES), F32)
        k1, k2 = lax.fori_loop(0, s_len // rows, kbody, (zero_row, zero_row))
        kmax_scr[0:8, :] = jnp.broadcast_to(k1, (8, LANES))
        kmax_scr[8:16, :] = jnp.broadcast_to(k2, (8, LANES))

    q = q_ref[0]
    lane = lax.broadcasted_iota(jnp.int32, (1, LANES), 1)
    zero = jnp.zeros_like(q)
    qq = jnp.concatenate([jnp.where(lane < DA_HEAD_DIM, q, zero),
                          jnp.where(lane >= DA_HEAD_DIM, q, zero)], axis=0)
    ones = jnp.ones((t, LANES), BF16)
    row = lax.broadcasted_iota(jnp.int32, (t, t), 0)
    col = lax.broadcasted_iota(jnp.int32, (t, t), 1)
    chunk_mask = (col // CHUNK) <= (row // CHUNK)
    n_far = jnp.maximum(i - 1, 0)

    def values(j):
        return jnp.concatenate([v_ref[0, pl.ds(pl.multiple_of(j * t, t), t), :], ones], axis=1)

    def scores(j, bias, mask):
        kb = k_ref[0, pl.ds(pl.multiple_of(j * t, t), t), :]
        s = lax.dot_general(qq, kb, (((1,), (1,)), ((), ())), preferred_element_type=F32)
        if bias is not None:
            s = s + jnp.concatenate([bias, bias], axis=0)
        if mask is not None:
            s = jnp.where(jnp.concatenate([mask, mask], axis=0) if mask.ndim else mask, s, NEG_BIG)
        return s, values(j)

    qf = qq.astype(F32)
    qn2 = jnp.dot((qf * qf).astype(BF16), ones_col, preferred_element_type=F32)
    kmax2 = jnp.concatenate([jnp.broadcast_to(kmax_scr[0:1, :], (t, LANES)),
                             jnp.broadcast_to(kmax_scr[8:9, :], (t, LANES))], axis=0)
    reach = jnp.sqrt(qn2 * kmax2) * 1.02
    b_hi = brange_ref[0, h]
    b_lo = brange_ref[1, h]
    fits = jnp.max(2.0 * reach + (b_hi - b_lo)) <= DA_SAFE_RANGE

    @pl.when(fits)
    def _():
        stab = reach + b_hi
        stab = jnp.concatenate([stab] * (t // LANES), axis=1)

        def probs_into(g, p_scr):
            for u in range(DA_UNROLL):
                s, _ = scores(g * DA_UNROLL + u, None, None)
                p_scr[u] = jnp.exp2(s - stab).astype(BF16)

        def weighted_from(g, p_scr):
            tot = jnp.dot(p_scr[0], values(g * DA_UNROLL), preferred_element_type=F32)
            for u in range(1, DA_UNROLL):
                tot = tot + jnp.dot(p_scr[u], values(g * DA_UNROLL + u), preferred_element_type=F32)
            acc_scr[...] += tot

        acc_scr[...] = jnp.zeros(acc_scr.shape, F32)
        n_grp = n_far // DA_UNROLL

        @pl.when(n_grp > 0)
        def _():
            probs_into(0, pa_scr)

            def pair_body(hp, carry):
                g = 2 * hp
                weighted_from(g, pa_scr)
                probs_into(g + 1, pb_scr)
                weighted_from(g + 1, pb_scr)
                probs_into(jnp.minimum(g + 2, n_grp - 1), pa_scr)
                return carry

            lax.fori_loop(0, n_grp // 2, pair_body, 0)

            @pl.when(n_grp % 2 == 1)
            def _():
                weighted_from(n_grp - 1, pa_scr)

        left = n_far - n_grp * DA_UNROLL
        tail = [(jnp.minimum(n_grp * DA_UNROLL + u, jnp.maximum(n_far - 1, 0)), None, u < left)
                for u in range(DA_UNROLL - 1)]
        tail += [(n_far, bias_ref[0, 1], i > 0), (i, bias_ref[0, 0], chunk_mask)]
        tail_scores = [scores(j, bias, mask) for j, bias, mask in tail]
        tot = None
        for s, vb in tail_scores:
            pv = jnp.dot(jnp.exp2(s - stab).astype(BF16), vb, preferred_element_type=F32)
            tot = pv if tot is None else tot + pv
        acc_scr[...] += tot

    @pl.when(jnp.logical_not(fits))
    def _():
        m_scr[...] = jnp.full(m_scr.shape, NEG_BIG, F32)
        acc_scr[...] = jnp.zeros(acc_scr.shape, F32)

        def step(j, bias=None, mask=None):
            s, vb = scores(j, bias, mask)
            m_prev = m_scr[...]
            m_next = jnp.maximum(m_prev, jnp.max(s, axis=1, keepdims=True))
            alpha = jnp.exp2(m_prev - m_next)
            p = jnp.exp2(s - jnp.concatenate([m_next] * (t // LANES), axis=1))
            pv = jnp.dot(p.astype(BF16), vb, preferred_element_type=F32)
            acc_scr[...] = jnp.concatenate([alpha, alpha], axis=1) * acc_scr[...] + pv
            m_scr[...] = m_next

        def far_body(j, carry):
            step(j)
            return carry

        lax.fori_loop(0, n_far, far_body, 0)

        @pl.when(i > 0)
        def _():
            step(i - 1, bias=bias_ref[0, 1])

        step(i, bias=bias_ref[0, 0], mask=chunk_mask)

    acc = acc_scr[...]
    o_all = acc[:, 0:LANES] / acc[:, LANES:2 * LANES]
    o = o_all[0:t] - lam_ref[0] * o_all[t:2 * t]
    ms = jnp.mean(o * o, axis=-1, keepdims=True)
    o_ref[0] = (o * lax.rsqrt(ms + EPS) * (g_ref[...] * out_scale)).astype(BF16)


def _diff_attention(proj3, lam, bias_range, bias_tiles, subln_row, out_scale):
    bsz, s_len, _ = proj3.shape
    t = DA_BLOCK
    nq = s_len // t
    kern = functools.partial(_da_kernel, out_scale=out_scale)
    return pl.pallas_call(
        kern,
        grid=(bsz, DA_HEADS, nq),
        in_specs=[
            pl.BlockSpec(memory_space=pltpu.SMEM),
            pl.BlockSpec(memory_space=pltpu.SMEM),
            pl.BlockSpec((1, t, LANES), lambda b, h, i: (b, i, h)),
            pl.BlockSpec((1, s_len, LANES), lambda b, h, i: (b, 0, DA_HEADS + h)),
            pl.BlockSpec((1, s_len, LANES), lambda b, h, i: (b, 0, 2 * DA_HEADS + h)),
            pl.BlockSpec((1, 2, t, t), lambda b, h, i: (h, 0, 0, 0)),
            pl.BlockSpec((1, LANES), lambda b, h, i: (0, 0)),
        ],
        out_specs=pl.BlockSpec((1, t, LANES), lambda b, h, i: (b, i, h)),
        out_shape=jax.ShapeDtypeStruct((bsz, s_len, DA_HEADS * LANES), BF16),
        scratch_shapes=[
            pltpu.VMEM((2 * t, LANES), F32),
            pltpu.VMEM((2 * t, 2 * LANES), F32),
            pltpu.VMEM((16, LANES), F32),
            pltpu.VMEM((DA_UNROLL, 2 * t, t), BF16),
            pltpu.VMEM((DA_UNROLL, 2 * t, t), BF16),
        ],
        compiler_params=_cparams(("parallel", "parallel", "arbitrary")),
        name="diff_attention",
    )(lam, bias_range, proj3, proj3, proj3, bias_tiles, subln_row)


def _sb_tiles(tiles, tri):
    zs = [lax.dot_general(q, k, (((1,), (1,)), ((), ())), preferred_element_type=F32) for q, k, _, _, _ in tiles]
    mids = []
    for z, (_, _, _, causal, _) in zip(zs, tiles):
        log_1m = -(jnp.maximum(z, 0.0) + jnp.log(1.0 + jnp.exp(-jnp.abs(z))))
        lm = log_1m if causal is None else jnp.where(causal, log_1m, 0.0)
        hi = lm.astype(BF16)
        mid = (lm - hi.astype(F32)).astype(BF16)
        inner = jnp.dot(hi, tri, preferred_element_type=F32) + jnp.dot(mid, tri, preferred_element_type=F32)
        mids.append((z + log_1m + inner, jnp.sum(lm, axis=1, keepdims=True)))
    outs = []
    for (logit, total), (_, _, v, causal, carry) in zip(mids, tiles):
        if carry is not None:
            logit = logit + carry
        a = jnp.exp(logit)
        if causal is not None:
            a = jnp.where(causal, a, 0.0)
        outs.append((jnp.dot(a.astype(BF16), v, preferred_element_type=F32), total))
    return outs


def _sb_kernel(q_ref, k_ref, v_ref, g_ref, tri_ref, o_ref, acc_scr, carry_scr):
    sub, win = SB_SUB, SB_WIN
    i = pl.program_id(2)
    lane = lax.broadcasted_iota(jnp.int32, (1, LANES), 1)
    tri = tri_ref[...]
    row = lax.broadcasted_iota(jnp.int32, (sub, win), 0)
    col = lax.broadcasted_iota(jnp.int32, (sub, win), 1)
    n_sub = SB_ROWS // sub

    tiles, starts = [], []
    for u in range(n_sub):
        r0 = i * SB_ROWS + u * sub
        start = pl.multiple_of(jnp.maximum(r0 - sub, 0), sub)
        qu = q_ref[0, u * sub:(u + 1) * sub, :]
        zero = jnp.zeros_like(qu)
        qs = jnp.concatenate([jnp.where(lane < SB_HEAD_DIM, qu, zero),
                              jnp.where(lane >= SB_HEAD_DIM, qu, zero)], axis=0)
        causal = (start + col) < (r0 + row)
        tiles.append((qs, k_ref[0, pl.ds(start, win), :], v_ref[0, pl.ds(start, win), :],
                      jnp.concatenate([causal, causal], axis=0), None))
        starts.append(start)
    chains = []
    for u, (pv, total) in enumerate(_sb_tiles(tiles, tri)):
        acc_scr[u] = pv
        carry_scr[u] = jnp.broadcast_to(total, (2 * sub, LANES))
        chains.append((u, tiles[u][0], starts[u] // sub - 1, jnp.max(total)))

    tri_sub = tri[0:sub, 0:sub]
    for u, qs, j0, alive0 in chains:
        def cond(state):
            j, alive = state
            return (j >= 0) & (alive > SB_DEAD)

        def body(state, u=u, qs=qs):
            j, _ = state
            s0 = pl.multiple_of(j * sub, sub)
            kb = k_ref[0, pl.ds(s0, sub), :]
            vb = v_ref[0, pl.ds(s0, sub), :]
            carry = carry_scr[u]
            (pv, total), = _sb_tiles([(qs, kb, vb, None, carry)], tri_sub)
            acc_scr[u] += pv
            carry = carry + total
            carry_scr[u] = carry
            return j - 1, jnp.max(carry)

        lax.while_loop(cond, body, (j0, alive0))

    for u in range(n_sub):
        acc = acc_scr[u]
        o = jnp.where(lane < SB_HEAD_DIM, acc[0:sub], acc[sub:2 * sub])
        sq = o * o
        ss0 = jnp.sum(jnp.where(lane < SB_HEAD_DIM, sq, 0.0), axis=1, keepdims=True)
        ss1 = jnp.sum(jnp.where(lane >= SB_HEAD_DIM, sq, 0.0), axis=1, keepdims=True)
        ms = jnp.where(lane < SB_HEAD_DIM, ss0, ss1) * (1.0 / SB_HEAD_DIM)
        o_ref[0, u * sub:(u + 1) * sub, :] = (o * lax.rsqrt(ms + EPS) * g_ref[...]).astype(BF16)


def _stick_breaking(proj3, sb_row, tri):
    bsz, s_len, _ = proj3.shape
    t = SB_ROWS
    nq = s_len // t
    pairs = SB_HEADS // 2
    q_blk = 3 * DA_HEADS
    n_chains = SB_ROWS // SB_SUB
    return pl.pallas_call(
        _sb_kernel,
        grid=(bsz, pairs, nq),
        in_specs=[
            pl.BlockSpec((1, t, LANES), lambda b, h, i: (b, i, q_blk + h)),
            pl.BlockSpec((1, s_len, LANES), lambda b, h, i: (b, 0, q_blk + pairs + h)),
            pl.BlockSpec((1, s_len, LANES), lambda b, h, i: (b, 0, q_blk + 2 * pairs + h)),
            pl.BlockSpec((1, LANES), lambda b, h, i: (0, 0)),
            pl.BlockSpec((SB_WIN, SB_WIN), lambda b, h, i: (0, 0)),
        ],
        out_specs=pl.BlockSpec((1, t, LANES), lambda b, h, i: (b, i, h)),
        out_shape=jax.ShapeDtypeStruct((bsz, s_len, pairs * LANES), BF16),
        scratch_shapes=[
            pltpu.VMEM((n_chains, 2 * SB_SUB, LANES), F32),
            pltpu.VMEM((n_chains, 2 * SB_SUB, LANES), F32),
        ],
        compiler_params=_cparams(("parallel", "parallel", "arbitrary")),
        name="stick_breaking",
    )(proj3, proj3, proj3, sb_row, tri)


def _out_proj_kernel(x_ref, oda_ref, osb_ref, wo_ref, g_ref, wr_ref, br_ref, upper_ref,
                     x1_ref, h2_ref, route_i_ref, route_f_ref, counts_ref, base_scr):
    half = oda_ref.shape[1]
    x1 = (x_ref[...]
          + jnp.dot(oda_ref[...], wo_ref[0:half, :], preferred_element_type=F32)
          + jnp.dot(osb_ref[...], wo_ref[half:2 * half, :], preferred_element_type=F32))
    x1_ref[...] = x1
    ms = jnp.mean(x1 * x1, axis=-1, keepdims=True)
    h2 = x1 * lax.rsqrt(ms + EPS) * g_ref[...]
    d_half = h2.shape[1] // 2
    hi_bits = lax.bitcast_convert_type(h2[:, :d_half].astype(BF16).astype(F32), jnp.uint32)
    lo_bits = lax.bitcast_convert_type(h2[:, d_half:].astype(BF16).astype(F32), jnp.uint32)
    h2_ref[...] = hi_bits | (lo_bits >> 16)

    a_hi = h2.astype(BF16)
    a_mid = (h2 - a_hi.astype(F32)).astype(BF16)
    nt = lambda w, a: lax.dot_general(w, a, (((1,), (1,)), ((), ())), preferred_element_type=F32)
    lt = nt(wr_ref[0], a_hi) + (nt(wr_ref[0], a_mid) + nt(wr_ref[1], a_hi)) + br_ref[...]
    n_rows, tm = lt.shape
    ridx = lax.broadcasted_iota(jnp.int32, (n_rows, tm), 0)
    big = jnp.int32(n_rows)
    is_g = ridx < N_GROUPS
    lg = jnp.where(is_g, lt, NEG_BIG)
    mg = jnp.max(lg, axis=0, keepdims=True)
    gsel = jnp.min(jnp.where(lg == mg, ridx, big), axis=0, keepdims=True)
    pg_sel = 1.0 / jnp.sum(jnp.where(is_g, jnp.exp(lg - mg), 0.0), axis=0, keepdims=True)
    lo_row = N_GROUPS + gsel * EXPERTS_PER_GROUP
    le = jnp.where(ridx >= lo_row, jnp.where(ridx < lo_row + EXPERTS_PER_GROUP, lt, NEG_BIG), NEG_BIG)
    l1 = jnp.max(le, axis=0, keepdims=True)
    i1 = jnp.min(jnp.where(le == l1, ridx, big), axis=0, keepdims=True)
    le2 = jnp.where(ridx == i1, NEG_BIG, le)
    l2 = jnp.max(le2, axis=0, keepdims=True)
    i2 = jnp.min(jnp.where(le2 == l2, ridx, big), axis=0, keepdims=True)
    e2 = jnp.exp(l2 - l1)
    g1 = pg_sel / (1.0 + e2)
    g2 = pg_sel * e2 / (1.0 + e2)

    @pl.when(pl.program_id(0) == 0)
    def _():
        base_scr[...] = jnp.zeros(base_scr.shape, F32)

    sel1 = ridx == i1
    sel2 = ridx == i2
    chosen = jnp.where(sel1, 1.0, jnp.where(sel2, 1.0, 0.0))
    before = jnp.dot(chosen.astype(BF16), upper_ref[...], preferred_element_type=F32)
    base = base_scr[...]
    before = before + jnp.concatenate([base] * (tm // LANES), axis=1)
    rank1 = jnp.sum(jnp.where(sel1, before, 0.0), axis=0, keepdims=True)
    rank2 = jnp.sum(jnp.where(sel2, before, 0.0), axis=0, keepdims=True)
    base = base + jnp.sum(chosen, axis=1, keepdims=True)
    base_scr[...] = base
    counts_ref[...] = base.astype(jnp.int32)

    r8 = lax.broadcasted_iota(jnp.int32, (8, tm), 0)
    route_i_ref[...] = jnp.where(r8 == 0, i1 - N_GROUPS, jnp.where(r8 == 1, i2 - N_GROUPS, jnp.where(
        r8 == 2, rank1.astype(jnp.int32), jnp.where(r8 == 3, rank2.astype(jnp.int32), 0))))
    route_f_ref[...] = jnp.where(r8 == 0, g1, jnp.where(r8 == 1, g2, 0.0))


def _out_proj(x2, o_da, o_sb, wo_bf, g_ffn, wr2, br, upper):
    n, d = x2.shape
    half = o_da.shape[1]
    tm = PROJ_ROWS
    r = ROUTE_ROWS
    return pl.pallas_call(
        _out_proj_kernel,
        grid=(n // tm,),
        in_specs=[
            pl.BlockSpec((tm, d), lambda i: (i, 0)),
            pl.BlockSpec((tm, half), lambda i: (i, 0)),
            pl.BlockSpec((tm, half), lambda i: (i, 0)),
            pl.BlockSpec((2 * half, d), lambda i: (0, 0)),
            pl.BlockSpec((1, d), lambda i: (0, 0)),
            pl.BlockSpec((2, r, d), lambda i: (0, 0, 0)),
            pl.BlockSpec((r, tm), lambda i: (0, 0)),
            pl.BlockSpec((tm, tm), lambda i: (0, 0)),
        ],
        out_specs=[
            pl.BlockSpec((tm, d), lambda i: (i, 0)),
            pl.BlockSpec((tm, d // 2), lambda i: (i, 0)),
            pl.BlockSpec((8, tm), lambda i: (0, i)),
            pl.BlockSpec((8, tm), lambda i: (0, i)),
            pl.BlockSpec((r, LANES), lambda i: (0, 0)),
        ],
        out_shape=[
            jax.ShapeDtypeStruct((n, d), F32),
            jax.ShapeDtypeStruct((n, d // 2), jnp.uint32),
            jax.ShapeDtypeStruct((8, n), jnp.int32),
            jax.ShapeDtypeStruct((8, n), F32),
            jax.ShapeDtypeStruct((r, LANES), jnp.int32),
        ],
        scratch_shapes=[pltpu.VMEM((r, LANES), F32)],
        compiler_params=_cparams(("arbitrary",)),
        name="out_proj_router",
    )(x2, o_da, o_sb, wo_bf, g_ffn, wr2, br, upper)


def _dispatch_kernel(dest_ref, h_ref, xs_in_ref, xs_ref, sem):
    del xs_in_ref
    tt = h_ref.shape[0]

    def copy(r, k):
        return pltpu.make_async_copy(h_ref.at[pl.ds(r, 1)], xs_ref.at[pl.ds(dest_ref[0, 0, 2 * r + k], 1)], sem)

    def issue(r, c):
        copy(r, 0).start()
        copy(r, 1).start()
        return c

    lax.fori_loop(0, tt, issue, 0, unroll=ROW_DMA_UNROLL)

    def drain(r, c):
        copy(r, 0).wait()
        copy(r, 1).wait()
        return c

    lax.fori_loop(0, tt, drain, 0, unroll=True)


def _dispatch(dest2, h2, xs_init):
    n, d = h2.shape
    tt = TOK_TILE
    return pl.pallas_call(
        _dispatch_kernel,
        grid=(n // tt,),
        in_specs=[
            pl.BlockSpec((1, 1, 2 * tt), lambda i: (i, 0, 0), memory_space=pltpu.SMEM),
            pl.BlockSpec((tt, d), lambda i: (i, 0)),
            pl.BlockSpec(memory_space=pl.ANY),
        ],
        out_specs=pl.BlockSpec(memory_space=pl.ANY),
        out_shape=jax.ShapeDtypeStruct(xs_init.shape, xs_init.dtype),
        scratch_shapes=[pltpu.SemaphoreType.DMA(())],
        input_output_aliases={2: 0},
        compiler_params=_cparams(("arbitrary",)),
        name="moe_dispatch",
    )(dest2, h2, xs_init)


def _expert_kernel(blk_e_ref, n_used_ref, xs_ref, w1_ref, w3_ref, w2_ref, ys_ref):
    i = pl.program_id(0)

    @pl.when(i < n_used_ref[0])
    def _():
        words = xs_ref[...]
        xb = jnp.concatenate([
            lax.bitcast_convert_type(words & jnp.uint32(0xFFFF0000), F32).astype(BF16),
            lax.bitcast_convert_type(words << 16, F32).astype(BF16)], axis=1)
        a = jnp.dot(xb, w1_ref[0], preferred_element_type=F32)
        b = jnp.dot(xb, w3_ref[0], preferred_element_type=F32)
        hmid = (a * jax.nn.sigmoid(a) * b).astype(BF16)
        ys_ref[...] = jnp.dot(hmid, w2_ref[0], preferred_element_type=F32)

    @pl.when(i >= n_used_ref[0])
    def _():
        ys_ref[...] = jnp.zeros(ys_ref.shape, F32)


def _experts(blk_e, n_used, xs, w1_bf, w3_bf, w2_bf):
    cap = xs.shape[0]
    d, hid = w1_bf.shape[1], w1_bf.shape[2]
    nb = cap // MOE_BLOCK

    def row_map(i, blk_e_ref, n_used_ref):
        return (jnp.minimum(i, n_used_ref[0] - 1), 0)

    def w_map(i, blk_e_ref, n_used_ref):
        return (blk_e_ref[jnp.minimum(i, n_used_ref[0] - 1)], 0, 0)

    grid_spec = pltpu.PrefetchScalarGridSpec(
        num_scalar_prefetch=2,
        grid=(nb,),
        in_specs=[
            pl.BlockSpec((MOE_BLOCK, d // 2), row_map),
            pl.BlockSpec((1, d, hid), w_map),
            pl.BlockSpec((1, d, hid), w_map),
            pl.BlockSpec((1, hid, d), w_map),
        ],
        out_specs=pl.BlockSpec((MOE_BLOCK, d), lambda i, blk_e_ref, n_used_ref: (i, 0)),
    )
    return pl.pallas_call(
        _expert_kernel,
        grid_spec=grid_spec,
        out_shape=jax.ShapeDtypeStruct((cap, d), F32),
        compiler_params=_cparams(("arbitrary",)),
        name="moe_experts",
    )(blk_e, n_used, xs, w1_bf, w3_bf, w2_bf)


def _combine_kernel(dest_ref, x1_ref, gates_ref, ys_ref, o_ref, y0_scr, y1_scr, sem):
    tt = x1_ref.shape[0]

    def copy(r, k):
        dst = y0_scr if k == 0 else y1_scr
        return pltpu.make_async_copy(ys_ref.at[pl.ds(dest_ref[0, 0, 2 * r + k], 1)], dst.at[pl.ds(r, 1)], sem)

    def issue(r, c):
        copy(r, 0).start()
        copy(r, 1).start()
        return c

    lax.fori_loop(0, tt, issue, 0, unroll=ROW_DMA_UNROLL)

    def drain(r, c):
        copy(r, 0).wait()
        copy(r, 1).wait()
        return c

    lax.fori_loop(0, tt, drain, 0, unroll=True)
    g = gates_ref[...]
    o_ref[...] = x1_ref[...] + y0_scr[...] * g[:, 0:1] + y1_scr[...] * g[:, 1:2]


def _combine(dest2, x1, gates, ys):
    n, d = x1.shape
    tt = TOK_TILE
    return pl.pallas_call(
        _combine_kernel,
        grid=(n // tt,),
        in_specs=[
            pl.BlockSpec((1, 1, 2 * tt), lambda i: (i, 0, 0), memory_space=pltpu.SMEM),
            pl.BlockSpec((tt, d), lambda i: (i, 0)),
            pl.BlockSpec((tt, LANES), lambda i: (i, 0)),
            pl.BlockSpec(memory_space=pl.ANY),
        ],
        out_specs=pl.BlockSpec((tt, d), lambda i: (i, 0)),
        out_shape=jax.ShapeDtypeStruct((n, d), F32),
        scratch_shapes=[
            pltpu.VMEM((tt, d), F32),
            pltpu.VMEM((tt, d), F32),
            pltpu.SemaphoreType.DMA(()),
        ],
        compiler_params=_cparams(("arbitrary",)),
        name="moe_combine",
    )(dest2, x1, gates, ys)


def _rel_bucket_np(rel):
    nb = REL_BUCKETS // 2
    max_exact = nb // 2
    base = np.where(rel > 0, nb, 0)
    n = np.abs(rel)
    nf = np.maximum(n, 1).astype(np.float64)
    large = max_exact + (np.log(nf / max_exact) / math.log(REL_MAX_DIST / max_exact) * (nb - max_exact)).astype(np.int64)
    large = np.minimum(large, nb - 1)
    return (base + np.where(n < max_exact, n, large)).astype(np.int32)


def _bias_bucket_rows(t):
    far = _rel_bucket_np(-np.arange(t + 1, 4 * t))
    far_bucket = int(far[0])
    assert (far == far_bucket).all()
    m = np.arange(2 * t)
    wrap = np.where(m < t, m, m - 2 * t)
    return np.stack([_rel_bucket_np(wrap), _rel_bucket_np(wrap - t)]), far_bucket


def _toeplitz(rows, t):
    lead = rows.shape[:-1]
    flat = jnp.tile(rows, (1,) * len(lead) + (t,))[..., : t * (2 * t - 1)]
    return flat.reshape(lead + (t, 2 * t - 1))[..., :t]


def kernel(x, g_attn, w_in, qn_g, kn_g, lam_q1, lam_k1, lam_q2, lam_k2, subln_g, sb_out_g, rel_bias, w_o,
           g_ffn, w_router_g, b_router_g, w_router_e, b_router_e, w1, w3, w2):
    bsz, s_len, d = x.shape
    n = bsz * s_len
    depth = g_attn.shape[0]
    da_width = DA_HEADS * 2 * DA_HEAD_DIM
    sb_width = SB_HEADS * SB_HEAD_DIM

    bd = jnp.asarray(np.kron(np.eye(256 // DA_HEAD_DIM), np.ones((DA_HEAD_DIM, DA_HEAD_DIM))), BF16)
    tri = jnp.asarray(np.tril(np.ones((SB_WIN, SB_WIN)), -1), BF16)
    bucket_rows, far_bucket = _bias_bucket_rows(DA_BLOCK)
    upper = jnp.asarray(np.triu(np.ones((PROJ_ROWS, PROJ_ROWS)), 1), BF16)

    for l in range(depth):
        lambda_init = 0.8 - 0.6 * math.exp(-0.3 * l)
        x2 = x.reshape(n, d)

        q_scale = DA_HEAD_DIM ** -0.5 * LOG2E
        gain_row = jnp.concatenate([
            jnp.tile(qn_g[l] * q_scale, 2 * DA_HEADS),
            jnp.tile(kn_g[l], 2 * DA_HEADS),
            jnp.ones((da_width,), F32),
            jnp.full((sb_width,), SB_HEAD_DIM ** -0.5, F32),
            jnp.ones((2 * sb_width,), F32),
        ])[None, :]
        proj = _in_proj(x2, g_attn[l][None, :], w_in[l].astype(BF16), gain_row, bd)
        proj3 = proj.reshape(bsz, s_len, -1)

        lam = (jnp.exp(jnp.sum(lam_q1[l] * lam_k1[l])) - jnp.exp(jnp.sum(lam_q2[l] * lam_k2[l]))
               + lambda_init).astype(F32).reshape(1)
        rb = (rel_bias - rel_bias[far_bucket][None, :]) * LOG2E
        bias_rows = jnp.transpose(rb[bucket_rows], (2, 0, 1)).astype(F32)
        bias_tiles = _toeplitz(bias_rows, DA_BLOCK)
        bias_range = jnp.stack([jnp.maximum(jnp.max(rb, axis=0), 0.0), jnp.minimum(jnp.min(rb, axis=0), 0.0)])
        o_da = _diff_attention(proj3, lam, bias_range.astype(F32), bias_tiles, subln_g[l][None, :],
                               1.0 - lambda_init)
        o_sb = _stick_breaking(proj3, jnp.tile(sb_out_g[l], 2)[None, :], tri)

        wr = jnp.concatenate([w_router_g[l], w_router_e[l],
                              jnp.zeros((d, ROUTE_ROWS - N_GROUPS - N_EXPERTS), F32)], axis=1).T
        wr_hi = wr.astype(BF16)
        wr_mid = (wr - wr_hi.astype(F32)).astype(BF16)
        br = jnp.concatenate([b_router_g[l], b_router_e[l],
                              jnp.zeros((ROUTE_ROWS - N_GROUPS - N_EXPERTS,), F32)])
        br = jnp.broadcast_to(br[:, None], (ROUTE_ROWS, PROJ_ROWS))
        x1, h2, route_i, route_f, counts_all = _out_proj(
            x2, o_da.reshape(n, -1), o_sb.reshape(n, -1), w_o[l].astype(BF16), g_ffn[l][None, :],
            jnp.stack([wr_hi, wr_mid]), br, upper)

        counts = counts_all[N_GROUPS:N_GROUPS + N_EXPERTS, 0]
        padded = (counts + MOE_BLOCK - 1) // MOE_BLOCK * MOE_BLOCK
        pad_ends = jnp.cumsum(padded)
        pad_starts = pad_ends - padded
        experts = jnp.arange(N_EXPERTS, dtype=jnp.int32)
        eid, rank = route_i[0:2], route_i[2:4]
        start_of = jnp.sum(jnp.where(eid[:, :, None] == experts, pad_starts, 0), axis=-1)
        dest2 = (start_of + rank).T.reshape(n // TOK_TILE, 1, 2 * TOK_TILE)
        gates = jnp.pad(route_f[0:2].T, ((0, 0), (0, LANES - 2)))
        cap = 2 * n + N_EXPERTS * MOE_BLOCK
        nb = cap // MOE_BLOCK
        blk_start = jnp.arange(nb, dtype=jnp.int32) * MOE_BLOCK
        blk_e = jnp.minimum(jnp.sum(pad_ends[None, :] <= blk_start[:, None], axis=1), N_EXPERTS - 1).astype(jnp.int32)
        n_used = (pad_ends[-1] // MOE_BLOCK).astype(jnp.int32).reshape(1)

        xs = _dispatch(dest2, h2, jnp.zeros((cap, d // 2), jnp.uint32))
        ys = _experts(blk_e, n_used, xs, w1[l].astype(BF16), w3[l].astype(BF16), w2[l].astype(BF16))
        x = _combine(dest2, x1, gates, ys).reshape(bsz, s_len, d)
    return x
```

```python
import functools
import math

import jax
import jax.numpy as jnp
import numpy as np
from jax import lax
from jax.experimental import pallas as pl
from jax.experimental.pallas import tpu as pltpu

F32 = jnp.float32
BF16 = jnp.bfloat16

EPS = 1e-6
LANES = 128
DA_HEADS = 4
DA_HEAD_DIM = 64
SB_HEADS = 8
SB_HEAD_DIM = 64
CHUNK = 64
REL_BUCKETS = 32
REL_MAX_DIST = 128
N_GROUPS = 4
EXPERTS_PER_GROUP = 8
N_EXPERTS = N_GROUPS * EXPERTS_PER_GROUP
NEG_BIG = -1e30
LOG2E = math.log2(math.e)
SB_DEAD = -87.5

VMEM_LIMIT = 48 * 1024 * 1024

PROJ_ROWS = 512
ROUTE_ROWS = 48
DA_BLOCK = 256
DA_UNROLL = 4
DA_SAFE_RANGE = 100.0
SB_SUB = 128
SB_ROWS = 1024
SB_WIN = 256
MOE_BLOCK = 512
MOE_CHUNKS = 2
TOK_TILE = 512
ROW_DMA_UNROLL = 8


def _cparams(sem):
    return pltpu.CompilerParams(dimension_semantics=sem, vmem_limit_bytes=VMEM_LIMIT)


def _in_proj_kernel(x_ref, g_ref, w_ref, gain_ref, bd_ref, o_ref):
    x = x_ref[...]
    ms = jnp.mean(x * x, axis=-1, keepdims=True)
    h = (x * lax.rsqrt(ms + EPS) * g_ref[...]).astype(BF16)
    n_chunks = o_ref.shape[1] // 512
    for c in range(n_chunks):
        cols = slice(c * 512, (c + 1) * 512)
        acc = jnp.dot(h, w_ref[:, cols], preferred_element_type=F32)
        if c < 2:
            sq = (acc * acc).astype(BF16)
            parts = []
            for s in range(2):
                ss = jnp.dot(sq[:, s * 256:(s + 1) * 256], bd_ref[...], preferred_element_type=F32)
                parts.append(ss)
            ss = jnp.concatenate(parts, axis=1)
            acc = acc * lax.rsqrt(ss * (1.0 / DA_HEAD_DIM) + EPS)
        o_ref[:, cols] = (acc * gain_ref[:, cols]).astype(BF16)


def _in_proj(x2, g_attn, w_in_bf, gain_row, bd):
    n, d = x2.shape
    width = w_in_bf.shape[1]
    return pl.pallas_call(
        _in_proj_kernel,
        grid=(n // PROJ_ROWS,),
        in_specs=[
            pl.BlockSpec((PROJ_ROWS, d), lambda i: (i, 0)),
            pl.BlockSpec((1, d), lambda i: (0, 0)),
            pl.BlockSpec((d, width), lambda i: (0, 0)),
            pl.BlockSpec((1, width), lambda i: (0, 0)),
            pl.BlockSpec((256, 256), lambda i: (0, 0)),
        ],
        out_specs=pl.BlockSpec((PROJ_ROWS, width), lambda i: (i, 0)),
        out_shape=jax.ShapeDtypeStruct((n, width), BF16),
        compiler_params=_cparams(("parallel",)),
        name="in_proj",
    )(x2, g_attn, w_in_bf, gain_row, bd)


def _da_kernel(lam_ref, brange_ref, q_ref, k_ref, v_ref, bias_ref, g_ref, o_ref, m_scr, acc_scr, kmax_scr,
               pa_scr, pb_scr, *, out_scale):
    t = DA_BLOCK
    h = pl.program_id(1)
    i = pl.program_id(2)
    s_len = k_ref.shape[1]
    ones_col = jnp.ones((LANES, LANES), BF16)

    @pl.when(i == 0)
    def _():
        lane_k = lax.broadcasted_iota(jnp.int32, (1, LANES), 1)
        rows = 512

        def kbody(c, best):
            kc = k_ref[0, pl.ds(pl.multiple_of(c * rows, rows), rows), :].astype(F32)
            sq = kc * kc
            n1 = jnp.dot(jnp.where(lane_k < DA_HEAD_DIM, sq, 0.0).astype(BF16), ones_col, preferred_element_type=F32)
            n2 = jnp.dot(jnp.where(lane_k >= DA_HEAD_DIM, sq, 0.0).astype(BF16), ones_col, preferred_element_type=F32)
            return (jnp.maximum(best[0], jnp.max(n1, axis=0, keepdims=True)),
                    jnp.maximum(best[1], jnp.max(n2, axis=0, keepdims=True)))

        zero_row = jnp.zeros((1, LANES), F32)
        k1, k2 = lax.fori_loop(0, s_len // rows, kbody, (zero_row, zero_row))
        kmax_scr[0:8, :] = jnp.broadcast_to(k1, (8, LANES))
        kmax_scr[8:16, :] = jnp.broadcast_to(k2, (8, LANES))

    q = q_ref[0]
    lane = lax.broadcasted_iota(jnp.int32, (1, LANES), 1)
    zero = jnp.zeros_like(q)
    qq = jnp.concatenate([jnp.where(lane < DA_HEAD_DIM, q, zero),
                          jnp.where(lane >= DA_HEAD_DIM, q, zero)], axis=0)
    ones = jnp.ones((t, LANES), BF16)
    row = lax.broadcasted_iota(jnp.int32, (t, t), 0)
    col = lax.broadcasted_iota(jnp.int32, (t, t), 1)
    chunk_mask = (col // CHUNK) <= (row // CHUNK)
    n_far = jnp.maximum(i - 1, 0)

    def values(j):
        return jnp.concatenate([v_ref[0, pl.ds(pl.multiple_of(j * t, t), t), :], ones], axis=1)

    def scores(j, bias, mask):
        kb = k_ref[0, pl.ds(pl.multiple_of(j * t, t), t), :]
        s = lax.dot_general(qq, kb, (((1,), (1,)), ((), ())), preferred_element_type=F32)
        if bias is not None:
            s = s + jnp.concatenate([bias, bias], axis=0)
        if mask is not None:
            s = jnp.where(jnp.concatenate([mask, mask], axis=0) if mask.ndim else mask, s, NEG_BIG)
        return s, values(j)

    qf = qq.astype(F32)
    qn2 = jnp.dot((qf * qf).astype(BF16), ones_col, preferred_element_type=F32)
    kmax2 = jnp.concatenate([jnp.broadcast_to(kmax_scr[0:1, :], (t, LANES)),
                             jnp.broadcast_to(kmax_scr[8:9, :], (t, LANES))], axis=0)
    reach = jnp.sqrt(qn2 * kmax2) * 1.02
    b_hi = brange_ref[0, h]
    b_lo = brange_ref[1, h]
    fits = jnp.max(2.0 * reach + (b_hi - b_lo)) <= DA_SAFE_RANGE

    @pl.when(fits)
    def _():
        stab = reach + b_hi
        stab = jnp.concatenate([stab] * (t // LANES), axis=1)

        def probs_into(g, p_scr):
            for u in range(DA_UNROLL):
                s, _ = scores(g * DA_UNROLL + u, None, None)
                p_scr[u] = jnp.exp2(s - stab).astype(BF16)

        def weighted_from(g, p_scr):
            tot = jnp.dot(p_scr[0], values(g * DA_UNROLL), preferred_element_type=F32)
            for u in range(1, DA_UNROLL):
                tot = tot + jnp.dot(p_scr[u], values(g * DA_UNROLL + u), preferred_element_type=F32)
            acc_scr[...] += tot

        n_grp = n_far // DA_UNROLL

        left = n_far - n_grp * DA_UNROLL
        tail = [(jnp.minimum(n_grp * DA_UNROLL + u, jnp.maximum(n_far - 1, 0)), None, u < left)
                for u in range(DA_UNROLL - 1)]
        tail += [(n_far, bias_ref[0, 1], i > 0), (i, bias_ref[0, 0], chunk_mask)]
        tail_scores = [scores(j, bias, mask) for j, bias, mask in tail]
        probs_into(0, pa_scr)
        tot = None
        for s, vb in tail_scores:
            pv = jnp.dot(jnp.exp2(s - stab).astype(BF16), vb, preferred_element_type=F32)
            tot = pv if tot is None else tot + pv
        acc_scr[...] = tot

        def pair_body(hp, carry):
            g = 2 * hp
            weighted_from(g, pa_scr)
            probs_into(g + 1, pb_scr)
            weighted_from(g + 1, pb_scr)
            probs_into(jnp.minimum(g + 2, n_grp - 1), pa_scr)
            return carry

        lax.fori_loop(0, n_grp // 2, pair_body, 0)

        @pl.when(n_grp % 2 == 1)
        def _():
            weighted_from(n_grp - 1, pa_scr)

    @pl.when(jnp.logical_not(fits))
    def _():
        m_scr[...] = jnp.full(m_scr.shape, NEG_BIG, F32)
        acc_scr[...] = jnp.zeros(acc_scr.shape, F32)

        def step(j, bias=None, mask=None):
            s, vb = scores(j, bias, mask)
            m_prev = m_scr[...]
            m_next = jnp.maximum(m_prev, jnp.max(s, axis=1, keepdims=True))
            alpha = jnp.exp2(m_prev - m_next)
            p = jnp.exp2(s - jnp.concatenate([m_next] * (t // LANES), axis=1))
            pv = jnp.dot(p.astype(BF16), vb, preferred_element_type=F32)
            acc_scr[...] = jnp.concatenate([alpha, alpha], axis=1) * acc_scr[...] + pv
            m_scr[...] = m_next

        def far_body(j, carry):
            step(j)
            return carry

        lax.fori_loop(0, n_far, far_body, 0)

        @pl.when(i > 0)
        def _():
            step(i - 1, bias=bias_ref[0, 1])

        step(i, bias=bias_ref[0, 0], mask=chunk_mask)

    acc = acc_scr[...]
    o_all = acc[:, 0:LANES] / acc[:, LANES:2 * LANES]
    o = o_all[0:t] - lam_ref[0] * o_all[t:2 * t]
    ms = jnp.mean(o * o, axis=-1, keepdims=True)
    o_ref[0] = (o * lax.rsqrt(ms + EPS) * (g_ref[...] * out_scale)).astype(BF16)


def _diff_attention(proj3, lam, bias_range, bias_tiles, subln_row, out_scale):
    bsz, s_len, _ = proj3.shape
    t = DA_BLOCK
    nq = s_len // t
    kern = functools.partial(_da_kernel, out_scale=out_scale)
    return pl.pallas_call(
        kern,
        grid=(bsz, DA_HEADS, nq),
        in_specs=[
            pl.BlockSpec(memory_space=pltpu.SMEM),
            pl.BlockSpec(memory_space=pltpu.SMEM),
            pl.BlockSpec((1, t, LANES), lambda b, h, i: (b, i, h)),
            pl.BlockSpec((1, s_len, LANES), lambda b, h, i: (b, 0, DA_HEADS + h)),
            pl.BlockSpec((1, s_len, LANES), lambda b, h, i: (b, 0, 2 * DA_HEADS + h)),
            pl.BlockSpec((1, 2, t, t), lambda b, h, i: (h, 0, 0, 0)),
            pl.BlockSpec((1, LANES), lambda b, h, i: (0, 0)),
        ],
        out_specs=pl.BlockSpec((1, t, LANES), lambda b, h, i: (b, i, h)),
        out_shape=jax.ShapeDtypeStruct((bsz, s_len, DA_HEADS * LANES), BF16),
        scratch_shapes=[
            pltpu.VMEM((2 * t, LANES), F32),
            pltpu.VMEM((2 * t, 2 * LANES), F32),
            pltpu.VMEM((16, LANES), F32),
            pltpu.VMEM((DA_UNROLL, 2 * t, t), BF16),
            pltpu.VMEM((DA_UNROLL, 2 * t, t), BF16),
        ],
        compiler_params=_cparams(("parallel", "parallel", "arbitrary")),
        name="diff_attention",
    )(lam, bias_range, proj3, proj3, proj3, bias_tiles, subln_row)


def _sb_tiles(tiles, tri):
    zs = [lax.dot_general(q, k, (((1,), (1,)), ((), ())), preferred_element_type=F32) for q, k, _, _, _ in tiles]
    mids = []
    for z, (_, _, _, causal, _) in zip(zs, tiles):
        log_1m = -(jnp.maximum(z, 0.0) + jnp.log(1.0 + jnp.exp(-jnp.abs(z))))
        lm = log_1m if causal is None else jnp.where(causal, log_1m, 0.0)
        hi = lm.astype(BF16)
        mid = (lm - hi.astype(F32)).astype(BF16)
        inner = jnp.dot(hi, tri, preferred_element_type=F32) + jnp.dot(mid, tri, preferred_element_type=F32)
        mids.append((z + log_1m + inner, jnp.sum(lm, axis=1, keepdims=True)))
    outs = []
    for (logit, total), (_, _, v, causal, carry) in zip(mids, tiles):
        if carry is not None:
            logit = logit + carry
        a = jnp.exp(logit)
        if causal is not None:
            a = jnp.where(causal, a, 0.0)
        outs.append((jnp.dot(a.astype(BF16), v, preferred_element_type=F32), total))
    return outs


def _sb_kernel(q_ref, k_ref, v_ref, g_ref, tri_ref, o_ref, acc_scr, carry_scr):
    sub, win = SB_SUB, SB_WIN
    i = pl.program_id(2)
    lane = lax.broadcasted_iota(jnp.int32, (1, LANES), 1)
    tri = tri_ref[...]
    row = lax.broadcasted_iota(jnp.int32, (sub, win), 0)
    col = lax.broadcasted_iota(jnp.int32, (sub, win), 1)
    n_sub = SB_ROWS // sub

    tiles, starts = [], []
    for u in range(n_sub):
        r0 = i * SB_ROWS + u * sub
        start = pl.multiple_of(jnp.maximum(r0 - sub, 0), sub)
        qu = q_ref[0, u * sub:(u + 1) * sub, :]
        zero = jnp.zeros_like(qu)
        qs = jnp.concatenate([jnp.where(lane < SB_HEAD_DIM, qu, zero),
                              jnp.where(lane >= SB_HEAD_DIM, qu, zero)], axis=0)
        causal = (start + col) < (r0 + row)
        tiles.append((qs, k_ref[0, pl.ds(start, win), :], v_ref[0, pl.ds(start, win), :],
                      jnp.concatenate([causal, causal], axis=0), None))
        starts.append(start)
    chains = []
    for u, (pv, total) in enumerate(_sb_tiles(tiles, tri)):
        acc_scr[u] = pv
        carry_scr[u] = jnp.broadcast_to(total, (2 * sub, LANES))
        chains.append((u, tiles[u][0], starts[u] // sub - 1, jnp.max(total)))

    tri_sub = tri[0:sub, 0:sub]
    for u, qs, j0, alive0 in chains:
        def cond(state):
            j, alive = state
            return (j >= 0) & (alive > SB_DEAD)

        def body(state, u=u, qs=qs):
            j, _ = state
            s0 = pl.multiple_of(j * sub, sub)
            kb = k_ref[0, pl.ds(s0, sub), :]
            vb = v_ref[0, pl.ds(s0, sub), :]
            carry = carry_scr[u]
            (pv, total), = _sb_tiles([(qs, kb, vb, None, carry)], tri_sub)
            acc_scr[u] += pv
            carry = carry + total
            carry_scr[u] = carry
            return j - 1, jnp.max(carry)

        lax.while_loop(cond, body, (j0, alive0))

    for u in range(n_sub):
        acc = acc_scr[u]
        o = jnp.where(lane < SB_HEAD_DIM, acc[0:sub], acc[sub:2 * sub])
        sq = o * o
        ss0 = jnp.sum(jnp.where(lane < SB_HEAD_DIM, sq, 0.0), axis=1, keepdims=True)
        ss1 = jnp.sum(jnp.where(lane >= SB_HEAD_DIM, sq, 0.0), axis=1, keepdims=True)
        ms = jnp.where(lane < SB_HEAD_DIM, ss0, ss1) * (1.0 / SB_HEAD_DIM)
        o_ref[0, u * sub:(u + 1) * sub, :] = (o * lax.rsqrt(ms + EPS) * g_ref[...]).astype(BF16)


def _stick_breaking(proj3, sb_row, tri):
    bsz, s_len, _ = proj3.shape
    t = SB_ROWS
    nq = s_len // t
    pairs = SB_HEADS // 2
    q_blk = 3 * DA_HEADS
    n_chains = SB_ROWS // SB_SUB
    return pl.pallas_call(
        _sb_kernel,
        grid=(bsz, pairs, nq),
        in_specs=[
            pl.BlockSpec((1, t, LANES), lambda b, h, i: (b, i, q_blk + h)),
            pl.BlockSpec((1, s_len, LANES), lambda b, h, i: (b, 0, q_blk + pairs + h)),
            pl.BlockSpec((1, s_len, LANES), lambda b, h, i: (b, 0, q_blk + 2 * pairs + h)),
            pl.BlockSpec((1, LANES), lambda b, h, i: (0, 0)),
            pl.BlockSpec((SB_WIN, SB_WIN), lambda b, h, i: (0, 0)),
        ],
        out_specs=pl.BlockSpec((1, t, LANES), lambda b, h, i: (b, i, h)),
        out_shape=jax.ShapeDtypeStruct((bsz, s_len, pairs * LANES), BF16),
        scratch_shapes=[
            pltpu.VMEM((n_chains, 2 * SB_SUB, LANES), F32),
            pltpu.VMEM((n_chains, 2 * SB_SUB, LANES), F32),
        ],
        compiler_params=_cparams(("parallel", "parallel", "arbitrary")),
        name="stick_breaking",
    )(proj3, proj3, proj3, sb_row, tri)


def _out_proj_kernel(x_ref, oda_ref, osb_ref, wo_ref, g_ref, wr_ref, br_ref, upper_ref,
                     x1_ref, h2_ref, route_i_ref, route_f_ref, counts_ref, base_scr):
    half = oda_ref.shape[1]
    x1 = (x_ref[...]
          + jnp.dot(oda_ref[...], wo_ref[0:half, :], preferred_element_type=F32)
          + jnp.dot(osb_ref[...], wo_ref[half:2 * half, :], preferred_element_type=F32))
    x1_ref[...] = x1
    ms = jnp.mean(x1 * x1, axis=-1, keepdims=True)
    h2 = x1 * lax.rsqrt(ms + EPS) * g_ref[...]
    d_half = h2.shape[1] // 2
    hi_bits = lax.bitcast_convert_type(h2[:, :d_half].astype(BF16).astype(F32), jnp.uint32)
    lo_bits = lax.bitcast_convert_type(h2[:, d_half:].astype(BF16).astype(F32), jnp.uint32)
    h2_ref[...] = hi_bits | (lo_bits >> 16)

    a_hi = h2.astype(BF16)
    a_mid = (h2 - a_hi.astype(F32)).astype(BF16)
    nt = lambda w, a: lax.dot_general(w, a, (((1,), (1,)), ((), ())), preferred_element_type=F32)
    lt = nt(wr_ref[0], a_hi) + (nt(wr_ref[0], a_mid) + nt(wr_ref[1], a_hi)) + br_ref[...]
    n_rows, tm = lt.shape
    ridx = lax.broadcasted_iota(jnp.int32, (n_rows, tm), 0)
    big = jnp.int32(n_rows)
    is_g = ridx < N_GROUPS
    lg = jnp.where(is_g, lt, NEG_BIG)
    mg = jnp.max(lg, axis=0, keepdims=True)
    gsel = jnp.min(jnp.where(lg == mg, ridx, big), axis=0, keepdims=True)
    pg_sel = 1.0 / jnp.sum(jnp.where(is_g, jnp.exp(lg - mg), 0.0), axis=0, keepdims=True)
    lo_row = N_GROUPS + gsel * EXPERTS_PER_GROUP
    le = jnp.where(ridx >= lo_row, jnp.where(ridx < lo_row + EXPERTS_PER_GROUP, lt, NEG_BIG), NEG_BIG)
    l1 = jnp.max(le, axis=0, keepdims=True)
    i1 = jnp.min(jnp.where(le == l1, ridx, big), axis=0, keepdims=True)
    le2 = jnp.where(ridx == i1, NEG_BIG, le)
    l2 = jnp.max(le2, axis=0, keepdims=True)
    i2 = jnp.min(jnp.where(le2 == l2, ridx, big), axis=0, keepdims=True)
    e2 = jnp.exp(l2 - l1)
    g1 = pg_sel / (1.0 + e2)
    g2 = pg_sel * e2 / (1.0 + e2)

    @pl.when(pl.program_id(0) == 0)
    def _():
        base_scr[...] = jnp.zeros(base_scr.shape, F32)

    sel1 = ridx == i1
    sel2 = ridx == i2
    chosen = jnp.where(sel1, 1.0, jnp.where(sel2, 1.0, 0.0))
    before = jnp.dot(chosen.astype(BF16), upper_ref[...], preferred_element_type=F32)
    base = base_scr[...]
    before = before + jnp.concatenate([base] * (tm // LANES), axis=1)
    rank1 = jnp.sum(jnp.where(sel1, before, 0.0), axis=0, keepdims=True)
    rank2 = jnp.sum(jnp.where(sel2, before, 0.0), axis=0, keepdims=True)
    base = base + jnp.sum(chosen, axis=1, keepdims=True)
    base_scr[...] = base
    counts_ref[...] = base.astype(jnp.int32)

    r8 = lax.broadcasted_iota(jnp.int32, (8, tm), 0)
    route_i_ref[...] = jnp.where(r8 == 0, i1 - N_GROUPS, jnp.where(r8 == 1, i2 - N_GROUPS, jnp.where(
        r8 == 2, rank1.astype(jnp.int32), jnp.where(r8 == 3, rank2.astype(jnp.int32), 0))))
    route_f_ref[...] = jnp.where(r8 == 0, g1, jnp.where(r8 == 1, g2, 0.0))


def _out_proj(x2, o_da, o_sb, wo_bf, g_ffn, wr2, br, upper):
    n, d = x2.shape
    half = o_da.shape[1]
    tm = PROJ_ROWS
    r = ROUTE_ROWS
    return pl.pallas_call(
        _out_proj_kernel,
        grid=(n // tm,),
        in_specs=[
            pl.BlockSpec((tm, d), lambda i: (i, 0)),
            pl.BlockSpec((tm, half), lambda i: (i, 0)),
            pl.BlockSpec((tm, half), lambda i: (i, 0)),
            pl.BlockSpec((2 * half, d), lambda i: (0, 0)),
            pl.BlockSpec((1, d), lambda i: (0, 0)),
            pl.BlockSpec((2, r, d), lambda i: (0, 0, 0)),
            pl.BlockSpec((r, tm), lambda i: (0, 0)),
            pl.BlockSpec((tm, tm), lambda i: (0, 0)),
        ],
        out_specs=[
            pl.BlockSpec((tm, d), lambda i: (i, 0)),
            pl.BlockSpec((tm, d // 2), lambda i: (i, 0)),
            pl.BlockSpec((8, tm), lambda i: (0, i)),
            pl.BlockSpec((8, tm), lambda i: (0, i)),
            pl.BlockSpec((r, LANES), lambda i: (0, 0)),
        ],
        out_shape=[
            jax.ShapeDtypeStruct((n, d), F32),
            jax.ShapeDtypeStruct((n, d // 2), jnp.uint32),
            jax.ShapeDtypeStruct((8, n), jnp.int32),
            jax.ShapeDtypeStruct((8, n), F32),
            jax.ShapeDtypeStruct((r, LANES), jnp.int32),
        ],
        scratch_shapes=[pltpu.VMEM((r, LANES), F32)],
        compiler_params=_cparams(("arbitrary",)),
        name="out_proj_router",
    )(x2, o_da, o_sb, wo_bf, g_ffn, wr2, br, upper)


def _dispatch_kernel(dest_ref, h_ref, xs_in_ref, xs_ref, sem):
    del xs_in_ref
    tt = h_ref.shape[0]

    def copy(r, k):
        return pltpu.make_async_copy(h_ref.at[pl.ds(r, 1)], xs_ref.at[pl.ds(dest_ref[0, 0, 2 * r + k], 1)], sem)

    def issue(r, c):
        copy(r, 0).start()
        copy(r, 1).start()
        return c

    lax.fori_loop(0, tt, issue, 0, unroll=ROW_DMA_UNROLL)

    def drain(r, c):
        copy(r, 0).wait()
        copy(r, 1).wait()
        return c

    lax.fori_loop(0, tt, drain, 0, unroll=True)


def _dispatch(dest2, h2, xs_init):
    n, d = h2.shape
    tt = TOK_TILE
    return pl.pallas_call(
        _dispatch_kernel,
        grid=(n // tt,),
        in_specs=[
            pl.BlockSpec((1, 1, 2 * tt), lambda i: (i, 0, 0), memory_space=pltpu.SMEM),
            pl.BlockSpec((tt, d), lambda i: (i, 0)),
            pl.BlockSpec(memory_space=pl.ANY),
        ],
        out_specs=pl.BlockSpec(memory_space=pl.ANY),
        out_shape=jax.ShapeDtypeStruct(xs_init.shape, xs_init.dtype),
        scratch_shapes=[pltpu.SemaphoreType.DMA(())],
        input_output_aliases={2: 0},
        compiler_params=_cparams(("arbitrary",)),
        name="moe_dispatch",
    )(dest2, h2, xs_init)


def _expert_kernel(blk_e_ref, n_used_ref, xs_ref, w1_ref, w3_ref, w2_ref, ys_ref):
    i = pl.program_id(0)

    @pl.when(i < n_used_ref[0])
    def _():
        rows = MOE_BLOCK // MOE_CHUNKS
        xbs = []
        for c in range(MOE_CHUNKS):
            words = xs_ref[c * rows:(c + 1) * rows, :]
            xbs.append(jnp.concatenate([
                lax.bitcast_convert_type(words & jnp.uint32(0xFFFF0000), F32).astype(BF16),
                lax.bitcast_convert_type(words << 16, F32).astype(BF16)], axis=1))
        gated = [(jnp.dot(xb, w1_ref[0], preferred_element_type=F32), jnp.dot(xb, w3_ref[0], preferred_element_type=F32))
                 for xb in xbs]
        hmids = [(a * jax.nn.sigmoid(a) * b).astype(BF16) for a, b in gated]
        for c, hmid in enumerate(hmids):
            ys_ref[c * rows:(c + 1) * rows, :] = jnp.dot(hmid, w2_ref[0], preferred_element_type=F32)

    @pl.when(i >= n_used_ref[0])
    def _():
        ys_ref[...] = jnp.zeros(ys_ref.shape, F32)


def _experts(blk_e, n_used, xs, w1_bf, w3_bf, w2_bf):
    cap = xs.shape[0]
    d, hid = w1_bf.shape[1], w1_bf.shape[2]
    nb = cap // MOE_BLOCK

    def row_map(i, blk_e_ref, n_used_ref):
        return (jnp.minimum(i, n_used_ref[0] - 1), 0)

    def w_map(i, blk_e_ref, n_used_ref):
        return (blk_e_ref[jnp.minimum(i, n_used_ref[0] - 1)], 0, 0)

    grid_spec = pltpu.PrefetchScalarGridSpec(
        num_scalar_prefetch=2,
        grid=(nb,),
        in_specs=[
            pl.BlockSpec((MOE_BLOCK, d // 2), row_map),
            pl.BlockSpec((1, d, hid), w_map),
            pl.BlockSpec((1, d, hid), w_map),
            pl.BlockSpec((1, hid, d), w_map),
        ],
        out_specs=pl.BlockSpec((MOE_BLOCK, d), lambda i, blk_e_ref, n_used_ref: (i, 0)),
    )
    return pl.pallas_call(
        _expert_kernel,
        grid_spec=grid_spec,
        out_shape=jax.ShapeDtypeStruct((cap, d), F32),
        compiler_params=_cparams(("arbitrary",)),
        name="moe_experts",
    )(blk_e, n_used, xs, w1_bf, w3_bf, w2_bf)


def _combine_kernel(dest_ref, x1_ref, gates_ref, ys_ref, o_ref, y0_scr, y1_scr, sem):
    tt = x1_ref.shape[0]

    def copy(r, k):
        dst = y0_scr if k == 0 else y1_scr
        return pltpu.make_async_copy(ys_ref.at[pl.ds(dest_ref[0, 0, 2 * r + k], 1)], dst.at[pl.ds(r, 1)], sem)

    def issue(r, c):
        copy(r, 0).start()
        copy(r, 1).start()
        return c

    lax.fori_loop(0, tt, issue, 0, unroll=ROW_DMA_UNROLL)

    def drain(r, c):
        copy(r, 0).wait()
        copy(r, 1).wait()
        return c

    lax.fori_loop(0, tt, drain, 0, unroll=True)
    g = gates_ref[...]
    o_ref[...] = x1_ref[...] + y0_scr[...] * g[:, 0:1] + y1_scr[...] * g[:, 1:2]


def _combine(dest2, x1, gates, ys):
    n, d = x1.shape
    tt = TOK_TILE
    return pl.pallas_call(
        _combine_kernel,
        grid=(n // tt,),
        in_specs=[
            pl.BlockSpec((1, 1, 2 * tt), lambda i: (i, 0, 0), memory_space=pltpu.SMEM),
            pl.BlockSpec((tt, d), lambda i: (i, 0)),
            pl.BlockSpec((tt, LANES), lambda i: (i, 0)),
            pl.BlockSpec(memory_space=pl.ANY),
        ],
        out_specs=pl.BlockSpec((tt, d), lambda i: (i, 0)),
        out_shape=jax.ShapeDtypeStruct((n, d), F32),
        scratch_shapes=[
            pltpu.VMEM((tt, d), F32),
            pltpu.VMEM((tt, d), F32),
            pltpu.SemaphoreType.DMA(()),
        ],
        compiler_params=_cparams(("arbitrary",)),
        name="moe_combine",
    )(dest2, x1, gates, ys)


def _rel_bucket_np(rel):
    nb = REL_BUCKETS // 2
    max_exact = nb // 2
    base = np.where(rel > 0, nb, 0)
    n = np.abs(rel)
    nf = np.maximum(n, 1).astype(np.float64)
    large = max_exact + (np.log(nf / max_exact) / math.log(REL_MAX_DIST / max_exact) * (nb - max_exact)).astype(np.int64)
    large = np.minimum(large, nb - 1)
    return (base + np.where(n < max_exact, n, large)).astype(np.int32)


def _bias_bucket_rows(t):
    far = _rel_bucket_np(-np.arange(t + 1, 4 * t))
    far_bucket = int(far[0])
    assert (far == far_bucket).all()
    m = np.arange(2 * t)
    wrap = np.where(m < t, m, m - 2 * t)
    return np.stack([_rel_bucket_np(wrap), _rel_bucket_np(wrap - t)]), far_bucket


def _toeplitz(rows, t):
    lead = rows.shape[:-1]
    flat = jnp.tile(rows, (1,) * len(lead) + (t,))[..., : t * (2 * t - 1)]
    return flat.reshape(lead + (t, 2 * t - 1))[..., :t]


def kernel(x, g_attn, w_in, qn_g, kn_g, lam_q1, lam_k1, lam_q2, lam_k2, subln_g, sb_out_g, rel_bias, w_o,
           g_ffn, w_router_g, b_router_g, w_router_e, b_router_e, w1, w3, w2):
    bsz, s_len, d = x.shape
    n = bsz * s_len
    depth = g_attn.shape[0]
    da_width = DA_HEADS * 2 * DA_HEAD_DIM
    sb_width = SB_HEADS * SB_HEAD_DIM

    bd = jnp.asarray(np.kron(np.eye(256 // DA_HEAD_DIM), np.ones((DA_HEAD_DIM, DA_HEAD_DIM))), BF16)
    tri = jnp.asarray(np.tril(np.ones((SB_WIN, SB_WIN)), -1), BF16)
    bucket_rows, far_bucket = _bias_bucket_rows(DA_BLOCK)
    upper = jnp.asarray(np.triu(np.ones((PROJ_ROWS, PROJ_ROWS)), 1), BF16)

    for l in range(depth):
        lambda_init = 0.8 - 0.6 * math.exp(-0.3 * l)
        x2 = x.reshape(n, d)

        q_scale = DA_HEAD_DIM ** -0.5 * LOG2E
        gain_row = jnp.concatenate([
            jnp.tile(qn_g[l] * q_scale, 2 * DA_HEADS),
            jnp.tile(kn_g[l], 2 * DA_HEADS),
            jnp.ones((da_width,), F32),
            jnp.full((sb_width,), SB_HEAD_DIM ** -0.5, F32),
            jnp.ones((2 * sb_width,), F32),
        ])[None, :]
        proj = _in_proj(x2, g_attn[l][None, :], w_in[l].astype(BF16), gain_row, bd)
        proj3 = proj.reshape(bsz, s_len, -1)

        lam = (jnp.exp(jnp.sum(lam_q1[l] * lam_k1[l])) - jnp.exp(jnp.sum(lam_q2[l] * lam_k2[l]))
               + lambda_init).astype(F32).reshape(1)
        rb = (rel_bias - rel_bias[far_bucket][None, :]) * LOG2E
        bias_rows = jnp.transpose(rb[bucket_rows], (2, 0, 1)).astype(F32)
        bias_tiles = _toeplitz(bias_rows, DA_BLOCK)
        bias_range = jnp.stack([jnp.maximum(jnp.max(rb, axis=0), 0.0), jnp.minimum(jnp.min(rb, axis=0), 0.0)])
        o_da = _diff_attention(proj3, lam, bias_range.astype(F32), bias_tiles, subln_g[l][None, :],
                               1.0 - lambda_init)
        o_sb = _stick_breaking(proj3, jnp.tile(sb_out_g[l], 2)[None, :], tri)

        wr = jnp.concatenate([w_router_g[l], w_router_e[l],
                              jnp.zeros((d, ROUTE_ROWS - N_GROUPS - N_EXPERTS), F32)], axis=1).T
        wr_hi = wr.astype(BF16)
        wr_mid = (wr - wr_hi.astype(F32)).astype(BF16)
        br = jnp.concatenate([b_router_g[l], b_router_e[l],
                              jnp.zeros((ROUTE_ROWS - N_GROUPS - N_EXPERTS,), F32)])
        br = jnp.broadcast_to(br[:, None], (ROUTE_ROWS, PROJ_ROWS))
        x1, h2, route_i, route_f, counts_all = _out_proj(
            x2, o_da.reshape(n, -1), o_sb.reshape(n, -1), w_o[l].astype(BF16), g_ffn[l][None, :],
            jnp.stack([wr_hi, wr_mid]), br, upper)

        counts = counts_all[N_GROUPS:N_GROUPS + N_EXPERTS, 0]
        padded = (counts + MOE_BLOCK - 1) // MOE_BLOCK * MOE_BLOCK
        pad_ends = jnp.cumsum(padded)
        pad_starts = pad_ends - padded
        experts = jnp.arange(N_EXPERTS, dtype=jnp.int32)
        eid, rank = route_i[0:2], route_i[2:4]
        start_of = jnp.sum(jnp.where(eid[:, :, None] == experts, pad_starts, 0), axis=-1)
        dest2 = (start_of + rank).T.reshape(n // TOK_TILE, 1, 2 * TOK_TILE)
        gates = jnp.pad(route_f[0:2].T, ((0, 0), (0, LANES - 2)))
        cap = 2 * n + N_EXPERTS * MOE_BLOCK
        nb = cap // MOE_BLOCK
        blk_start = jnp.arange(nb, dtype=jnp.int32) * MOE_BLOCK
        blk_e = jnp.minimum(jnp.sum(pad_ends[None, :] <= blk_start[:, None], axis=1), N_EXPERTS - 1).astype(jnp.int32)
        n_used = (pad_ends[-1] // MOE_BLOCK).astype(jnp.int32).reshape(1)

        xs = _dispatch(dest2, h2, jnp.zeros((cap, d // 2), jnp.uint32))
        ys = _experts(blk_e, n_used, xs, w1[l].astype(BF16), w3[l].astype(BF16), w2[l].astype(BF16))
        x = _combine(dest2, x1, gates, ys).reshape(bsz, s_len, d)
    return x
```

```python
import functools
import math

import jax
import jax.numpy as jnp
import numpy as np
from jax import lax
from jax.experimental import pallas as pl
from jax.experimental.pallas import tpu as pltpu

F32 = jnp.float32
BF16 = jnp.bfloat16

EPS = 1e-6
LANES = 128
DA_HEADS = 4
DA_HEAD_DIM = 64
SB_HEADS = 8
SB_HEAD_DIM = 64
CHUNK = 64
REL_BUCKETS = 32
REL_MAX_DIST = 128
N_GROUPS = 4
EXPERTS_PER_GROUP = 8
N_EXPERTS = N_GROUPS * EXPERTS_PER_GROUP
NEG_BIG = -1e30
LOG2E = math.log2(math.e)
SB_DEAD = -87.5

VMEM_LIMIT = 48 * 1024 * 1024

PROJ_ROWS = 512
ROUTE_ROWS = 48
DA_BLOCK = 256
DA_UNROLL = 4
DA_SAFE_RANGE = 100.0
SB_SUB = 128
SB_ROWS = 1024
SB_WIN = 256
MOE_BLOCK = 512
MOE_CHUNKS = 2
TOK_TILE = 512


def _cparams(sem):
    return pltpu.CompilerParams(dimension_semantics=sem, vmem_limit_bytes=VMEM_LIMIT)


def _in_proj_kernel(x_ref, g_ref, w_ref, gain_ref, bd_ref, o_ref):
    x = x_ref[...]
    ms = jnp.mean(x * x, axis=-1, keepdims=True)
    h = (x * lax.rsqrt(ms + EPS) * g_ref[...]).astype(BF16)
    n_chunks = o_ref.shape[1] // 512
    for c in range(n_chunks):
        cols = slice(c * 512, (c + 1) * 512)
        acc = jnp.dot(h, w_ref[:, cols], preferred_element_type=F32)
        if c < 2:
            sq = (acc * acc).astype(BF16)
            parts = []
            for s in range(2):
                ss = jnp.dot(sq[:, s * 256:(s + 1) * 256], bd_ref[...], preferred_element_type=F32)
                parts.append(ss)
            ss = jnp.concatenate(parts, axis=1)
            acc = acc * lax.rsqrt(ss * (1.0 / DA_HEAD_DIM) + EPS)
        o_ref[:, cols] = (acc * gain_ref[:, cols]).astype(BF16)


def _in_proj(x2, g_attn, w_in_bf, gain_row, bd):
    n, d = x2.shape
    width = w_in_bf.shape[1]
    return pl.pallas_call(
        _in_proj_kernel,
        grid=(n // PROJ_ROWS,),
        in_specs=[
            pl.BlockSpec((PROJ_ROWS, d), lambda i: (i, 0)),
            pl.BlockSpec((1, d), lambda i: (0, 0)),
            pl.BlockSpec((d, width), lambda i: (0, 0)),
            pl.BlockSpec((1, width), lambda i: (0, 0)),
            pl.BlockSpec((256, 256), lambda i: (0, 0)),
        ],
        out_specs=pl.BlockSpec((PROJ_ROWS, width), lambda i: (i, 0)),
        out_shape=jax.ShapeDtypeStruct((n, width), BF16),
        compiler_params=_cparams(("parallel",)),
        name="in_proj",
    )(x2, g_attn, w_in_bf, gain_row, bd)


def _da_kernel(lam_ref, brange_ref, q_ref, k_ref, v_ref, bias_ref, g_ref, o_ref, m_scr, acc_scr, stab_scr,
               fits_scr, pa_scr, pb_scr, *, out_scale):
    t = DA_BLOCK
    h = pl.program_id(1)
    i = pl.program_id(2)
    s_len = k_ref.shape[1]
    ones_col = jnp.ones((LANES, LANES), BF16)

    @pl.when(i == 0)
    def _():
        lane_k = lax.broadcasted_iota(jnp.int32, (1, LANES), 1)
        rows = 512

        def kbody(c, best):
            kc = k_ref[0, pl.ds(pl.multiple_of(c * rows, rows), rows), :].astype(F32)
            sq = kc * kc
            n1 = jnp.dot(jnp.where(lane_k < DA_HEAD_DIM, sq, 0.0).astype(BF16), ones_col, preferred_element_type=F32)
            n2 = jnp.dot(jnp.where(lane_k >= DA_HEAD_DIM, sq, 0.0).astype(BF16), ones_col, preferred_element_type=F32)
            return (jnp.maximum(best[0], jnp.max(n1, axis=0, keepdims=True)),
                    jnp.maximum(best[1], jnp.max(n2, axis=0, keepdims=True)))

        zero_row = jnp.zeros((1, LANES), F32)
        k1, k2 = lax.fori_loop(0, s_len // rows, kbody, (zero_row, zero_row))
        reach1 = jnp.max(jnp.sqrt(k1)) * (brange_ref[2, h] * 1.02)
        reach2 = jnp.max(jnp.sqrt(k2)) * (brange_ref[2, h] * 1.02)
        stab_scr[0] = reach1 + brange_ref[0, h]
        stab_scr[1] = reach2 + brange_ref[0, h]
        spread = 2.0 * jnp.maximum(reach1, reach2) + (brange_ref[0, h] - brange_ref[1, h])
        fits_scr[0] = (spread <= DA_SAFE_RANGE).astype(jnp.int32)

    q = q_ref[0]
    lane = lax.broadcasted_iota(jnp.int32, (1, LANES), 1)
    zero = jnp.zeros_like(q)
    qq = jnp.concatenate([jnp.where(lane < DA_HEAD_DIM, q, zero),
                          jnp.where(lane >= DA_HEAD_DIM, q, zero)], axis=0)
    ones = jnp.ones((t, LANES), BF16)
    row = lax.broadcasted_iota(jnp.int32, (t, t), 0)
    col = lax.broadcasted_iota(jnp.int32, (t, t), 1)
    chunk_mask = (col // CHUNK) <= (row // CHUNK)
    n_far = jnp.maximum(i - 1, 0)

    def values(j):
        return jnp.concatenate([v_ref[0, pl.ds(pl.multiple_of(j * t, t), t), :], ones], axis=1)

    def scores(j, bias, mask):
        kb = k_ref[0, pl.ds(pl.multiple_of(j * t, t), t), :]
        s = lax.dot_general(qq, kb, (((1,), (1,)), ((), ())), preferred_element_type=F32)
        if bias is not None:
            s = s + jnp.concatenate([bias, bias], axis=0)
        if mask is not None:
            s = jnp.where(jnp.concatenate([mask, mask], axis=0) if mask.ndim else mask, s, NEG_BIG)
        return s, values(j)

    fits = fits_scr[0] != 0

    @pl.when(fits)
    def _():
        stab = jnp.concatenate([jnp.full((t, t), stab_scr[0], F32), jnp.full((t, t), stab_scr[1], F32)], axis=0)

        def probs_into(g, p_scr):
            for u in range(DA_UNROLL):
                s, _ = scores(g * DA_UNROLL + u, None, None)
                p_scr[u] = jnp.exp2(s - stab).astype(BF16)

        def weighted_from(g, p_scr):
            tot = jnp.dot(p_scr[0], values(g * DA_UNROLL), preferred_element_type=F32)
            for u in range(1, DA_UNROLL):
                tot = tot + jnp.dot(p_scr[u], values(g * DA_UNROLL + u), preferred_element_type=F32)
            acc_scr[...] += tot

        n_grp = n_far // DA_UNROLL

        left = n_far - n_grp * DA_UNROLL
        tail = [(jnp.minimum(n_grp * DA_UNROLL + u, jnp.maximum(n_far - 1, 0)), None, u < left)
                for u in range(DA_UNROLL - 1)]
        tail += [(n_far, bias_ref[0, 1], i > 0), (i, bias_ref[0, 0], chunk_mask)]
        tail_scores = [scores(j, bias, mask) for j, bias, mask in tail]
        probs_into(0, pa_scr)
        tot = None
        for s, vb in tail_scores:
            pv = jnp.dot(jnp.exp2(s - stab).astype(BF16), vb, preferred_element_type=F32)
            tot = pv if tot is None else tot + pv
        acc_scr[...] = tot

        def pair_body(hp, carry):
            g = 2 * hp
            weighted_from(g, pa_scr)
            probs_into(g + 1, pb_scr)
            weighted_from(g + 1, pb_scr)
            probs_into(jnp.minimum(g + 2, n_grp - 1), pa_scr)
            return carry

        lax.fori_loop(0, n_grp // 2, pair_body, 0)

        @pl.when(n_grp % 2 == 1)
        def _():
            weighted_from(n_grp - 1, pa_scr)

    @pl.when(jnp.logical_not(fits))
    def _():
        m_scr[...] = jnp.full(m_scr.shape, NEG_BIG, F32)
        acc_scr[...] = jnp.zeros(acc_scr.shape, F32)

        def step(j, bias=None, mask=None):
            s, vb = scores(j, bias, mask)
            m_prev = m_scr[...]
            m_next = jnp.maximum(m_prev, jnp.max(s, axis=1, keepdims=True))
            alpha = jnp.exp2(m_prev - m_next)
            p = jnp.exp2(s - jnp.concatenate([m_next] * (t // LANES), axis=1))
            pv = jnp.dot(p.astype(BF16), vb, preferred_element_type=F32)
            acc_scr[...] = jnp.concatenate([alpha, alpha], axis=1) * acc_scr[...] + pv
            m_scr[...] = m_next

        def far_body(j, carry):
            step(j)
            return carry

        lax.fori_loop(0, n_far, far_body, 0)

        @pl.when(i > 0)
        def _():
            step(i - 1, bias=bias_ref[0, 1])

        step(i, bias=bias_ref[0, 0], mask=chunk_mask)

    acc = acc_scr[...]
    o_all = acc[:, 0:LANES] / acc[:, LANES:2 * LANES]
    o = o_all[0:t] - lam_ref[0] * o_all[t:2 * t]
    ms = jnp.mean(o * o, axis=-1, keepdims=True)
    o_ref[0] = (o * lax.rsqrt(ms + EPS) * (g_ref[...] * out_scale)).astype(BF16)


def _diff_attention(proj3, lam, bias_range, bias_tiles, subln_row, out_scale):
    bsz, s_len, _ = proj3.shape
    t = DA_BLOCK
    nq = s_len // t
    kern = functools.partial(_da_kernel, out_scale=out_scale)
    return pl.pallas_call(
        kern,
        grid=(bsz, DA_HEADS, nq),
        in_specs=[
            pl.BlockSpec(memory_space=pltpu.SMEM),
            pl.BlockSpec(memory_space=pltpu.SMEM),
            pl.BlockSpec((1, t, LANES), lambda b, h, i: (b, i, h)),
            pl.BlockSpec((1, s_len, LANES), lambda b, h, i: (b, 0, DA_HEADS + h)),
            pl.BlockSpec((1, s_len, LANES), lambda b, h, i: (b, 0, 2 * DA_HEADS + h)),
            pl.BlockSpec((1, 2, t, t), lambda b, h, i: (h, 0, 0, 0)),
            pl.BlockSpec((1, LANES), lambda b, h, i: (0, 0)),
        ],
        out_specs=pl.BlockSpec((1, t, LANES), lambda b, h, i: (b, i, h)),
        out_shape=jax.ShapeDtypeStruct((bsz, s_len, DA_HEADS * LANES), BF16),
        scratch_shapes=[
            pltpu.VMEM((2 * t, LANES), F32),
            pltpu.VMEM((2 * t, 2 * LANES), F32),
            pltpu.SMEM((2,), F32),
            pltpu.SMEM((1,), jnp.int32),
            pltpu.VMEM((DA_UNROLL, 2 * t, t), BF16),
            pltpu.VMEM((DA_UNROLL, 2 * t, t), BF16),
        ],
        compiler_params=_cparams(("parallel", "parallel", "arbitrary")),
        name="diff_attention",
    )(lam, bias_range, proj3, proj3, proj3, bias_tiles, subln_row)


def _sb_tiles(tiles, tri):
    zs = [lax.dot_general(q, k, (((1,), (1,)), ((), ())), preferred_element_type=F32) for q, k, _, _, _ in tiles]
    mids = []
    for z, (_, _, _, causal, _) in zip(zs, tiles):
        log_1m = -(jnp.maximum(z, 0.0) + jnp.log(1.0 + jnp.exp(-jnp.abs(z))))
        lm = log_1m if causal is None else jnp.where(causal, log_1m, 0.0)
        hi = lm.astype(BF16)
        mid = (lm - hi.astype(F32)).astype(BF16)
        inner = jnp.dot(hi, tri, preferred_element_type=F32) + jnp.dot(mid, tri, preferred_element_type=F32)
        mids.append((z + log_1m + inner, jnp.sum(lm, axis=1, keepdims=True)))
    outs = []
    for (logit, total), (_, _, v, causal, carry) in zip(mids, tiles):
        if carry is not None:
            logit = logit + carry
        a = jnp.exp(logit)
        if causal is not None:
            a = jnp.where(causal, a, 0.0)
        outs.append((jnp.dot(a.astype(BF16), v, preferred_element_type=F32), total))
    return outs


def _sb_kernel(q_ref, k_ref, v_ref, g_ref, tri_ref, o_ref, acc_scr, carry_scr):
    sub, win = SB_SUB, SB_WIN
    i = pl.program_id(2)
    lane = lax.broadcasted_iota(jnp.int32, (1, LANES), 1)
    tri = tri_ref[...]
    row = lax.broadcasted_iota(jnp.int32, (sub, win), 0)
    col = lax.broadcasted_iota(jnp.int32, (sub, win), 1)
    n_sub = SB_ROWS // sub

    tiles, starts = [], []
    for u in range(n_sub):
        r0 = i * SB_ROWS + u * sub
        start = pl.multiple_of(jnp.maximum(r0 - sub, 0), sub)
        qu = q_ref[0, u * sub:(u + 1) * sub, :]
        zero = jnp.zeros_like(qu)
        qs = jnp.concatenate([jnp.where(lane < SB_HEAD_DIM, qu, zero),
                              jnp.where(lane >= SB_HEAD_DIM, qu, zero)], axis=0)
        causal = (start + col) < (r0 + row)
        tiles.append((qs, k_ref[0, pl.ds(start, win), :], v_ref[0, pl.ds(start, win), :],
                      jnp.concatenate([causal, causal], axis=0), None))
        starts.append(start)
    chains = []
    for u, (pv, total) in enumerate(_sb_tiles(tiles, tri)):
        acc_scr[u] = pv
        carry_scr[u] = jnp.broadcast_to(total, (2 * sub, LANES))
        chains.append((u, tiles[u][0], starts[u] // sub - 1, jnp.max(total)))

    tri_sub = tri[0:sub, 0:sub]
    for u, qs, j0, alive0 in chains:
        def cond(state):
            j, alive = state
            return (j >= 0) & (alive > SB_DEAD)

        def body(state, u=u, qs=qs):
            j, _ = state
            s0 = pl.multiple_of(j * sub, sub)
            kb = k_ref[0, pl.ds(s0, sub), :]
            vb = v_ref[0, pl.ds(s0, sub), :]
            carry = carry_scr[u]
            (pv, total), = _sb_tiles([(qs, kb, vb, None, carry)], tri_sub)
            acc_scr[u] += pv
            carry = carry + total
            carry_scr[u] = carry
            return j - 1, jnp.max(carry)

        lax.while_loop(cond, body, (j0, alive0))

    for u in range(n_sub):
        acc = acc_scr[u]
        o = jnp.where(lane < SB_HEAD_DIM, acc[0:sub], acc[sub:2 * sub])
        sq = o * o
        ss0 = jnp.sum(jnp.where(lane < SB_HEAD_DIM, sq, 0.0), axis=1, keepdims=True)
        ss1 = jnp.sum(jnp.where(lane >= SB_HEAD_DIM, sq, 0.0), axis=1, keepdims=True)
        ms = jnp.where(lane < SB_HEAD_DIM, ss0, ss1) * (1.0 / SB_HEAD_DIM)
        o_ref[0, u * sub:(u + 1) * sub, :] = (o * lax.rsqrt(ms + EPS) * g_ref[...]).astype(BF16)


def _stick_breaking(proj3, sb_row, tri):
    bsz, s_len, _ = proj3.shape
    t = SB_ROWS
    nq = s_len // t
    pairs = SB_HEADS // 2
    q_blk = 3 * DA_HEADS
    n_chains = SB_ROWS // SB_SUB
    return pl.pallas_call(
        _sb_kernel,
        grid=(bsz, pairs, nq),
        in_specs=[
            pl.BlockSpec((1, t, LANES), lambda b, h, i: (b, i, q_blk + h)),
            pl.BlockSpec((1, s_len, LANES), lambda b, h, i: (b, 0, q_blk + pairs + h)),
            pl.BlockSpec((1, s_len, LANES), lambda b, h, i: (b, 0, q_blk + 2 * pairs + h)),
            pl.BlockSpec((1, LANES), lambda b, h, i: (0, 0)),
            pl.BlockSpec((SB_WIN, SB_WIN), lambda b, h, i: (0, 0)),
        ],
        out_specs=pl.BlockSpec((1, t, LANES), lambda b, h, i: (b, i, h)),
        out_shape=jax.ShapeDtypeStruct((bsz, s_len, pairs * LANES), BF16),
        scratch_shapes=[
            pltpu.VMEM((n_chains, 2 * SB_SUB, LANES), F32),
            pltpu.VMEM((n_chains, 2 * SB_SUB, LANES), F32),
        ],
        compiler_params=_cparams(("parallel", "parallel", "arbitrary")),
        name="stick_breaking",
    )(proj3, proj3, proj3, sb_row, tri)


def _out_proj_kernel(x_ref, oda_ref, osb_ref, wo_ref, g_ref, wr_ref, br_ref, upper_ref,
                     x1_ref, h2_ref, route_i_ref, route_f_ref, counts_ref, base_scr):
    half = oda_ref.shape[1]
    x1 = (x_ref[...]
          + jnp.dot(oda_ref[...], wo_ref[0:half, :], preferred_element_type=F32)
          + jnp.dot(osb_ref[...], wo_ref[half:2 * half, :], preferred_element_type=F32))
    x1_ref[...] = x1
    ms = jnp.mean(x1 * x1, axis=-1, keepdims=True)
    h2 = x1 * lax.rsqrt(ms + EPS) * g_ref[...]
    d_half = h2.shape[1] // 2
    hi_bits = lax.bitcast_convert_type(h2[:, :d_half].astype(BF16).astype(F32), jnp.uint32)
    lo_bits = lax.bitcast_convert_type(h2[:, d_half:].astype(BF16).astype(F32), jnp.uint32)
    h2_ref[...] = hi_bits | (lo_bits >> 16)

    a_hi = h2.astype(BF16)
    a_mid = (h2 - a_hi.astype(F32)).astype(BF16)
    nt = lambda w, a: lax.dot_general(w, a, (((1,), (1,)), ((), ())), preferred_element_type=F32)
    lt = nt(wr_ref[0], a_hi) + (nt(wr_ref[0], a_mid) + nt(wr_ref[1], a_hi)) + br_ref[...]
    n_rows, tm = lt.shape
    ridx = lax.broadcasted_iota(jnp.int32, (n_rows, tm), 0)
    big = jnp.int32(n_rows)
    is_g = ridx < N_GROUPS
    lg = jnp.where(is_g, lt, NEG_BIG)
    mg = jnp.max(lg, axis=0, keepdims=True)
    gsel = jnp.min(jnp.where(lg == mg, ridx, big), axis=0, keepdims=True)
    pg_sel = 1.0 / jnp.sum(jnp.where(is_g, jnp.exp(lg - mg), 0.0), axis=0, keepdims=True)
    lo_row = N_GROUPS + gsel * EXPERTS_PER_GROUP
    le = jnp.where(ridx >= lo_row, jnp.where(ridx < lo_row + EXPERTS_PER_GROUP, lt, NEG_BIG), NEG_BIG)
    l1 = jnp.max(le, axis=0, keepdims=True)
    i1 = jnp.min(jnp.where(le == l1, ridx, big), axis=0, keepdims=True)
    le2 = jnp.where(ridx == i1, NEG_BIG, le)
    l2 = jnp.max(le2, axis=0, keepdims=True)
    i2 = jnp.min(jnp.where(le2 == l2, ridx, big), axis=0, keepdims=True)
    e2 = jnp.exp(l2 - l1)
    g1 = pg_sel / (1.0 + e2)
    g2 = pg_sel * e2 / (1.0 + e2)

    @pl.when(pl.program_id(0) == 0)
    def _():
        base_scr[...] = jnp.zeros(base_scr.shape, F32)

    sel1 = ridx == i1
    sel2 = ridx == i2
    chosen = jnp.where(sel1, 1.0, jnp.where(sel2, 1.0, 0.0))
    before = jnp.dot(chosen.astype(BF16), upper_ref[...], preferred_element_type=F32)
    base = base_scr[...]
    before = before + jnp.concatenate([base] * (tm // LANES), axis=1)
    rank1 = jnp.sum(jnp.where(sel1, before, 0.0), axis=0, keepdims=True)
    rank2 = jnp.sum(jnp.where(sel2, before, 0.0), axis=0, keepdims=True)
    base = base + jnp.sum(chosen, axis=1, keepdims=True)
    base_scr[...] = base
    counts_ref[...] = base.astype(jnp.int32)

    r8 = lax.broadcasted_iota(jnp.int32, (8, tm), 0)
    route_i_ref[...] = jnp.where(r8 == 0, i1 - N_GROUPS, jnp.where(r8 == 1, i2 - N_GROUPS, jnp.where(
        r8 == 2, rank1.astype(jnp.int32), jnp.where(r8 == 3, rank2.astype(jnp.int32), 0))))
    route_f_ref[...] = jnp.where(r8 == 0, g1, jnp.where(r8 == 1, g2, 0.0))


def _out_proj(x2, o_da, o_sb, wo_bf, g_ffn, wr2, br, upper):
    n, d = x2.shape
    half = o_da.shape[1]
    tm = PROJ_ROWS
    r = ROUTE_ROWS
    return pl.pallas_call(
        _out_proj_kernel,
        grid=(n // tm,),
        in_specs=[
            pl.BlockSpec((tm, d), lambda i: (i, 0)),
            pl.BlockSpec((tm, half), lambda i: (i, 0)),
            pl.BlockSpec((tm, half), lambda i: (i, 0)),
            pl.BlockSpec((2 * half, d), lambda i: (0, 0)),
            pl.BlockSpec((1, d), lambda i: (0, 0)),
            pl.BlockSpec((2, r, d), lambda i: (0, 0, 0)),
            pl.BlockSpec((r, tm), lambda i: (0, 0)),
            pl.BlockSpec((tm, tm), lambda i: (0, 0)),
        ],
        out_specs=[
            pl.BlockSpec((tm, d), lambda i: (i, 0)),
            pl.BlockSpec((tm, d // 2), lambda i: (i, 0)),
            pl.BlockSpec((8, tm), lambda i: (0, i)),
            pl.BlockSpec((8, tm), lambda i: (0, i)),
            pl.BlockSpec((r, LANES), lambda i: (0, 0)),
        ],
        out_shape=[
            jax.ShapeDtypeStruct((n, d), F32),
            jax.ShapeDtypeStruct((n, d // 2), jnp.uint32),
            jax.ShapeDtypeStruct((8, n), jnp.int32),
            jax.ShapeDtypeStruct((8, n), F32),
            jax.ShapeDtypeStruct((r, LANES), jnp.int32),
        ],
        scratch_shapes=[pltpu.VMEM((r, LANES), F32)],
        compiler_params=_cparams(("arbitrary",)),
        name="out_proj_router",
    )(x2, o_da, o_sb, wo_bf, g_ffn, wr2, br, upper)


def _dispatch_kernel(dest_ref, h_ref, xs_in_ref, xs_ref, sem):
    del xs_in_ref
    tt = h_ref.shape[0]

    def copy(r, k):
        return pltpu.make_async_copy(h_ref.at[pl.ds(r, 1)], xs_ref.at[pl.ds(dest_ref[0, 0, 2 * r + k], 1)], sem)

    for r in range(tt):
        copy(r, 0).start(priority=0)
        copy(r, 1).start(priority=1)

    def drain(r, c):
        copy(r, 0).wait()
        copy(r, 1).wait()
        return c

    lax.fori_loop(0, tt, drain, 0, unroll=True)


def _dispatch(dest2, h2, xs_init):
    n, d = h2.shape
    tt = TOK_TILE
    return pl.pallas_call(
        _dispatch_kernel,
        grid=(n // tt,),
        in_specs=[
            pl.BlockSpec((1, 1, 2 * tt), lambda i: (i, 0, 0), memory_space=pltpu.SMEM),
            pl.BlockSpec((tt, d), lambda i: (i, 0)),
            pl.BlockSpec(memory_space=pl.ANY),
        ],
        out_specs=pl.BlockSpec(memory_space=pl.ANY),
        out_shape=jax.ShapeDtypeStruct(xs_init.shape, xs_init.dtype),
        scratch_shapes=[pltpu.SemaphoreType.DMA(())],
        input_output_aliases={2: 0},
        compiler_params=_cparams(("arbitrary",)),
        name="moe_dispatch",
    )(dest2, h2, xs_init)


def _expert_kernel(blk_e_ref, n_used_ref, xs_ref, w1_ref, w3_ref, w2_ref, ys_ref):
    i = pl.program_id(0)

    @pl.when(i < n_used_ref[0])
    def _():
        rows = MOE_BLOCK // MOE_CHUNKS
        xbs = []
        for c in range(MOE_CHUNKS):
            words = xs_ref[c * rows:(c + 1) * rows, :]
            xbs.append(jnp.concatenate([
                lax.bitcast_convert_type(words & jnp.uint32(0xFFFF0000), F32).astype(BF16),
                lax.bitcast_convert_type(words << 16, F32).astype(BF16)], axis=1))
        gated = [(jnp.dot(xb, w1_ref[0], preferred_element_type=F32), jnp.dot(xb, w3_ref[0], preferred_element_type=F32))
                 for xb in xbs]
        hmids = [(a * jax.nn.sigmoid(a) * b).astype(BF16) for a, b in gated]
        for c, hmid in enumerate(hmids):
            ys_ref[c * rows:(c + 1) * rows, :] = jnp.dot(hmid, w2_ref[0], preferred_element_type=F32)

    @pl.when(i >= n_used_ref[0])
    def _():
        ys_ref[...] = jnp.zeros(ys_ref.shape, F32)


def _experts(blk_e, n_used, xs, w1_bf, w3_bf, w2_bf):
    cap = xs.shape[0]
    d, hid = w1_bf.shape[1], w1_bf.shape[2]
    nb = cap // MOE_BLOCK

    def row_map(i, blk_e_ref, n_used_ref):
        return (jnp.minimum(i, n_used_ref[0] - 1), 0)

    def w_map(i, blk_e_ref, n_used_ref):
        return (blk_e_ref[jnp.minimum(i, n_used_ref[0] - 1)], 0, 0)

    grid_spec = pltpu.PrefetchScalarGridSpec(
        num_scalar_prefetch=2,
        grid=(nb,),
        in_specs=[
            pl.BlockSpec((MOE_BLOCK, d // 2), row_map),
            pl.BlockSpec((1, d, hid), w_map),
            pl.BlockSpec((1, d, hid), w_map),
            pl.BlockSpec((1, hid, d), w_map),
        ],
        out_specs=pl.BlockSpec((MOE_BLOCK, d), lambda i, blk_e_ref, n_used_ref: (i, 0)),
    )
    return pl.pallas_call(
        _expert_kernel,
        grid_spec=grid_spec,
        out_shape=jax.ShapeDtypeStruct((cap, d), F32),
        compiler_params=_cparams(("arbitrary",)),
        name="moe_experts",
    )(blk_e, n_used, xs, w1_bf, w3_bf, w2_bf)


def _combine_kernel(dest_ref, x1_ref, gates_ref, ys_ref, o_ref, y0_scr, y1_scr, sem):
    tt = x1_ref.shape[0]

    def copy(r, k):
        dst = y0_scr if k == 0 else y1_scr
        return pltpu.make_async_copy(ys_ref.at[pl.ds(dest_ref[0, 0, 2 * r + k], 1)], dst.at[pl.ds(r, 1)], sem)

    for r in range(tt):
        copy(r, 0).start(priority=0)
        copy(r, 1).start(priority=1)

    def drain(r, c):
        copy(r, 0).wait()
        copy(r, 1).wait()
        return c

    lax.fori_loop(0, tt, drain, 0, unroll=True)
    g = gates_ref[...]
    o_ref[...] = x1_ref[...] + y0_scr[...] * g[:, 0:1] + y1_scr[...] * g[:, 1:2]


def _combine(dest2, x1, gates, ys):
    n, d = x1.shape
    tt = TOK_TILE
    return pl.pallas_call(
        _combine_kernel,
        grid=(n // tt,),
        in_specs=[
            pl.BlockSpec((1, 1, 2 * tt), lambda i: (i, 0, 0), memory_space=pltpu.SMEM),
            pl.BlockSpec((tt, d), lambda i: (i, 0)),
            pl.BlockSpec((tt, LANES), lambda i: (i, 0)),
            pl.BlockSpec(memory_space=pl.ANY),
        ],
        out_specs=pl.BlockSpec((tt, d), lambda i: (i, 0)),
        out_shape=jax.ShapeDtypeStruct((n, d), F32),
        scratch_shapes=[
            pltpu.VMEM((tt, d), F32),
            pltpu.VMEM((tt, d), F32),
            pltpu.SemaphoreType.DMA(()),
        ],
        compiler_params=_cparams(("arbitrary",)),
        name="moe_combine",
    )(dest2, x1, gates, ys)


def _rel_bucket_np(rel):
    nb = REL_BUCKETS // 2
    max_exact = nb // 2
    base = np.where(rel > 0, nb, 0)
    n = np.abs(rel)
    nf = np.maximum(n, 1).astype(np.float64)
    large = max_exact + (np.log(nf / max_exact) / math.log(REL_MAX_DIST / max_exact) * (nb - max_exact)).astype(np.int64)
    large = np.minimum(large, nb - 1)
    return (base + np.where(n < max_exact, n, large)).astype(np.int32)


def _bias_bucket_rows(t):
    far = _rel_bucket_np(-np.arange(t + 1, 4 * t))
    far_bucket = int(far[0])
    assert (far == far_bucket).all()
    m = np.arange(2 * t)
    wrap = np.where(m < t, m, m - 2 * t)
    return np.stack([_rel_bucket_np(wrap), _rel_bucket_np(wrap - t)]), far_bucket


def _toeplitz(rows, t):
    lead = rows.shape[:-1]
    flat = jnp.tile(rows, (1,) * len(lead) + (t,))[..., : t * (2 * t - 1)]
    return flat.reshape(lead + (t, 2 * t - 1))[..., :t]


def kernel(x, g_attn, w_in, qn_g, kn_g, lam_q1, lam_k1, lam_q2, lam_k2, subln_g, sb_out_g, rel_bias, w_o,
           g_ffn, w_router_g, b_router_g, w_router_e, b_router_e, w1, w3, w2):
    bsz, s_len, d = x.shape
    n = bsz * s_len
    depth = g_attn.shape[0]
    da_width = DA_HEADS * 2 * DA_HEAD_DIM
    sb_width = SB_HEADS * SB_HEAD_DIM

    bd = jnp.asarray(np.kron(np.eye(256 // DA_HEAD_DIM), np.ones((DA_HEAD_DIM, DA_HEAD_DIM))), BF16)
    tri = jnp.asarray(np.tril(np.ones((SB_WIN, SB_WIN)), -1), BF16)
    bucket_rows, far_bucket = _bias_bucket_rows(DA_BLOCK)
    upper = jnp.asarray(np.triu(np.ones((PROJ_ROWS, PROJ_ROWS)), 1), BF16)

    for l in range(depth):
        lambda_init = 0.8 - 0.6 * math.exp(-0.3 * l)
        x2 = x.reshape(n, d)

        q_scale = DA_HEAD_DIM ** -0.5 * LOG2E
        gain_row = jnp.concatenate([
            jnp.tile(qn_g[l] * q_scale, 2 * DA_HEADS),
            jnp.tile(kn_g[l], 2 * DA_HEADS),
            jnp.ones((da_width,), F32),
            jnp.full((sb_width,), SB_HEAD_DIM ** -0.5, F32),
            jnp.ones((2 * sb_width,), F32),
        ])[None, :]
        proj = _in_proj(x2, g_attn[l][None, :], w_in[l].astype(BF16), gain_row, bd)
        proj3 = proj.reshape(bsz, s_len, -1)

        lam = (jnp.exp(jnp.sum(lam_q1[l] * lam_k1[l])) - jnp.exp(jnp.sum(lam_q2[l] * lam_k2[l]))
               + lambda_init).astype(F32).reshape(1)
        rb = (rel_bias - rel_bias[far_bucket][None, :]) * LOG2E
        bias_rows = jnp.transpose(rb[bucket_rows], (2, 0, 1)).astype(F32)
        bias_tiles = _toeplitz(bias_rows, DA_BLOCK)
        q_reach = jnp.full((DA_HEADS,), DA_HEAD_DIM ** 0.5 * 1.02) * jnp.max(jnp.abs(qn_g[l] * q_scale))
        bias_range = jnp.stack([jnp.maximum(jnp.max(rb, axis=0), 0.0), jnp.minimum(jnp.min(rb, axis=0), 0.0),
                                q_reach])
        o_da = _diff_attention(proj3, lam, bias_range.astype(F32), bias_tiles, subln_g[l][None, :],
                               1.0 - lambda_init)
        o_sb = _stick_breaking(proj3, jnp.tile(sb_out_g[l], 2)[None, :], tri)

        wr = jnp.concatenate([w_router_g[l], w_router_e[l],
                              jnp.zeros((d, ROUTE_ROWS - N_GROUPS - N_EXPERTS), F32)], axis=1).T
        wr_hi = wr.astype(BF16)
        wr_mid = (wr - wr_hi.astype(F32)).astype(BF16)
        br = jnp.concatenate([b_router_g[l], b_router_e[l],
                              jnp.zeros((ROUTE_ROWS - N_GROUPS - N_EXPERTS,), F32)])
        br = jnp.broadcast_to(br[:, None], (ROUTE_ROWS, PROJ_ROWS))
        x1, h2, route_i, route_f, counts_all = _out_proj(
            x2, o_da.reshape(n, -1), o_sb.reshape(n, -1), w_o[l].astype(BF16), g_ffn[l][None, :],
            jnp.stack([wr_hi, wr_mid]), br, upper)

        counts = counts_all[N_GROUPS:N_GROUPS + N_EXPERTS, 0]
        padded = (counts + MOE_BLOCK - 1) // MOE_BLOCK * MOE_BLOCK
        pad_ends = jnp.cumsum(padded)
        pad_starts = pad_ends - padded
        experts = jnp.arange(N_EXPERTS, dtype=jnp.int32)
        eid, rank = route_i[0:2], route_i[2:4]
        start_of = jnp.sum(jnp.where(eid[:, :, None] == experts, pad_starts, 0), axis=-1)
        dest2 = (start_of + rank).T.reshape(n // TOK_TILE, 1, 2 * TOK_TILE)
        gates = jnp.pad(route_f[0:2].T, ((0, 0), (0, LANES - 2)))
        cap = 2 * n + N_EXPERTS * MOE_BLOCK
        nb = cap // MOE_BLOCK
        blk_start = jnp.arange(nb, dtype=jnp.int32) * MOE_BLOCK
        blk_e = jnp.minimum(jnp.sum(pad_ends[None, :] <= blk_start[:, None], axis=1), N_EXPERTS - 1).astype(jnp.int32)
        n_used = (pad_ends[-1] // MOE_BLOCK).astype(jnp.int32).reshape(1)

        xs = _dispatch(dest2, h2, jnp.zeros((cap, d // 2), jnp.uint32))
        ys = _experts(blk_e, n_used, xs, w1[l].astype(BF16), w3[l].astype(BF16), w2[l].astype(BF16))
        x = _combine(dest2, x1, gates, ys).reshape(bsz, s_len, d)
    return x
```

```python
import functools
import math

import jax
import jax.numpy as jnp
import numpy as np
from jax import lax
from jax.experimental import pallas as pl
from jax.experimental.pallas import tpu as pltpu

F32 = jnp.float32
BF16 = jnp.bfloat16

EPS = 1e-6
LANES = 128
DA_HEADS = 4
DA_HEAD_DIM = 64
SB_HEADS = 8
SB_HEAD_DIM = 64
CHUNK = 64
REL_BUCKETS = 32
REL_MAX_DIST = 128
N_GROUPS = 4
EXPERTS_PER_GROUP = 8
N_EXPERTS = N_GROUPS * EXPERTS_PER_GROUP
NEG_BIG = -1e30
LOG2E = math.log2(math.e)
SB_DEAD = -87.5

VMEM_LIMIT = 48 * 1024 * 1024

PROJ_ROWS = 512
ROUTE_ROWS = 48
DA_BLOCK = 256
DA_UNROLL = 4
DA_SAFE_RANGE = 100.0
SB_SUB = 128
SB_ROWS = 1024
SB_WIN = 256
MOE_BLOCK = 512
MOE_CHUNKS = 2
TOK_TILE = 512


def _cparams(sem):
    return pltpu.CompilerParams(dimension_semantics=sem, vmem_limit_bytes=VMEM_LIMIT)


def _in_proj_kernel(x_ref, g_ref, w_ref, gain_ref, bd_ref, o_ref):
    x = x_ref[...]
    ms = jnp.mean(x * x, axis=-1, keepdims=True)
    h = (x * lax.rsqrt(ms + EPS) * g_ref[...]).astype(BF16)
    n_chunks = o_ref.shape[1] // 512
    for c in range(n_chunks):
        cols = slice(c * 512, (c + 1) * 512)
        acc = jnp.dot(h, w_ref[:, cols], preferred_element_type=F32)
        if c < 2:
            sq = (acc * acc).astype(BF16)
            parts = []
            for s in range(2):
                ss = jnp.dot(sq[:, s * 256:(s + 1) * 256], bd_ref[...], preferred_element_type=F32)
                parts.append(ss)
            ss = jnp.concatenate(parts, axis=1)
            acc = acc * lax.rsqrt(ss * (1.0 / DA_HEAD_DIM) + EPS)
        o_ref[:, cols] = (acc * gain_ref[:, cols]).astype(BF16)


def _in_proj(x2, g_attn, w_in_bf, gain_row, bd):
    n, d = x2.shape
    width = w_in_bf.shape[1]
    return pl.pallas_call(
        _in_proj_kernel,
        grid=(n // PROJ_ROWS,),
        in_specs=[
            pl.BlockSpec((PROJ_ROWS, d), lambda i: (i, 0)),
            pl.BlockSpec((1, d), lambda i: (0, 0)),
            pl.BlockSpec((d, width), lambda i: (0, 0)),
            pl.BlockSpec((1, width), lambda i: (0, 0)),
            pl.BlockSpec((256, 256), lambda i: (0, 0)),
        ],
        out_specs=pl.BlockSpec((PROJ_ROWS, width), lambda i: (i, 0)),
        out_shape=jax.ShapeDtypeStruct((n, width), BF16),
        compiler_params=_cparams(("parallel",)),
        name="in_proj",
    )(x2, g_attn, w_in_bf, gain_row, bd)


def _da_kernel(lam_ref, brange_ref, q_ref, k_ref, v_ref, bias_ref, g_ref, o_ref, m_scr, acc_scr, stab_scr,
               fits_scr, pa_scr, pb_scr, *, out_scale):
    t = DA_BLOCK
    h = pl.program_id(1)
    i = pl.program_id(2)
    s_len = k_ref.shape[1]
    ones_col = jnp.ones((LANES, LANES), BF16)

    @pl.when(i == 0)
    def _():
        lane_k = lax.broadcasted_iota(jnp.int32, (1, LANES), 1)
        rows = 512

        def kbody(c, best):
            kc = k_ref[0, pl.ds(pl.multiple_of(c * rows, rows), rows), :].astype(F32)
            sq = kc * kc
            n1 = jnp.dot(jnp.where(lane_k < DA_HEAD_DIM, sq, 0.0).astype(BF16), ones_col, preferred_element_type=F32)
            n2 = jnp.dot(jnp.where(lane_k >= DA_HEAD_DIM, sq, 0.0).astype(BF16), ones_col, preferred_element_type=F32)
            return (jnp.maximum(best[0], jnp.max(n1, axis=0, keepdims=True)),
                    jnp.maximum(best[1], jnp.max(n2, axis=0, keepdims=True)))

        zero_row = jnp.zeros((1, LANES), F32)
        k1, k2 = lax.fori_loop(0, s_len // rows, kbody, (zero_row, zero_row))
        reach1 = jnp.max(jnp.sqrt(k1)) * (brange_ref[2, h] * 1.02)
        reach2 = jnp.max(jnp.sqrt(k2)) * (brange_ref[2, h] * 1.02)
        stab_scr[0] = reach1 + brange_ref[0, h]
        stab_scr[1] = reach2 + brange_ref[0, h]
        spread = 2.0 * jnp.maximum(reach1, reach2) + (brange_ref[0, h] - brange_ref[1, h])
        fits_scr[0] = (spread <= DA_SAFE_RANGE).astype(jnp.int32)

    q = q_ref[0]
    lane = lax.broadcasted_iota(jnp.int32, (1, LANES), 1)
    zero = jnp.zeros_like(q)
    qq = jnp.concatenate([jnp.where(lane < DA_HEAD_DIM, q, zero),
                          jnp.where(lane >= DA_HEAD_DIM, q, zero)], axis=0)
    ones = jnp.ones((t, LANES), BF16)
    row = lax.broadcasted_iota(jnp.int32, (t, t), 0)
    col = lax.broadcasted_iota(jnp.int32, (t, t), 1)
    chunk_mask = (col // CHUNK) <= (row // CHUNK)
    n_far = jnp.maximum(i - 1, 0)

    def values(j):
        return jnp.concatenate([v_ref[0, pl.ds(pl.multiple_of(j * t, t), t), :], ones], axis=1)

    def scores(j, bias, mask):
        kb = k_ref[0, pl.ds(pl.multiple_of(j * t, t), t), :]
        s = lax.dot_general(qq, kb, (((1,), (1,)), ((), ())), preferred_element_type=F32)
        if bias is not None:
            s = s + jnp.concatenate([bias, bias], axis=0)
        if mask is not None:
            s = jnp.where(jnp.concatenate([mask, mask], axis=0) if mask.ndim else mask, s, NEG_BIG)
        return s, values(j)

    fits = fits_scr[0] != 0

    @pl.when(fits)
    def _():
        stab = jnp.concatenate([jnp.full((t, t), stab_scr[0], F32), jnp.full((t, t), stab_scr[1], F32)], axis=0)

        def probs_into(g, p_scr):
            for u in range(DA_UNROLL):
                s, _ = scores(g * DA_UNROLL + u, None, None)
                p_scr[u] = jnp.exp2(s - stab).astype(BF16)

        def weighted_from(g, p_scr):
            tot = jnp.dot(p_scr[0], values(g * DA_UNROLL), preferred_element_type=F32)
            for u in range(1, DA_UNROLL):
                tot = tot + jnp.dot(p_scr[u], values(g * DA_UNROLL + u), preferred_element_type=F32)
            acc_scr[...] += tot

        n_grp = n_far // DA_UNROLL

        left = n_far - n_grp * DA_UNROLL

        def head_region(n_left):
            tail = [(n_grp * DA_UNROLL + u, None, None) for u in range(n_left)]
            tail += [(n_far, bias_ref[0, 1], i > 0), (i, bias_ref[0, 0], chunk_mask)]
            tail_scores = [scores(j, bias, mask) for j, bias, mask in tail]
            probs_into(0, pa_scr)
            tot = None
            for s, vb in tail_scores:
                pv = jnp.dot(jnp.exp2(s - stab).astype(BF16), vb, preferred_element_type=F32)
                tot = pv if tot is None else tot + pv
            acc_scr[...] = tot

        for n_left in range(DA_UNROLL):
            pl.when(left == n_left)(functools.partial(head_region, n_left))

        def pair_body(hp, carry):
            g = 2 * hp
            weighted_from(g, pa_scr)
            probs_into(g + 1, pb_scr)
            weighted_from(g + 1, pb_scr)
            probs_into(g + 2, pa_scr)
            return carry

        lax.fori_loop(0, (n_grp - 1) // 2, pair_body, 0)

        @pl.when(n_grp % 2 == 1)
        def _():
            weighted_from(n_grp - 1, pa_scr)

        @pl.when((n_grp % 2 == 0) & (n_grp > 0))
        def _():
            weighted_from(n_grp - 2, pa_scr)
            probs_into(n_grp - 1, pb_scr)
            weighted_from(n_grp - 1, pb_scr)

    @pl.when(jnp.logical_not(fits))
    def _():
        m_scr[...] = jnp.full(m_scr.shape, NEG_BIG, F32)
        acc_scr[...] = jnp.zeros(acc_scr.shape, F32)

        def step(j, bias=None, mask=None):
            s, vb = scores(j, bias, mask)
            m_prev = m_scr[...]
            m_next = jnp.maximum(m_prev, jnp.max(s, axis=1, keepdims=True))
            alpha = jnp.exp2(m_prev - m_next)
            p = jnp.exp2(s - jnp.concatenate([m_next] * (t // LANES), axis=1))
            pv = jnp.dot(p.astype(BF16), vb, preferred_element_type=F32)
            acc_scr[...] = jnp.concatenate([alpha, alpha], axis=1) * acc_scr[...] + pv
            m_scr[...] = m_next

        def far_body(j, carry):
            step(j)
            return carry

        lax.fori_loop(0, n_far, far_body, 0)

        @pl.when(i > 0)
        def _():
            step(i - 1, bias=bias_ref[0, 1])

        step(i, bias=bias_ref[0, 0], mask=chunk_mask)

    acc = acc_scr[...]
    o_all = acc[:, 0:LANES] / acc[:, LANES:2 * LANES]
    o = o_all[0:t] - lam_ref[0] * o_all[t:2 * t]
    ms = jnp.mean(o * o, axis=-1, keepdims=True)
    o_ref[0] = (o * lax.rsqrt(ms + EPS) * (g_ref[...] * out_scale)).astype(BF16)


def _diff_attention(proj3, lam, bias_range, bias_tiles, subln_row, out_scale):
    bsz, s_len, _ = proj3.shape
    t = DA_BLOCK
    nq = s_len // t
    kern = functools.partial(_da_kernel, out_scale=out_scale)
    return pl.pallas_call(
        kern,
        grid=(bsz, DA_HEADS, nq),
        in_specs=[
            pl.BlockSpec(memory_space=pltpu.SMEM),
            pl.BlockSpec(memory_space=pltpu.SMEM),
            pl.BlockSpec((1, t, LANES), lambda b, h, i: (b, i, h)),
            pl.BlockSpec((1, s_len, LANES), lambda b, h, i: (b, 0, DA_HEADS + h)),
            pl.BlockSpec((1, s_len, LANES), lambda b, h, i: (b, 0, 2 * DA_HEADS + h)),
            pl.BlockSpec((1, 2, t, t), lambda b, h, i: (h, 0, 0, 0)),
            pl.BlockSpec((1, LANES), lambda b, h, i: (0, 0)),
        ],
        out_specs=pl.BlockSpec((1, t, LANES), lambda b, h, i: (b, i, h)),
        out_shape=jax.ShapeDtypeStruct((bsz, s_len, DA_HEADS * LANES), BF16),
        scratch_shapes=[
            pltpu.VMEM((2 * t, LANES), F32),
            pltpu.VMEM((2 * t, 2 * LANES), F32),
            pltpu.SMEM((2,), F32),
            pltpu.SMEM((1,), jnp.int32),
            pltpu.VMEM((DA_UNROLL, 2 * t, t), BF16),
            pltpu.VMEM((DA_UNROLL, 2 * t, t), BF16),
        ],
        compiler_params=_cparams(("parallel", "parallel", "arbitrary")),
        name="diff_attention",
    )(lam, bias_range, proj3, proj3, proj3, bias_tiles, subln_row)


def _sb_tiles(tiles, tri):
    zs = [lax.dot_general(q, k, (((1,), (1,)), ((), ())), preferred_element_type=F32) for q, k, _, _, _ in tiles]
    mids = []
    for z, (_, _, _, causal, _) in zip(zs, tiles):
        log_1m = -(jnp.maximum(z, 0.0) + jnp.log(1.0 + jnp.exp(-jnp.abs(z))))
        lm = log_1m if causal is None else jnp.where(causal, log_1m, 0.0)
        hi = lm.astype(BF16)
        mid = (lm - hi.astype(F32)).astype(BF16)
        inner = jnp.dot(hi, tri, preferred_element_type=F32) + jnp.dot(mid, tri, preferred_element_type=F32)
        mids.append((z + log_1m + inner, jnp.sum(lm, axis=1, keepdims=True)))
    outs = []
    for (logit, total), (_, _, v, causal, carry) in zip(mids, tiles):
        if carry is not None:
            logit = logit + carry
        a = jnp.exp(logit)
        if causal is not None:
            a = jnp.where(causal, a, 0.0)
        outs.append((jnp.dot(a.astype(BF16), v, preferred_element_type=F32), total))
    return outs


def _sb_kernel(q_ref, k_ref, v_ref, g_ref, tri_ref, o_ref, acc_scr, carry_scr):
    sub, win = SB_SUB, SB_WIN
    i = pl.program_id(2)
    lane = lax.broadcasted_iota(jnp.int32, (1, LANES), 1)
    tri = tri_ref[...]
    row = lax.broadcasted_iota(jnp.int32, (sub, win), 0)
    col = lax.broadcasted_iota(jnp.int32, (sub, win), 1)
    n_sub = SB_ROWS // sub

    tiles, starts = [], []
    for u in range(n_sub):
        r0 = i * SB_ROWS + u * sub
        start = pl.multiple_of(jnp.maximum(r0 - sub, 0), sub)
        qu = q_ref[0, u * sub:(u + 1) * sub, :]
        zero = jnp.zeros_like(qu)
        qs = jnp.concatenate([jnp.where(lane < SB_HEAD_DIM, qu, zero),
                              jnp.where(lane >= SB_HEAD_DIM, qu, zero)], axis=0)
        causal = (start + col) < (r0 + row)
        tiles.append((qs, k_ref[0, pl.ds(start, win), :], v_ref[0, pl.ds(start, win), :],
                      jnp.concatenate([causal, causal], axis=0), None))
        starts.append(start)
    chains = []
    for u, (pv, total) in enumerate(_sb_tiles(tiles, tri)):
        acc_scr[u] = pv
        carry_scr[u] = jnp.broadcast_to(total, (2 * sub, LANES))
        chains.append((u, tiles[u][0], starts[u] // sub - 1, jnp.max(total)))

    tri_sub = tri[0:sub, 0:sub]
    for u, qs, j0, alive0 in chains:
        def cond(state):
            j, alive = state
            return (j >= 0) & (alive > SB_DEAD)

        def body(state, u=u, qs=qs):
            j, _ = state
            s0 = pl.multiple_of(j * sub, sub)
            kb = k_ref[0, pl.ds(s0, sub), :]
            vb = v_ref[0, pl.ds(s0, sub), :]
            carry = carry_scr[u]
            (pv, total), = _sb_tiles([(qs, kb, vb, None, carry)], tri_sub)
            acc_scr[u] += pv
            carry = carry + total
            carry_scr[u] = carry
            return j - 1, jnp.max(carry)

        lax.while_loop(cond, body, (j0, alive0))

    for u in range(n_sub):
        acc = acc_scr[u]
        o = jnp.where(lane < SB_HEAD_DIM, acc[0:sub], acc[sub:2 * sub])
        sq = o * o
        ss0 = jnp.sum(jnp.where(lane < SB_HEAD_DIM, sq, 0.0), axis=1, keepdims=True)
        ss1 = jnp.sum(jnp.where(lane >= SB_HEAD_DIM, sq, 0.0), axis=1, keepdims=True)
        ms = jnp.where(lane < SB_HEAD_DIM, ss0, ss1) * (1.0 / SB_HEAD_DIM)
        o_ref[0, u * sub:(u + 1) * sub, :] = (o * lax.rsqrt(ms + EPS) * g_ref[...]).astype(BF16)


def _stick_breaking(proj3, sb_row, tri):
    bsz, s_len, _ = proj3.shape
    t = SB_ROWS
    nq = s_len // t
    pairs = SB_HEADS // 2
    q_blk = 3 * DA_HEADS
    n_chains = SB_ROWS // SB_SUB
    return pl.pallas_call(
        _sb_kernel,
        grid=(bsz, pairs, nq),
        in_specs=[
            pl.BlockSpec((1, t, LANES), lambda b, h, i: (b, i, q_blk + h)),
            pl.BlockSpec((1, s_len, LANES), lambda b, h, i: (b, 0, q_blk + pairs + h)),
            pl.BlockSpec((1, s_len, LANES), lambda b, h, i: (b, 0, q_blk + 2 * pairs + h)),
            pl.BlockSpec((1, LANES), lambda b, h, i: (0, 0)),
            pl.BlockSpec((SB_WIN, SB_WIN), lambda b, h, i: (0, 0)),
        ],
        out_specs=pl.BlockSpec((1, t, LANES), lambda b, h, i: (b, i, h)),
        out_shape=jax.ShapeDtypeStruct((bsz, s_len, pairs * LANES), BF16),
        scratch_shapes=[
            pltpu.VMEM((n_chains, 2 * SB_SUB, LANES), F32),
            pltpu.VMEM((n_chains, 2 * SB_SUB, LANES), F32),
        ],
        compiler_params=_cparams(("parallel", "parallel", "arbitrary")),
        name="stick_breaking",
    )(proj3, proj3, proj3, sb_row, tri)


def _out_proj_kernel(x_ref, oda_ref, osb_ref, wo_ref, g_ref, wr_ref, br_ref, upper_ref,
                     x1_ref, h2_ref, route_i_ref, route_f_ref, counts_ref, base_scr):
    half = oda_ref.shape[1]
    x1 = (x_ref[...]
          + jnp.dot(oda_ref[...], wo_ref[0:half, :], preferred_element_type=F32)
          + jnp.dot(osb_ref[...], wo_ref[half:2 * half, :], preferred_element_type=F32))
    x1_ref[...] = x1
    ms = jnp.mean(x1 * x1, axis=-1, keepdims=True)
    h2 = x1 * lax.rsqrt(ms + EPS) * g_ref[...]
    d_half = h2.shape[1] // 2
    hi_bits = lax.bitcast_convert_type(h2[:, :d_half].astype(BF16).astype(F32), jnp.uint32)
    lo_bits = lax.bitcast_convert_type(h2[:, d_half:].astype(BF16).astype(F32), jnp.uint32)
    h2_ref[...] = hi_bits | (lo_bits >> 16)

    a_hi = h2.astype(BF16)
    a_mid = (h2 - a_hi.astype(F32)).astype(BF16)
    nt = lambda w, a: lax.dot_general(w, a, (((1,), (1,)), ((), ())), preferred_element_type=F32)
    lt = nt(wr_ref[0], a_hi) + (nt(wr_ref[0], a_mid) + nt(wr_ref[1], a_hi)) + br_ref[...]
    n_rows, tm = lt.shape
    ridx = lax.broadcasted_iota(jnp.int32, (n_rows, tm), 0)
    big = jnp.int32(n_rows)
    is_g = ridx < N_GROUPS
    lg = jnp.where(is_g, lt, NEG_BIG)
    mg = jnp.max(lg, axis=0, keepdims=True)
    gsel = jnp.min(jnp.where(lg == mg, ridx, big), axis=0, keepdims=True)
    pg_sel = 1.0 / jnp.sum(jnp.where(is_g, jnp.exp(lg - mg), 0.0), axis=0, keepdims=True)
    lo_row = N_GROUPS + gsel * EXPERTS_PER_GROUP
    le = jnp.where(ridx >= lo_row, jnp.where(ridx < lo_row + EXPERTS_PER_GROUP, lt, NEG_BIG), NEG_BIG)
    l1 = jnp.max(le, axis=0, keepdims=True)
    i1 = jnp.min(jnp.where(le == l1, ridx, big), axis=0, keepdims=True)
    le2 = jnp.where(ridx == i1, NEG_BIG, le)
    l2 = jnp.max(le2, axis=0, keepdims=True)
    i2 = jnp.min(jnp.where(le2 == l2, ridx, big), axis=0, keepdims=True)
    e2 = jnp.exp(l2 - l1)
    g1 = pg_sel / (1.0 + e2)
    g2 = pg_sel * e2 / (1.0 + e2)

    @pl.when(pl.program_id(0) == 0)
    def _():
        base_scr[...] = jnp.zeros(base_scr.shape, F32)

    sel1 = ridx == i1
    sel2 = ridx == i2
    chosen = jnp.where(sel1, 1.0, jnp.where(sel2, 1.0, 0.0))
    before = jnp.dot(chosen.astype(BF16), upper_ref[...], preferred_element_type=F32)
    base = base_scr[...]
    before = before + jnp.concatenate([base] * (tm // LANES), axis=1)
    rank1 = jnp.sum(jnp.where(sel1, before, 0.0), axis=0, keepdims=True)
    rank2 = jnp.sum(jnp.where(sel2, before, 0.0), axis=0, keepdims=True)
    base = base + jnp.sum(chosen, axis=1, keepdims=True)
    base_scr[...] = base
    counts_ref[...] = base.astype(jnp.int32)

    r8 = lax.broadcasted_iota(jnp.int32, (8, tm), 0)
    route_i_ref[...] = jnp.where(r8 == 0, i1 - N_GROUPS, jnp.where(r8 == 1, i2 - N_GROUPS, jnp.where(
        r8 == 2, rank1.astype(jnp.int32), jnp.where(r8 == 3, rank2.astype(jnp.int32), 0))))
    route_f_ref[...] = jnp.where(r8 == 0, g1, jnp.where(r8 == 1, g2, 0.0))


def _out_proj(x2, o_da, o_sb, wo_bf, g_ffn, wr2, br, upper):
    n, d = x2.shape
    half = o_da.shape[1]
    tm = PROJ_ROWS
    r = ROUTE_ROWS
    return pl.pallas_call(
        _out_proj_kernel,
        grid=(n // tm,),
        in_specs=[
            pl.BlockSpec((tm, d), lambda i: (i, 0)),
            pl.BlockSpec((tm, half), lambda i: (i, 0)),
            pl.BlockSpec((tm, half), lambda i: (i, 0)),
            pl.BlockSpec((2 * half, d), lambda i: (0, 0)),
            pl.BlockSpec((1, d), lambda i: (0, 0)),
            pl.BlockSpec((2, r, d), lambda i: (0, 0, 0)),
            pl.BlockSpec((r, tm), lambda i: (0, 0)),
            pl.BlockSpec((tm, tm), lambda i: (0, 0)),
        ],
        out_specs=[
            pl.BlockSpec((tm, d), lambda i: (i, 0)),
            pl.BlockSpec((tm, d // 2), lambda i: (i, 0)),
            pl.BlockSpec((8, tm), lambda i: (0, i)),
            pl.BlockSpec((8, tm), lambda i: (0, i)),
            pl.BlockSpec((r, LANES), lambda i: (0, 0)),
        ],
        out_shape=[
            jax.ShapeDtypeStruct((n, d), F32),
            jax.ShapeDtypeStruct((n, d // 2), jnp.uint32),
            jax.ShapeDtypeStruct((8, n), jnp.int32),
            jax.ShapeDtypeStruct((8, n), F32),
            jax.ShapeDtypeStruct((r, LANES), jnp.int32),
        ],
        scratch_shapes=[pltpu.VMEM((r, LANES), F32)],
        compiler_params=_cparams(("arbitrary",)),
        name="out_proj_router",
    )(x2, o_da, o_sb, wo_bf, g_ffn, wr2, br, upper)


def _dispatch_kernel(dest_ref, h_ref, xs_in_ref, xs_ref, sem):
    del xs_in_ref
    tt = h_ref.shape[0]

    def copy(r, k):
        return pltpu.make_async_copy(h_ref.at[pl.ds(r, 1)], xs_ref.at[pl.ds(dest_ref[0, 0, 2 * r + k], 1)], sem)

    for r in range(tt):
        copy(r, 0).start(priority=0)
        copy(r, 1).start(priority=1)

    def drain(r, c):
        copy(r, 0).wait()
        copy(r, 1).wait()
        return c

    lax.fori_loop(0, tt, drain, 0, unroll=True)


def _dispatch(dest2, h2, xs_init):
    n, d = h2.shape
    tt = TOK_TILE
    return pl.pallas_call(
        _dispatch_kernel,
        grid=(n // tt,),
        in_specs=[
            pl.BlockSpec((1, 1, 2 * tt), lambda i: (i, 0, 0), memory_space=pltpu.SMEM),
            pl.BlockSpec((tt, d), lambda i: (i, 0)),
            pl.BlockSpec(memory_space=pl.ANY),
        ],
        out_specs=pl.BlockSpec(memory_space=pl.ANY),
        out_shape=jax.ShapeDtypeStruct(xs_init.shape, xs_init.dtype),
        scratch_shapes=[pltpu.SemaphoreType.DMA(())],
        input_output_aliases={2: 0},
        compiler_params=_cparams(("arbitrary",)),
        name="moe_dispatch",
    )(dest2, h2, xs_init)


def _expert_kernel(blk_e_ref, n_used_ref, xs_ref, w1_ref, w3_ref, w2_ref, ys_ref):
    i = pl.program_id(0)

    @pl.when(i < n_used_ref[0])
    def _():
        rows = MOE_BLOCK // MOE_CHUNKS
        xbs = []
        for c in range(MOE_CHUNKS):
            words = xs_ref[c * rows:(c + 1) * rows, :]
            xbs.append(jnp.concatenate([
                lax.bitcast_convert_type(words & jnp.uint32(0xFFFF0000), F32).astype(BF16),
                lax.bitcast_convert_type(words << 16, F32).astype(BF16)], axis=1))
        gated = [(jnp.dot(xb, w1_ref[0], preferred_element_type=F32), jnp.dot(xb, w3_ref[0], preferred_element_type=F32))
                 for xb in xbs]
        hmids = [(a * jax.nn.sigmoid(a) * b).astype(BF16) for a, b in gated]
        for c, hmid in enumerate(hmids):
            ys_ref[c * rows:(c + 1) * rows, :] = jnp.dot(hmid, w2_ref[0], preferred_element_type=F32)

    @pl.when(i >= n_used_ref[0])
    def _():
        ys_ref[...] = jnp.zeros(ys_ref.shape, F32)


def _experts(blk_e, n_used, xs, w1_bf, w3_bf, w2_bf):
    cap = xs.shape[0]
    d, hid = w1_bf.shape[1], w1_bf.shape[2]
    nb = cap // MOE_BLOCK

    def row_map(i, blk_e_ref, n_used_ref):
        return (jnp.minimum(i, n_used_ref[0] - 1), 0)

    def w_map(i, blk_e_ref, n_used_ref):
        return (blk_e_ref[jnp.minimum(i, n_used_ref[0] - 1)], 0, 0)

    grid_spec = pltpu.PrefetchScalarGridSpec(
        num_scalar_prefetch=2,
        grid=(nb,),
        in_specs=[
            pl.BlockSpec((MOE_BLOCK, d // 2), row_map),
            pl.BlockSpec((1, d, hid), w_map),
            pl.BlockSpec((1, d, hid), w_map),
            pl.BlockSpec((1, hid, d), w_map),
        ],
        out_specs=pl.BlockSpec((MOE_BLOCK, d), lambda i, blk_e_ref, n_used_ref: (i, 0)),
    )
    return pl.pallas_call(
        _expert_kernel,
        grid_spec=grid_spec,
        out_shape=jax.ShapeDtypeStruct((cap, d), F32),
        compiler_params=_cparams(("arbitrary",)),
        name="moe_experts",
    )(blk_e, n_used, xs, w1_bf, w3_bf, w2_bf)


def _combine_kernel(dest_ref, x1_ref, gates_ref, ys_ref, o_ref, y0_scr, y1_scr, sem):
    tt = x1_ref.shape[0]

    def copy(r, k):
        dst = y0_scr if k == 0 else y1_scr
        return pltpu.make_async_copy(ys_ref.at[pl.ds(dest_ref[0, 0, 2 * r + k], 1)], dst.at[pl.ds(r, 1)], sem)

    for r in range(tt):
        copy(r, 0).start(priority=0)
        copy(r, 1).start(priority=1)

    def drain(r, c):
        copy(r, 0).wait()
        copy(r, 1).wait()
        return c

    lax.fori_loop(0, tt, drain, 0, unroll=True)
    g = gates_ref[...]
    o_ref[...] = x1_ref[...] + y0_scr[...] * g[:, 0:1] + y1_scr[...] * g[:, 1:2]


def _combine(dest2, x1, gates, ys):
    n, d = x1.shape
    tt = TOK_TILE
    return pl.pallas_call(
        _combine_kernel,
        grid=(n // tt,),
        in_specs=[
            pl.BlockSpec((1, 1, 2 * tt), lambda i: (i, 0, 0), memory_space=pltpu.SMEM),
            pl.BlockSpec((tt, d), lambda i: (i, 0)),
            pl.BlockSpec((tt, LANES), lambda i: (i, 0)),
            pl.BlockSpec(memory_space=pl.ANY),
        ],
        out_specs=pl.BlockSpec((tt, d), lambda i: (i, 0)),
        out_shape=jax.ShapeDtypeStruct((n, d), F32),
        scratch_shapes=[
            pltpu.VMEM((tt, d), F32),
            pltpu.VMEM((tt, d), F32),
            pltpu.SemaphoreType.DMA(()),
        ],
        compiler_params=_cparams(("arbitrary",)),
        name="moe_combine",
    )(dest2, x1, gates, ys)


def _rel_bucket_np(rel):
    nb = REL_BUCKETS // 2
    max_exact = nb // 2
    base = np.where(rel > 0, nb, 0)
    n = np.abs(rel)
    nf = np.maximum(n, 1).astype(np.float64)
    large = max_exact + (np.log(nf / max_exact) / math.log(REL_MAX_DIST / max_exact) * (nb - max_exact)).astype(np.int64)
    large = np.minimum(large, nb - 1)
    return (base + np.where(n < max_exact, n, large)).astype(np.int32)


def _bias_bucket_rows(t):
    far = _rel_bucket_np(-np.arange(t + 1, 4 * t))
    far_bucket = int(far[0])
    assert (far == far_bucket).all()
    m = np.arange(2 * t)
    wrap = np.where(m < t, m, m - 2 * t)
    return np.stack([_rel_bucket_np(wrap), _rel_bucket_np(wrap - t)]), far_bucket


def _toeplitz(rows, t):
    lead = rows.shape[:-1]
    flat = jnp.tile(rows, (1,) * len(lead) + (t,))[..., : t * (2 * t - 1)]
    return flat.reshape(lead + (t, 2 * t - 1))[..., :t]


def kernel(x, g_attn, w_in, qn_g, kn_g, lam_q1, lam_k1, lam_q2, lam_k2, subln_g, sb_out_g, rel_bias, w_o,
           g_ffn, w_router_g, b_router_g, w_router_e, b_router_e, w1, w3, w2):
    bsz, s_len, d = x.shape
    n = bsz * s_len
    depth = g_attn.shape[0]
    da_width = DA_HEADS * 2 * DA_HEAD_DIM
    sb_width = SB_HEADS * SB_HEAD_DIM

    bd = jnp.asarray(np.kron(np.eye(256 // DA_HEAD_DIM), np.ones((DA_HEAD_DIM, DA_HEAD_DIM))), BF16)
    tri = jnp.asarray(np.tril(np.ones((SB_WIN, SB_WIN)), -1), BF16)
    bucket_rows, far_bucket = _bias_bucket_rows(DA_BLOCK)
    upper = jnp.asarray(np.triu(np.ones((PROJ_ROWS, PROJ_ROWS)), 1), BF16)

    for l in range(depth):
        lambda_init = 0.8 - 0.6 * math.exp(-0.3 * l)
        x2 = x.reshape(n, d)

        q_scale = DA_HEAD_DIM ** -0.5 * LOG2E
        gain_row = jnp.concatenate([
            jnp.tile(qn_g[l] * q_scale, 2 * DA_HEADS),
            jnp.tile(kn_g[l], 2 * DA_HEADS),
            jnp.ones((da_width,), F32),
            jnp.full((sb_width,), SB_HEAD_DIM ** -0.5, F32),
            jnp.ones((2 * sb_width,), F32),
        ])[None, :]
        proj = _in_proj(x2, g_attn[l][None, :], w_in[l].astype(BF16), gain_row, bd)
        proj3 = proj.reshape(bsz, s_len, -1)

        lam = (jnp.exp(jnp.sum(lam_q1[l] * lam_k1[l])) - jnp.exp(jnp.sum(lam_q2[l] * lam_k2[l]))
               + lambda_init).astype(F32).reshape(1)
        rb = (rel_bias - rel_bias[far_bucket][None, :]) * LOG2E
        bias_rows = jnp.transpose(rb[bucket_rows], (2, 0, 1)).astype(F32)
        bias_tiles = _toeplitz(bias_rows, DA_BLOCK)
        q_reach = jnp.full((DA_HEADS,), DA_HEAD_DIM ** 0.5 * 1.02) * jnp.max(jnp.abs(qn_g[l] * q_scale))
        bias_range = jnp.stack([jnp.maximum(jnp.max(rb, axis=0), 0.0), jnp.minimum(jnp.min(rb, axis=0), 0.0),
                                q_reach])
        o_da = _diff_attention(proj3, lam, bias_range.astype(F32), bias_tiles, subln_g[l][None, :],
                               1.0 - lambda_init)
        o_sb = _stick_breaking(proj3, jnp.tile(sb_out_g[l], 2)[None, :], tri)

        wr = jnp.concatenate([w_router_g[l], w_router_e[l],
                              jnp.zeros((d, ROUTE_ROWS - N_GROUPS - N_EXPERTS), F32)], axis=1).T
        wr_hi = wr.astype(BF16)
        wr_mid = (wr - wr_hi.astype(F32)).astype(BF16)
        br = jnp.concatenate([b_router_g[l], b_router_e[l],
                              jnp.zeros((ROUTE_ROWS - N_GROUPS - N_EXPERTS,), F32)])
        br = jnp.broadcast_to(br[:, None], (ROUTE_ROWS, PROJ_ROWS))
        x1, h2, route_i, route_f, counts_all = _out_proj(
            x2, o_da.reshape(n, -1), o_sb.reshape(n, -1), w_o[l].astype(BF16), g_ffn[l][None, :],
            jnp.stack([wr_hi, wr_mid]), br, upper)

        counts = counts_all[N_GROUPS:N_GROUPS + N_EXPERTS, 0]
        padded = (counts + MOE_BLOCK - 1) // MOE_BLOCK * MOE_BLOCK
        pad_ends = jnp.cumsum(padded)
        pad_starts = pad_ends - padded
        experts = jnp.arange(N_EXPERTS, dtype=jnp.int32)
        eid, rank = route_i[0:2], route_i[2:4]
        start_of = jnp.sum(jnp.where(eid[:, :, None] == experts, pad_starts, 0), axis=-1)
        dest2 = (start_of + rank).T.reshape(n // TOK_TILE, 1, 2 * TOK_TILE)
        gates = jnp.pad(route_f[0:2].T, ((0, 0), (0, LANES - 2)))
        cap = 2 * n + N_EXPERTS * MOE_BLOCK
        nb = cap // MOE_BLOCK
        blk_start = jnp.arange(nb, dtype=jnp.int32) * MOE_BLOCK
        blk_e = jnp.minimum(jnp.sum(pad_ends[None, :] <= blk_start[:, None], axis=1), N_EXPERTS - 1).astype(jnp.int32)
        n_used = (pad_ends[-1] // MOE_BLOCK).astype(jnp.int32).reshape(1)

        xs = _dispatch(dest2, h2, jnp.zeros((cap, d // 2), jnp.uint32))
        ys = _experts(blk_e, n_used, xs, w1[l].astype(BF16), w3[l].astype(BF16), w2[l].astype(BF16))
        x = _combine(dest2, x1, gates, ys).reshape(bsz, s_len, d)
    return x
```

```python
import functools
import math

import jax
import jax.numpy as jnp
import numpy as np
from jax import lax
from jax.experimental import pallas as pl
from jax.experimental.pallas import tpu as pltpu

F32 = jnp.float32
BF16 = jnp.bfloat16

EPS = 1e-6
LANES = 128
DA_HEADS = 4
DA_HEAD_DIM = 64
SB_HEADS = 8
SB_HEAD_DIM = 64
CHUNK = 64
REL_BUCKETS = 32
REL_MAX_DIST = 128
N_GROUPS = 4
EXPERTS_PER_GROUP = 8
N_EXPERTS = N_GROUPS * EXPERTS_PER_GROUP
NEG_BIG = -1e30
LOG2E = math.log2(math.e)
SB_DEAD = -87.5

VMEM_LIMIT = 48 * 1024 * 1024

PROJ_ROWS = 1024
ROUTE_ROWS = 48
DA_BLOCK = 256
DA_TILES = 4
DA_UNROLL = 4
DA_SAFE_RANGE = 100.0
SB_SUB = 128
SB_ROWS = 1024
SB_WIN = 256
MOE_BLOCK = 512
MOE_CHUNKS = 2
TOK_TILE = 512


def _cparams(sem):
    return pltpu.CompilerParams(dimension_semantics=sem, vmem_limit_bytes=VMEM_LIMIT)


def _in_proj_kernel(x_ref, g_ref, w_ref, gain_ref, bd_ref, o_ref):
    x = x_ref[...]
    ms = jnp.mean(x * x, axis=-1, keepdims=True)
    h = (x * lax.rsqrt(ms + EPS) * g_ref[...]).astype(BF16)
    n_chunks = o_ref.shape[1] // 512
    for c in range(n_chunks):
        cols = slice(c * 512, (c + 1) * 512)
        acc = jnp.dot(h, w_ref[:, cols], preferred_element_type=F32)
        if c < 2:
            sq = (acc * acc).astype(BF16)
            parts = []
            for s in range(2):
                ss = jnp.dot(sq[:, s * 256:(s + 1) * 256], bd_ref[...], preferred_element_type=F32)
                parts.append(ss)
            ss = jnp.concatenate(parts, axis=1)
            acc = acc * lax.rsqrt(ss * (1.0 / DA_HEAD_DIM) + EPS)
        o_ref[:, cols] = (acc * gain_ref[:, cols]).astype(BF16)


def _in_proj(x2, g_attn, w_in_bf, gain_row, bd):
    n, d = x2.shape
    width = w_in_bf.shape[1]
    return pl.pallas_call(
        _in_proj_kernel,
        grid=(n // PROJ_ROWS,),
        in_specs=[
            pl.BlockSpec((PROJ_ROWS, d), lambda i: (i, 0)),
            pl.BlockSpec((1, d), lambda i: (0, 0)),
            pl.BlockSpec((d, width), lambda i: (0, 0)),
            pl.BlockSpec((1, width), lambda i: (0, 0)),
            pl.BlockSpec((256, 256), lambda i: (0, 0)),
        ],
        out_specs=pl.BlockSpec((PROJ_ROWS, width), lambda i: (i, 0)),
        out_shape=jax.ShapeDtypeStruct((n, width), BF16),
        compiler_params=_cparams(("parallel",)),
        name="in_proj",
    )(x2, g_attn, w_in_bf, gain_row, bd)


def _da_kernel(*refs, out_scale):
    def tile_body(sub, carry):
        _da_tile(pl.program_id(2) * DA_TILES + sub, sub, *refs, out_scale=out_scale)
        return carry

    lax.fori_loop(0, DA_TILES, tile_body, 0)


def _da_tile(i, sub, lam_ref, brange_ref, q_ref, k_ref, v_ref, bias_ref, g_ref, o_ref, m_scr, acc_scr, stab_scr,
             fits_scr, pa_scr, pb_scr, *, out_scale):
    t = DA_BLOCK
    h = pl.program_id(1)
    s_len = k_ref.shape[1]
    rows_of_tile = pl.ds(pl.multiple_of(sub * t, t), t)
    ones_col = jnp.ones((LANES, LANES), BF16)

    @pl.when(i == 0)
    def _():
        lane_k = lax.broadcasted_iota(jnp.int32, (1, LANES), 1)
        rows = 512

        def kbody(c, best):
            kc = k_ref[0, pl.ds(pl.multiple_of(c * rows, rows), rows), :].astype(F32)
            sq = kc * kc
            n1 = jnp.dot(jnp.where(lane_k < DA_HEAD_DIM, sq, 0.0).astype(BF16), ones_col, preferred_element_type=F32)
            n2 = jnp.dot(jnp.where(lane_k >= DA_HEAD_DIM, sq, 0.0).astype(BF16), ones_col, preferred_element_type=F32)
            return (jnp.maximum(best[0], jnp.max(n1, axis=0, keepdims=True)),
                    jnp.maximum(best[1], jnp.max(n2, axis=0, keepdims=True)))

        zero_row = jnp.zeros((1, LANES), F32)
        k1, k2 = lax.fori_loop(0, s_len // rows, kbody, (zero_row, zero_row))
        reach1 = jnp.max(jnp.sqrt(k1)) * (brange_ref[2, h] * 1.02)
        reach2 = jnp.max(jnp.sqrt(k2)) * (brange_ref[2, h] * 1.02)
        stab_scr[0] = reach1 + brange_ref[0, h]
        stab_scr[1] = reach2 + brange_ref[0, h]
        spread = 2.0 * jnp.maximum(reach1, reach2) + (brange_ref[0, h] - brange_ref[1, h])
        fits_scr[0] = (spread <= DA_SAFE_RANGE).astype(jnp.int32)

    q = q_ref[0, rows_of_tile, :]
    lane = lax.broadcasted_iota(jnp.int32, (1, LANES), 1)
    zero = jnp.zeros_like(q)
    qq = jnp.concatenate([jnp.where(lane < DA_HEAD_DIM, q, zero),
                          jnp.where(lane >= DA_HEAD_DIM, q, zero)], axis=0)
    ones = jnp.ones((t, LANES), BF16)
    row = lax.broadcasted_iota(jnp.int32, (t, t), 0)
    col = lax.broadcasted_iota(jnp.int32, (t, t), 1)
    chunk_mask = (col // CHUNK) <= (row // CHUNK)
    n_far = jnp.maximum(i - 1, 0)

    def values(j):
        return jnp.concatenate([v_ref[0, pl.ds(pl.multiple_of(j * t, t), t), :], ones], axis=1)

    def scores(j, bias, mask):
        kb = k_ref[0, pl.ds(pl.multiple_of(j * t, t), t), :]
        s = lax.dot_general(qq, kb, (((1,), (1,)), ((), ())), preferred_element_type=F32)
        if bias is not None:
            s = s + jnp.concatenate([bias, bias], axis=0)
        if mask is not None:
            s = jnp.where(jnp.concatenate([mask, mask], axis=0) if mask.ndim else mask, s, NEG_BIG)
        return s, values(j)

    fits = fits_scr[0] != 0

    @pl.when(fits)
    def _():
        stab = jnp.concatenate([jnp.full((t, t), stab_scr[0], F32), jnp.full((t, t), stab_scr[1], F32)], axis=0)

        def probs_into(g, p_scr):
            for u in range(DA_UNROLL):
                s, _ = scores(g * DA_UNROLL + u, None, None)
                p_scr[u] = jnp.exp2(s - stab).astype(BF16)

        def weighted_from(g, p_scr):
            tot = jnp.dot(p_scr[0], values(g * DA_UNROLL), preferred_element_type=F32)
            for u in range(1, DA_UNROLL):
                tot = tot + jnp.dot(p_scr[u], values(g * DA_UNROLL + u), preferred_element_type=F32)
            acc_scr[...] += tot

        n_grp = n_far // DA_UNROLL

        left = n_far - n_grp * DA_UNROLL

        def head_region(n_left):
            tail = [(n_grp * DA_UNROLL + u, None, None) for u in range(n_left)]
            tail += [(n_far, bias_ref[0, 1], i > 0), (i, bias_ref[0, 0], chunk_mask)]
            tail_scores = [scores(j, bias, mask) for j, bias, mask in tail]
            probs_into(0, pa_scr)
            tot = None
            for s, vb in tail_scores:
                pv = jnp.dot(jnp.exp2(s - stab).astype(BF16), vb, preferred_element_type=F32)
                tot = pv if tot is None else tot + pv
            acc_scr[...] = tot

        for n_left in range(DA_UNROLL):
            pl.when(left == n_left)(functools.partial(head_region, n_left))

        def pair_body(hp, carry):
            g = 2 * hp
            weighted_from(g, pa_scr)
            probs_into(g + 1, pb_scr)
            weighted_from(g + 1, pb_scr)
            probs_into(g + 2, pa_scr)
            return carry

        lax.fori_loop(0, (n_grp - 1) // 2, pair_body, 0)

        @pl.when(n_grp % 2 == 1)
        def _():
            weighted_from(n_grp - 1, pa_scr)

        @pl.when((n_grp % 2 == 0) & (n_grp > 0))
        def _():
            weighted_from(n_grp - 2, pa_scr)
            probs_into(n_grp - 1, pb_scr)
            weighted_from(n_grp - 1, pb_scr)

    @pl.when(jnp.logical_not(fits))
    def _():
        m_scr[...] = jnp.full(m_scr.shape, NEG_BIG, F32)
        acc_scr[...] = jnp.zeros(acc_scr.shape, F32)

        def step(j, bias=None, mask=None):
            s, vb = scores(j, bias, mask)
            m_prev = m_scr[...]
            m_next = jnp.maximum(m_prev, jnp.max(s, axis=1, keepdims=True))
            alpha = jnp.exp2(m_prev - m_next)
            p = jnp.exp2(s - jnp.concatenate([m_next] * (t // LANES), axis=1))
            pv = jnp.dot(p.astype(BF16), vb, preferred_element_type=F32)
            acc_scr[...] = jnp.concatenate([alpha, alpha], axis=1) * acc_scr[...] + pv
            m_scr[...] = m_next

        def far_body(j, carry):
            step(j)
            return carry

        lax.fori_loop(0, n_far, far_body, 0)

        @pl.when(i > 0)
        def _():
            step(i - 1, bias=bias_ref[0, 1])

        step(i, bias=bias_ref[0, 0], mask=chunk_mask)

    acc = acc_scr[...]
    o_all = acc[:, 0:LANES] / acc[:, LANES:2 * LANES]
    o = o_all[0:t] - lam_ref[0] * o_all[t:2 * t]
    ms = jnp.mean(o * o, axis=-1, keepdims=True)
    o_ref[0, rows_of_tile, :] = (o * lax.rsqrt(ms + EPS) * (g_ref[...] * out_scale)).astype(BF16)


def _diff_attention(proj3, lam, bias_range, bias_tiles, subln_row, out_scale):
    bsz, s_len, _ = proj3.shape
    t = DA_BLOCK
    tq = DA_TILES * t
    nq = s_len // tq
    kern = functools.partial(_da_kernel, out_scale=out_scale)
    return pl.pallas_call(
        kern,
        grid=(bsz, DA_HEADS, nq),
        in_specs=[
            pl.BlockSpec(memory_space=pltpu.SMEM),
            pl.BlockSpec(memory_space=pltpu.SMEM),
            pl.BlockSpec((1, tq, LANES), lambda b, h, i: (b, i, h)),
            pl.BlockSpec((1, s_len, LANES), lambda b, h, i: (b, 0, DA_HEADS + h)),
            pl.BlockSpec((1, s_len, LANES), lambda b, h, i: (b, 0, 2 * DA_HEADS + h)),
            pl.BlockSpec((1, 2, t, t), lambda b, h, i: (h, 0, 0, 0)),
            pl.BlockSpec((1, LANES), lambda b, h, i: (0, 0)),
        ],
        out_specs=pl.BlockSpec((1, tq, LANES), lambda b, h, i: (b, i, h)),
        out_shape=jax.ShapeDtypeStruct((bsz, s_len, DA_HEADS * LANES), BF16),
        scratch_shapes=[
            pltpu.VMEM((2 * t, LANES), F32),
            pltpu.VMEM((2 * t, 2 * LANES), F32),
            pltpu.SMEM((2,), F32),
            pltpu.SMEM((1,), jnp.int32),
            pltpu.VMEM((DA_UNROLL, 2 * t, t), BF16),
            pltpu.VMEM((DA_UNROLL, 2 * t, t), BF16),
        ],
        compiler_params=_cparams(("parallel", "parallel", "arbitrary")),
        name="diff_attention",
    )(lam, bias_range, proj3, proj3, proj3, bias_tiles, subln_row)


def _sb_tiles(tiles, tri):
    zs = [lax.dot_general(q, k, (((1,), (1,)), ((), ())), preferred_element_type=F32) for q, k, _, _, _ in tiles]
    mids = []
    for z, (_, _, _, causal, _) in zip(zs, tiles):
        log_1m = -(jnp.maximum(z, 0.0) + jnp.log(1.0 + jnp.exp(-jnp.abs(z))))
        lm = log_1m if causal is None else jnp.where(causal, log_1m, 0.0)
        hi = lm.astype(BF16)
        mid = (lm - hi.astype(F32)).astype(BF16)
        inner = jnp.dot(hi, tri, preferred_element_type=F32) + jnp.dot(mid, tri, preferred_element_type=F32)
        mids.append((z + log_1m + inner, jnp.sum(lm, axis=1, keepdims=True)))
    outs = []
    for (logit, total), (_, _, v, causal, carry) in zip(mids, tiles):
        if carry is not None:
            logit = logit + carry
        a = jnp.exp(logit)
        if causal is not None:
            a = jnp.where(causal, a, 0.0)
        outs.append((jnp.dot(a.astype(BF16), v, preferred_element_type=F32), total))
    return outs


def _sb_kernel(q_ref, k_ref, v_ref, g_ref, tri_ref, o_ref, acc_scr, carry_scr):
    sub, win = SB_SUB, SB_WIN
    i = pl.program_id(2)
    lane = lax.broadcasted_iota(jnp.int32, (1, LANES), 1)
    tri = tri_ref[...]
    row = lax.broadcasted_iota(jnp.int32, (sub, win), 0)
    col = lax.broadcasted_iota(jnp.int32, (sub, win), 1)
    n_sub = SB_ROWS // sub

    tiles, starts = [], []
    for u in range(n_sub):
        r0 = i * SB_ROWS + u * sub
        start = pl.multiple_of(jnp.maximum(r0 - sub, 0), sub)
        qu = q_ref[0, u * sub:(u + 1) * sub, :]
        zero = jnp.zeros_like(qu)
        qs = jnp.concatenate([jnp.where(lane < SB_HEAD_DIM, qu, zero),
                              jnp.where(lane >= SB_HEAD_DIM, qu, zero)], axis=0)
        causal = (start + col) < (r0 + row)
        tiles.append((qs, k_ref[0, pl.ds(start, win), :], v_ref[0, pl.ds(start, win), :],
                      jnp.concatenate([causal, causal], axis=0), None))
        starts.append(start)
    chains = []
    for u, (pv, total) in enumerate(_sb_tiles(tiles, tri)):
        acc_scr[u] = pv
        carry_scr[u] = jnp.broadcast_to(total, (2 * sub, LANES))
        chains.append((u, tiles[u][0], starts[u] // sub - 1, jnp.max(total)))

    tri_sub = tri[0:sub, 0:sub]
    for u, qs, j0, alive0 in chains:
        def cond(state):
            j, alive = state
            return (j >= 0) & (alive > SB_DEAD)

        def body(state, u=u, qs=qs):
            j, _ = state
            s0 = pl.multiple_of(j * sub, sub)
            kb = k_ref[0, pl.ds(s0, sub), :]
            vb = v_ref[0, pl.ds(s0, sub), :]
            carry = carry_scr[u]
            (pv, total), = _sb_tiles([(qs, kb, vb, None, carry)], tri_sub)
            acc_scr[u] += pv
            carry = carry + total
            carry_scr[u] = carry
            return j - 1, jnp.max(carry)

        lax.while_loop(cond, body, (j0, alive0))

    for u in range(n_sub):
        acc = acc_scr[u]
        o = jnp.where(lane < SB_HEAD_DIM, acc[0:sub], acc[sub:2 * sub])
        sq = o * o
        ss0 = jnp.sum(jnp.where(lane < SB_HEAD_DIM, sq, 0.0), axis=1, keepdims=True)
        ss1 = jnp.sum(jnp.where(lane >= SB_HEAD_DIM, sq, 0.0), axis=1, keepdims=True)
        ms = jnp.where(lane < SB_HEAD_DIM, ss0, ss1) * (1.0 / SB_HEAD_DIM)
        o_ref[0, u * sub:(u + 1) * sub, :] = (o * lax.rsqrt(ms + EPS) * g_ref[...]).astype(BF16)


def _stick_breaking(proj3, sb_row, tri):
    bsz, s_len, _ = proj3.shape
    t = SB_ROWS
    nq = s_len // t
    pairs = SB_HEADS // 2
    q_blk = 3 * DA_HEADS
    n_chains = SB_ROWS // SB_SUB
    return pl.pallas_call(
        _sb_kernel,
        grid=(bsz, pairs, nq),
        in_specs=[
            pl.BlockSpec((1, t, LANES), lambda b, h, i: (b, i, q_blk + h)),
            pl.BlockSpec((1, s_len, LANES), lambda b, h, i: (b, 0, q_blk + pairs + h)),
            pl.BlockSpec((1, s_len, LANES), lambda b, h, i: (b, 0, q_blk + 2 * pairs + h)),
            pl.BlockSpec((1, LANES), lambda b, h, i: (0, 0)),
            pl.BlockSpec((SB_WIN, SB_WIN), lambda b, h, i: (0, 0)),
        ],
        out_specs=pl.BlockSpec((1, t, LANES), lambda b, h, i: (b, i, h)),
        out_shape=jax.ShapeDtypeStruct((bsz, s_len, pairs * LANES), BF16),
        scratch_shapes=[
            pltpu.VMEM((n_chains, 2 * SB_SUB, LANES), F32),
            pltpu.VMEM((n_chains, 2 * SB_SUB, LANES), F32),
        ],
        compiler_params=_cparams(("parallel", "parallel", "arbitrary")),
        name="stick_breaking",
    )(proj3, proj3, proj3, sb_row, tri)


def _out_proj_kernel(x_ref, oda_ref, osb_ref, wo_ref, g_ref, wr_ref, br_ref, upper_ref,
                     x1_ref, h2_ref, route_i_ref, route_f_ref, counts_ref, base_scr):
    half = oda_ref.shape[1]
    x1 = (x_ref[...]
          + jnp.dot(oda_ref[...], wo_ref[0:half, :], preferred_element_type=F32)
          + jnp.dot(osb_ref[...], wo_ref[half:2 * half, :], preferred_element_type=F32))
    x1_ref[...] = x1
    ms = jnp.mean(x1 * x1, axis=-1, keepdims=True)
    h2 = x1 * lax.rsqrt(ms + EPS) * g_ref[...]
    d_half = h2.shape[1] // 2
    hi_bits = lax.bitcast_convert_type(h2[:, :d_half].astype(BF16).astype(F32), jnp.uint32)
    lo_bits = lax.bitcast_convert_type(h2[:, d_half:].astype(BF16).astype(F32), jnp.uint32)
    h2_ref[...] = hi_bits | (lo_bits >> 16)

    a_hi = h2.astype(BF16)
    a_mid = (h2 - a_hi.astype(F32)).astype(BF16)
    nt = lambda w, a: lax.dot_general(w, a, (((1,), (1,)), ((), ())), preferred_element_type=F32)
    lt = nt(wr_ref[0], a_hi) + (nt(wr_ref[0], a_mid) + nt(wr_ref[1], a_hi)) + br_ref[...]
    n_rows, tm = lt.shape
    ridx = lax.broadcasted_iota(jnp.int32, (n_rows, tm), 0)
    big = jnp.int32(n_rows)
    is_g = ridx < N_GROUPS
    lg = jnp.where(is_g, lt, NEG_BIG)
    mg = jnp.max(lg, axis=0, keepdims=True)
    gsel = jnp.min(jnp.where(lg == mg, ridx, big), axis=0, keepdims=True)
    pg_sel = 1.0 / jnp.sum(jnp.where(is_g, jnp.exp(lg - mg), 0.0), axis=0, keepdims=True)
    lo_row = N_GROUPS + gsel * EXPERTS_PER_GROUP
    le = jnp.where(ridx >= lo_row, jnp.where(ridx < lo_row + EXPERTS_PER_GROUP, lt, NEG_BIG), NEG_BIG)
    l1 = jnp.max(le, axis=0, keepdims=True)
    i1 = jnp.min(jnp.where(le == l1, ridx, big), axis=0, keepdims=True)
    le2 = jnp.where(ridx == i1, NEG_BIG, le)
    l2 = jnp.max(le2, axis=0, keepdims=True)
    i2 = jnp.min(jnp.where(le2 == l2, ridx, big), axis=0, keepdims=True)
    e2 = jnp.exp(l2 - l1)
    g1 = pg_sel / (1.0 + e2)
    g2 = pg_sel * e2 / (1.0 + e2)

    @pl.when(pl.program_id(0) == 0)
    def _():
        base_scr[...] = jnp.zeros(base_scr.shape, F32)

    sel1 = ridx == i1
    sel2 = ridx == i2
    chosen = jnp.where(sel1, 1.0, jnp.where(sel2, 1.0, 0.0))
    before = jnp.dot(chosen.astype(BF16), upper_ref[...], preferred_element_type=F32)
    base = base_scr[...]
    before = before + jnp.concatenate([base] * (tm // LANES), axis=1)
    rank1 = jnp.sum(jnp.where(sel1, before, 0.0), axis=0, keepdims=True)
    rank2 = jnp.sum(jnp.where(sel2, before, 0.0), axis=0, keepdims=True)
    base = base + jnp.sum(chosen, axis=1, keepdims=True)
    base_scr[...] = base
    counts_ref[...] = base.astype(jnp.int32)

    r8 = lax.broadcasted_iota(jnp.int32, (8, tm), 0)
    route_i_ref[...] = jnp.where(r8 == 0, i1 - N_GROUPS, jnp.where(r8 == 1, i2 - N_GROUPS, jnp.where(
        r8 == 2, rank1.astype(jnp.int32), jnp.where(r8 == 3, rank2.astype(jnp.int32), 0))))
    route_f_ref[...] = jnp.where(r8 == 0, g1, jnp.where(r8 == 1, g2, 0.0))


def _out_proj(x2, o_da, o_sb, wo_bf, g_ffn, wr2, br, upper):
    n, d = x2.shape
    half = o_da.shape[1]
    tm = PROJ_ROWS
    r = ROUTE_ROWS
    return pl.pallas_call(
        _out_proj_kernel,
        grid=(n // tm,),
        in_specs=[
            pl.BlockSpec((tm, d), lambda i: (i, 0)),
            pl.BlockSpec((tm, half), lambda i: (i, 0)),
            pl.BlockSpec((tm, half), lambda i: (i, 0)),
            pl.BlockSpec((2 * half, d), lambda i: (0, 0)),
            pl.BlockSpec((1, d), lambda i: (0, 0)),
            pl.BlockSpec((2, r, d), lambda i: (0, 0, 0)),
            pl.BlockSpec((r, tm), lambda i: (0, 0)),
            pl.BlockSpec((tm, tm), lambda i: (0, 0)),
        ],
        out_specs=[
            pl.BlockSpec((tm, d), lambda i: (i, 0)),
            pl.BlockSpec((tm, d // 2), lambda i: (i, 0)),
            pl.BlockSpec((8, tm), lambda i: (0, i)),
            pl.BlockSpec((8, tm), lambda i: (0, i)),
            pl.BlockSpec((r, LANES), lambda i: (0, 0)),
        ],
        out_shape=[
            jax.ShapeDtypeStruct((n, d), F32),
            jax.ShapeDtypeStruct((n, d // 2), jnp.uint32),
            jax.ShapeDtypeStruct((8, n), jnp.int32),
            jax.ShapeDtypeStruct((8, n), F32),
            jax.ShapeDtypeStruct((r, LANES), jnp.int32),
        ],
        scratch_shapes=[pltpu.VMEM((r, LANES), F32)],
        compiler_params=_cparams(("arbitrary",)),
        name="out_proj_router",
    )(x2, o_da, o_sb, wo_bf, g_ffn, wr2, br, upper)


def _dispatch_kernel(dest_ref, h_ref, xs_in_ref, xs_ref, sem):
    del xs_in_ref
    tt = h_ref.shape[0]

    def copy(r, k):
        return pltpu.make_async_copy(h_ref.at[pl.ds(r, 1)], xs_ref.at[pl.ds(dest_ref[0, 0, 2 * r + k], 1)], sem)

    for r in range(tt):
        copy(r, 0).start(priority=0)
        copy(r, 1).start(priority=1)

    def drain(r, c):
        copy(r, 0).wait()
        copy(r, 1).wait()
        return c

    lax.fori_loop(0, tt, drain, 0, unroll=True)


def _dispatch(dest2, h2, xs_init):
    n, d = h2.shape
    tt = TOK_TILE
    return pl.pallas_call(
        _dispatch_kernel,
        grid=(n // tt,),
        in_specs=[
            pl.BlockSpec((1, 1, 2 * tt), lambda i: (i, 0, 0), memory_space=pltpu.SMEM),
            pl.BlockSpec((tt, d), lambda i: (i, 0)),
            pl.BlockSpec(memory_space=pl.ANY),
        ],
        out_specs=pl.BlockSpec(memory_space=pl.ANY),
        out_shape=jax.ShapeDtypeStruct(xs_init.shape, xs_init.dtype),
        scratch_shapes=[pltpu.SemaphoreType.DMA(())],
        input_output_aliases={2: 0},
        compiler_params=_cparams(("arbitrary",)),
        name="moe_dispatch",
    )(dest2, h2, xs_init)


def _expert_kernel(blk_e_ref, n_used_ref, xs_ref, w1_ref, w3_ref, w2_ref, ys_ref):
    i = pl.program_id(0)

    @pl.when(i < n_used_ref[0])
    def _():
        rows = MOE_BLOCK // MOE_CHUNKS
        xbs = []
        for c in range(MOE_CHUNKS):
            words = xs_ref[c * rows:(c + 1) * rows, :]
            xbs.append(jnp.concatenate([
                lax.bitcast_convert_type(words & jnp.uint32(0xFFFF0000), F32).astype(BF16),
                lax.bitcast_convert_type(words << 16, F32).astype(BF16)], axis=1))
        gated = [(jnp.dot(xb, w1_ref[0], preferred_element_type=F32), jnp.dot(xb, w3_ref[0], preferred_element_type=F32))
                 for xb in xbs]
        hmids = [(a * jax.nn.sigmoid(a) * b).astype(BF16) for a, b in gated]
        for c, hmid in enumerate(hmids):
            ys_ref[c * rows:(c + 1) * rows, :] = jnp.dot(hmid, w2_ref[0], preferred_element_type=F32)

    @pl.when(i >= n_used_ref[0])
    def _():
        ys_ref[...] = jnp.zeros(ys_ref.shape, F32)


def _experts(blk_e, n_used, xs, w1_bf, w3_bf, w2_bf):
    cap = xs.shape[0]
    d, hid = w1_bf.shape[1], w1_bf.shape[2]
    nb = cap // MOE_BLOCK

    def row_map(i, blk_e_ref, n_used_ref):
        return (jnp.minimum(i, n_used_ref[0] - 1), 0)

    def w_map(i, blk_e_ref, n_used_ref):
        return (blk_e_ref[jnp.minimum(i, n_used_ref[0] - 1)], 0, 0)

    grid_spec = pltpu.PrefetchScalarGridSpec(
        num_scalar_prefetch=2,
        grid=(nb,),
        in_specs=[
            pl.BlockSpec((MOE_BLOCK, d // 2), row_map),
            pl.BlockSpec((1, d, hid), w_map),
            pl.BlockSpec((1, d, hid), w_map),
            pl.BlockSpec((1, hid, d), w_map),
        ],
        out_specs=pl.BlockSpec((MOE_BLOCK, d), lambda i, blk_e_ref, n_used_ref: (i, 0)),
    )
    return pl.pallas_call(
        _expert_kernel,
        grid_spec=grid_spec,
        out_shape=jax.ShapeDtypeStruct((cap, d), F32),
        compiler_params=_cparams(("arbitrary",)),
        name="moe_experts",
    )(blk_e, n_used, xs, w1_bf, w3_bf, w2_bf)


def _combine_kernel(dest_ref, x1_ref, gates_ref, ys_ref, o_ref, y0_scr, y1_scr, sem):
    tt = x1_ref.shape[0]

    def copy(r, k):
        dst = y0_scr if k == 0 else y1_scr
        return pltpu.make_async_copy(ys_ref.at[pl.ds(dest_ref[0, 0, 2 * r + k], 1)], dst.at[pl.ds(r, 1)], sem)

    for r in range(tt):
        copy(r, 0).start(priority=0)
        copy(r, 1).start(priority=1)

    def drain(r, c):
        copy(r, 0).wait()
        copy(r, 1).wait()
        return c

    lax.fori_loop(0, tt, drain, 0, unroll=True)
    g = gates_ref[...]
    o_ref[...] = x1_ref[...] + y0_scr[...] * g[:, 0:1] + y1_scr[...] * g[:, 1:2]


def _combine(dest2, x1, gates, ys):
    n, d = x1.shape
    tt = TOK_TILE
    return pl.pallas_call(
        _combine_kernel,
        grid=(n // tt,),
        in_specs=[
            pl.BlockSpec((1, 1, 2 * tt), lambda i: (i, 0, 0), memory_space=pltpu.SMEM),
            pl.BlockSpec((tt, d), lambda i: (i, 0)),
            pl.BlockSpec((tt, LANES), lambda i: (i, 0)),
            pl.BlockSpec(memory_space=pl.ANY),
        ],
        out_specs=pl.BlockSpec((tt, d), lambda i: (i, 0)),
        out_shape=jax.ShapeDtypeStruct((n, d), F32),
        scratch_shapes=[
            pltpu.VMEM((tt, d), F32),
            pltpu.VMEM((tt, d), F32),
            pltpu.SemaphoreType.DMA(()),
        ],
        compiler_params=_cparams(("arbitrary",)),
        name="moe_combine",
    )(dest2, x1, gates, ys)


def _rel_bucket_np(rel):
    nb = REL_BUCKETS // 2
    max_exact = nb // 2
    base = np.where(rel > 0, nb, 0)
    n = np.abs(rel)
    nf = np.maximum(n, 1).astype(np.float64)
    large = max_exact + (np.log(nf / max_exact) / math.log(REL_MAX_DIST / max_exact) * (nb - max_exact)).astype(np.int64)
    large = np.minimum(large, nb - 1)
    return (base + np.where(n < max_exact, n, large)).astype(np.int32)


def _bias_bucket_rows(t):
    far = _rel_bucket_np(-np.arange(t + 1, 4 * t))
    far_bucket = int(far[0])
    assert (far == far_bucket).all()
    m = np.arange(2 * t)
    wrap = np.where(m < t, m, m - 2 * t)
    return np.stack([_rel_bucket_np(wrap), _rel_bucket_np(wrap - t)]), far_bucket


def _toeplitz(rows, t):
    lead = rows.shape[:-1]
    flat = jnp.tile(rows, (1,) * len(lead) + (t,))[..., : t * (2 * t - 1)]
    return flat.reshape(lead + (t, 2 * t - 1))[..., :t]


def kernel(x, g_attn, w_in, qn_g, kn_g, lam_q1, lam_k1, lam_q2, lam_k2, subln_g, sb_out_g, rel_bias, w_o,
           g_ffn, w_router_g, b_router_g, w_router_e, b_router_e, w1, w3, w2):
    bsz, s_len, d = x.shape
    n = bsz * s_len
    depth = g_attn.shape[0]
    da_width = DA_HEADS * 2 * DA_HEAD_DIM
    sb_width = SB_HEADS * SB_HEAD_DIM

    bd = jnp.asarray(np.kron(np.eye(256 // DA_HEAD_DIM), np.ones((DA_HEAD_DIM, DA_HEAD_DIM))), BF16)
    tri = jnp.asarray(np.tril(np.ones((SB_WIN, SB_WIN)), -1), BF16)
    bucket_rows, far_bucket = _bias_bucket_rows(DA_BLOCK)
    upper = jnp.asarray(np.triu(np.ones((PROJ_ROWS, PROJ_ROWS)), 1), BF16)

    for l in range(depth):
        lambda_init = 0.8 - 0.6 * math.exp(-0.3 * l)
        x2 = x.reshape(n, d)

        q_scale = DA_HEAD_DIM ** -0.5 * LOG2E
        gain_row = jnp.concatenate([
            jnp.tile(qn_g[l] * q_scale, 2 * DA_HEADS),
            jnp.tile(kn_g[l], 2 * DA_HEADS),
            jnp.ones((da_width,), F32),
            jnp.full((sb_width,), SB_HEAD_DIM ** -0.5, F32),
            jnp.ones((2 * sb_width,), F32),
        ])[None, :]
        proj = _in_proj(x2, g_attn[l][None, :], w_in[l].astype(BF16), gain_row, bd)
        proj3 = proj.reshape(bsz, s_len, -1)

        lam = (jnp.exp(jnp.sum(lam_q1[l] * lam_k1[l])) - jnp.exp(jnp.sum(lam_q2[l] * lam_k2[l]))
               + lambda_init).astype(F32).reshape(1)
        rb = (rel_bias - rel_bias[far_bucket][None, :]) * LOG2E
        bias_rows = jnp.transpose(rb[bucket_rows], (2, 0, 1)).astype(F32)
        bias_tiles = _toeplitz(bias_rows, DA_BLOCK)
        q_reach = jnp.full((DA_HEADS,), DA_HEAD_DIM ** 0.5 * 1.02) * jnp.max(jnp.abs(qn_g[l] * q_scale))
        bias_range = jnp.stack([jnp.maximum(jnp.max(rb, axis=0), 0.0), jnp.minimum(jnp.min(rb, axis=0), 0.0),
                                q_reach])
        o_da = _diff_attention(proj3, lam, bias_range.astype(F32), bias_tiles, subln_g[l][None, :],
                               1.0 - lambda_init)
        o_sb = _stick_breaking(proj3, jnp.tile(sb_out_g[l], 2)[None, :], tri)

        wr = jnp.concatenate([w_router_g[l], w_router_e[l],
                              jnp.zeros((d, ROUTE_ROWS - N_GROUPS - N_EXPERTS), F32)], axis=1).T
        wr_hi = wr.astype(BF16)
        wr_mid = (wr - wr_hi.astype(F32)).astype(BF16)
        br = jnp.concatenate([b_router_g[l], b_router_e[l],
                              jnp.zeros((ROUTE_ROWS - N_GROUPS - N_EXPERTS,), F32)])
        br = jnp.broadcast_to(br[:, None], (ROUTE_ROWS, PROJ_ROWS))
        x1, h2, route_i, route_f, counts_all = _out_proj(
            x2, o_da.reshape(n, -1), o_sb.reshape(n, -1), w_o[l].astype(BF16), g_ffn[l][None, :],
            jnp.stack([wr_hi, wr_mid]), br, upper)

        counts = counts_all[N_GROUPS:N_GROUPS + N_EXPERTS, 0]
        padded = (counts + MOE_BLOCK - 1) // MOE_BLOCK * MOE_BLOCK
        pad_ends = jnp.cumsum(padded)
        pad_starts = pad_ends - padded
        experts = jnp.arange(N_EXPERTS, dtype=jnp.int32)
        eid, rank = route_i[0:2], route_i[2:4]
        start_of = jnp.sum(jnp.where(eid[:, :, None] == experts, pad_starts, 0), axis=-1)
        dest2 = (start_of + rank).T.reshape(n // TOK_TILE, 1, 2 * TOK_TILE)
        gates = jnp.pad(route_f[0:2].T, ((0, 0), (0, LANES - 2)))
        cap = 2 * n + N_EXPERTS * MOE_BLOCK
        nb = cap // MOE_BLOCK
        blk_start = jnp.arange(nb, dtype=jnp.int32) * MOE_BLOCK
        blk_e = jnp.minimum(jnp.sum(pad_ends[None, :] <= blk_start[:, None], axis=1), N_EXPERTS - 1).astype(jnp.int32)
        n_used = (pad_ends[-1] // MOE_BLOCK).astype(jnp.int32).reshape(1)

        xs = _dispatch(dest2, h2, jnp.zeros((cap, d // 2), jnp.uint32))
        ys = _experts(blk_e, n_used, xs, w1[l].astype(BF16), w3[l].astype(BF16), w2[l].astype(BF16))
        x = _combine(dest2, x1, gates, ys).reshape(bsz, s_len, d)
    return x
```

```python
import functools
import math

import jax
import jax.numpy as jnp
import numpy as np
from jax import lax
from jax.experimental import pallas as pl
from jax.experimental.pallas import tpu as pltpu

F32 = jnp.float32
BF16 = jnp.bfloat16

EPS = 1e-6
LANES = 128
DA_HEADS = 4
DA_HEAD_DIM = 64
SB_HEADS = 8
SB_HEAD_DIM = 64
CHUNK = 64
REL_BUCKETS = 32
REL_MAX_DIST = 128
N_GROUPS = 4
EXPERTS_PER_GROUP = 8
N_EXPERTS = N_GROUPS * EXPERTS_PER_GROUP
NEG_BIG = -1e30
LOG2E = math.log2(math.e)
SB_DEAD = -87.5

VMEM_LIMIT = 48 * 1024 * 1024

PROJ_ROWS = 1024
ROUTE_ROWS = 48
DA_BLOCK = 256
DA_TILES = 4
DA_UNROLL = 4
DA_SAFE_RANGE = 100.0
SB_SUB = 128
SB_ROWS = 1024
SB_WIN = 256
MOE_BLOCK = 512
MOE_CHUNKS = 2
TOK_TILE = 512


def _cparams(sem):
    return pltpu.CompilerParams(dimension_semantics=sem, vmem_limit_bytes=VMEM_LIMIT)


def _in_proj_kernel(x_ref, g_ref, w_ref, gain_ref, bd_ref, o_ref):
    x = x_ref[...]
    ms = jnp.mean(x * x, axis=-1, keepdims=True)
    h = (x * lax.rsqrt(ms + EPS) * g_ref[...]).astype(BF16)
    n_chunks = o_ref.shape[1] // 512
    for c in range(n_chunks):
        cols = slice(c * 512, (c + 1) * 512)
        acc = jnp.dot(h, w_ref[:, cols], preferred_element_type=F32)
        if c < 2:
            sq = (acc * acc).astype(BF16)
            parts = []
            for s in range(2):
                ss = jnp.dot(sq[:, s * 256:(s + 1) * 256], bd_ref[...], preferred_element_type=F32)
                parts.append(ss)
            ss = jnp.concatenate(parts, axis=1)
            acc = acc * lax.rsqrt(ss * (1.0 / DA_HEAD_DIM) + EPS)
        o_ref[:, cols] = (acc * gain_ref[:, cols]).astype(BF16)


def _in_proj(x2, g_attn, w_in_bf, gain_row, bd):
    n, d = x2.shape
    width = w_in_bf.shape[1]
    return pl.pallas_call(
        _in_proj_kernel,
        grid=(n // PROJ_ROWS,),
        in_specs=[
            pl.BlockSpec((PROJ_ROWS, d), lambda i: (i, 0)),
            pl.BlockSpec((1, d), lambda i: (0, 0)),
            pl.BlockSpec((d, width), lambda i: (0, 0)),
            pl.BlockSpec((1, width), lambda i: (0, 0)),
            pl.BlockSpec((256, 256), lambda i: (0, 0)),
        ],
        out_specs=pl.BlockSpec((PROJ_ROWS, width), lambda i: (i, 0)),
        out_shape=jax.ShapeDtypeStruct((n, width), BF16),
        compiler_params=_cparams(("parallel",)),
        name="in_proj",
    )(x2, g_attn, w_in_bf, gain_row, bd)


def _da_kernel(*refs, out_scale):
    def tile_body(sub, carry):
        _da_tile(pl.program_id(2) * DA_TILES + sub, sub, *refs, out_scale=out_scale)
        return carry

    lax.fori_loop(0, DA_TILES, tile_body, 0)


def _da_tile(i, sub, lam_ref, brange_ref, q_ref, k_ref, v_ref, bias_ref, g_ref, o_ref, m_scr, acc_scr, stab_scr,
             fits_scr, pa_scr, pb_scr, *, out_scale):
    t = DA_BLOCK
    h = pl.program_id(1)
    s_len = k_ref.shape[1]
    rows_of_tile = pl.ds(pl.multiple_of(sub * t, t), t)
    ones_col = jnp.ones((LANES, LANES), BF16)

    @pl.when(i == 0)
    def _():
        lane_k = lax.broadcasted_iota(jnp.int32, (1, LANES), 1)
        rows = 512

        def kbody(c, best):
            kc = k_ref[0, pl.ds(pl.multiple_of(c * rows, rows), rows), :].astype(F32)
            sq = kc * kc
            n1 = jnp.dot(jnp.where(lane_k < DA_HEAD_DIM, sq, 0.0).astype(BF16), ones_col, preferred_element_type=F32)
            n2 = jnp.dot(jnp.where(lane_k >= DA_HEAD_DIM, sq, 0.0).astype(BF16), ones_col, preferred_element_type=F32)
            return (jnp.maximum(best[0], jnp.max(n1, axis=0, keepdims=True)),
                    jnp.maximum(best[1], jnp.max(n2, axis=0, keepdims=True)))

        zero_row = jnp.zeros((1, LANES), F32)
        k1, k2 = lax.fori_loop(0, s_len // rows, kbody, (zero_row, zero_row))
        reach1 = jnp.max(jnp.sqrt(k1)) * (brange_ref[2, h] * 1.02)
        reach2 = jnp.max(jnp.sqrt(k2)) * (brange_ref[2, h] * 1.02)
        stab_scr[0] = reach1 + brange_ref[0, h]
        stab_scr[1] = reach2 + brange_ref[0, h]
        spread = 2.0 * jnp.maximum(reach1, reach2) + (brange_ref[0, h] - brange_ref[1, h])
        fits_scr[0] = (spread <= DA_SAFE_RANGE).astype(jnp.int32)

    q = q_ref[0, rows_of_tile, :]
    lane = lax.broadcasted_iota(jnp.int32, (1, LANES), 1)
    zero = jnp.zeros_like(q)
    qq = jnp.concatenate([jnp.where(lane < DA_HEAD_DIM, q, zero),
                          jnp.where(lane >= DA_HEAD_DIM, q, zero)], axis=0)
    ones = jnp.ones((t, LANES), BF16)
    row = lax.broadcasted_iota(jnp.int32, (t, t), 0)
    col = lax.broadcasted_iota(jnp.int32, (t, t), 1)
    chunk_mask = (col // CHUNK) <= (row // CHUNK)
    n_far = jnp.maximum(i - 1, 0)

    def values(j):
        return jnp.concatenate([v_ref[0, pl.ds(pl.multiple_of(j * t, t), t), :], ones], axis=1)

    def scores(j, bias, mask):
        kb = k_ref[0, pl.ds(pl.multiple_of(j * t, t), t), :]
        s = lax.dot_general(qq, kb, (((1,), (1,)), ((), ())), preferred_element_type=F32)
        if bias is not None:
            s = s + jnp.concatenate([bias, bias], axis=0)
        if mask is not None:
            s = jnp.where(jnp.concatenate([mask, mask], axis=0) if mask.ndim else mask, s, NEG_BIG)
        return s, values(j)

    fits = fits_scr[0] != 0

    @pl.when(fits)
    def _():
        stab = jnp.concatenate([jnp.full((t, t), stab_scr[0], F32), jnp.full((t, t), stab_scr[1], F32)], axis=0)

        def probs_into(g, p_scr):
            for u in range(DA_UNROLL):
                s, _ = scores(g * DA_UNROLL + u, None, None)
                p_scr[u] = jnp.exp2(s - stab).astype(BF16)

        def weighted_from(g, p_scr):
            tot = jnp.dot(p_scr[0], values(g * DA_UNROLL), preferred_element_type=F32)
            for u in range(1, DA_UNROLL):
                tot = tot + jnp.dot(p_scr[u], values(g * DA_UNROLL + u), preferred_element_type=F32)
            acc_scr[...] += tot

        n_grp = n_far // DA_UNROLL

        left = n_far - n_grp * DA_UNROLL

        def head_region(n_left):
            tail = [(n_grp * DA_UNROLL + u, None, None) for u in range(n_left)]
            tail += [(n_far, bias_ref[0, 1], i > 0), (i, bias_ref[0, 0], chunk_mask)]
            tail_scores = [scores(j, bias, mask) for j, bias, mask in tail]
            probs_into(0, pa_scr)
            tot = None
            for s, vb in tail_scores:
                pv = jnp.dot(jnp.exp2(s - stab).astype(BF16), vb, preferred_element_type=F32)
                tot = pv if tot is None else tot + pv
            acc_scr[...] = tot

        for n_left in range(DA_UNROLL):
            pl.when(left == n_left)(functools.partial(head_region, n_left))

        def pair_body(hp, carry):
            g = 2 * hp
            weighted_from(g, pa_scr)
            probs_into(g + 1, pb_scr)
            weighted_from(g + 1, pb_scr)
            probs_into(g + 2, pa_scr)
            return carry

        lax.fori_loop(0, (n_grp - 1) // 2, pair_body, 0)

        @pl.when(n_grp % 2 == 1)
        def _():
            weighted_from(n_grp - 1, pa_scr)

        @pl.when((n_grp % 2 == 0) & (n_grp > 0))
        def _():
            weighted_from(n_grp - 2, pa_scr)
            probs_into(n_grp - 1, pb_scr)
            weighted_from(n_grp - 1, pb_scr)

    @pl.when(jnp.logical_not(fits))
    def _():
        m_scr[...] = jnp.full(m_scr.shape, NEG_BIG, F32)
        acc_scr[...] = jnp.zeros(acc_scr.shape, F32)

        def step(j, bias=None, mask=None):
            s, vb = scores(j, bias, mask)
            m_prev = m_scr[...]
            m_next = jnp.maximum(m_prev, jnp.max(s, axis=1, keepdims=True))
            alpha = jnp.exp2(m_prev - m_next)
            p = jnp.exp2(s - jnp.concatenate([m_next] * (t // LANES), axis=1))
            pv = jnp.dot(p.astype(BF16), vb, preferred_element_type=F32)
            acc_scr[...] = jnp.concatenate([alpha, alpha], axis=1) * acc_scr[...] + pv
            m_scr[...] = m_next

        def far_body(j, carry):
            step(j)
            return carry

        lax.fori_loop(0, n_far, far_body, 0)

        @pl.when(i > 0)
        def _():
            step(i - 1, bias=bias_ref[0, 1])

        step(i, bias=bias_ref[0, 0], mask=chunk_mask)

    acc = acc_scr[...]
    o_all = acc[:, 0:LANES] / acc[:, LANES:2 * LANES]
    o = o_all[0:t] - lam_ref[0] * o_all[t:2 * t]
    ms = jnp.mean(o * o, axis=-1, keepdims=True)
    o_ref[0, rows_of_tile, :] = (o * lax.rsqrt(ms + EPS) * (g_ref[...] * out_scale)).astype(BF16)


def _diff_attention(proj3, lam, bias_range, bias_tiles, subln_row, out_scale):
    bsz, s_len, _ = proj3.shape
    t = DA_BLOCK
    tq = DA_TILES * t
    nq = s_len // tq
    kern = functools.partial(_da_kernel, out_scale=out_scale)
    return pl.pallas_call(
        kern,
        grid=(bsz, DA_HEADS, nq),
        in_specs=[
            pl.BlockSpec(memory_space=pltpu.SMEM),
            pl.BlockSpec(memory_space=pltpu.SMEM),
            pl.BlockSpec((1, tq, LANES), lambda b, h, i: (b, i, h)),
            pl.BlockSpec((1, s_len, LANES), lambda b, h, i: (b, 0, DA_HEADS + h)),
            pl.BlockSpec((1, s_len, LANES), lambda b, h, i: (b, 0, 2 * DA_HEADS + h)),
            pl.BlockSpec((1, 2, t, t), lambda b, h, i: (h, 0, 0, 0)),
            pl.BlockSpec((1, LANES), lambda b, h, i: (0, 0)),
        ],
        out_specs=pl.BlockSpec((1, tq, LANES), lambda b, h, i: (b, i, h)),
        out_shape=jax.ShapeDtypeStruct((bsz, s_len, DA_HEADS * LANES), BF16),
        scratch_shapes=[
            pltpu.VMEM((2 * t, LANES), F32),
            pltpu.VMEM((2 * t, 2 * LANES), F32),
            pltpu.SMEM((2,), F32),
            pltpu.SMEM((1,), jnp.int32),
            pltpu.VMEM((DA_UNROLL, 2 * t, t), BF16),
            pltpu.VMEM((DA_UNROLL, 2 * t, t), BF16),
        ],
        compiler_params=_cparams(("parallel", "parallel", "arbitrary")),
        name="diff_attention",
    )(lam, bias_range, proj3, proj3, proj3, bias_tiles, subln_row)


def _sb_tiles(tiles, tri):
    zs = [lax.dot_general(q, k, (((1,), (1,)), ((), ())), preferred_element_type=F32) for q, k, _, _, _ in tiles]
    mids = []
    for z, (_, _, _, causal, _) in zip(zs, tiles):
        log_1m = -(jnp.maximum(z, 0.0) + jnp.log(1.0 + jnp.exp(-jnp.abs(z))))
        lm = log_1m if causal is None else jnp.where(causal, log_1m, 0.0)
        inner = jnp.dot(lm.astype(BF16), tri, preferred_element_type=F32)
        mids.append((z + log_1m + inner, jnp.sum(lm, axis=1, keepdims=True)))
    outs = []
    for (logit, total), (_, _, v, causal, carry) in zip(mids, tiles):
        if carry is not None:
            logit = logit + carry
        a = jnp.exp(logit)
        if causal is not None:
            a = jnp.where(causal, a, 0.0)
        outs.append((jnp.dot(a.astype(BF16), v, preferred_element_type=F32), total))
    return outs


def _sb_kernel(q_ref, k_ref, v_ref, g_ref, tri_ref, o_ref, acc_scr, carry_scr):
    sub, win = SB_SUB, SB_WIN
    i = pl.program_id(2)
    lane = lax.broadcasted_iota(jnp.int32, (1, LANES), 1)
    tri = tri_ref[...]
    ahead = (lax.broadcasted_iota(jnp.int32, (sub, win), 1)
             - lax.broadcasted_iota(jnp.int32, (sub, win), 0))
    n_sub = SB_ROWS // sub

    tiles, starts = [], []
    for u in range(n_sub):
        r0 = i * SB_ROWS + u * sub
        start = pl.multiple_of(jnp.maximum(r0 - sub, 0), sub)
        qu = q_ref[0, u * sub:(u + 1) * sub, :]
        zero = jnp.zeros_like(qu)
        qs = jnp.concatenate([jnp.where(lane < SB_HEAD_DIM, qu, zero),
                              jnp.where(lane >= SB_HEAD_DIM, qu, zero)], axis=0)
        causal = ahead < r0 - start
        tiles.append((qs, k_ref[0, pl.ds(start, win), :], v_ref[0, pl.ds(start, win), :],
                      jnp.concatenate([causal, causal], axis=0), None))
        starts.append(start)
    chains = []
    for u, (pv, total) in enumerate(_sb_tiles(tiles, tri)):
        acc_scr[u] = pv
        carry_scr[u] = jnp.broadcast_to(total, (2 * sub, LANES))
        chains.append((u, tiles[u][0], starts[u] // sub - 1, jnp.max(total)))

    tri_sub = tri[0:sub, 0:sub]
    for u, qs, j0, alive0 in chains:
        def cond(state):
            j, alive = state
            return (j >= 0) & (alive > SB_DEAD)

        def body(state, u=u, qs=qs):
            j, _ = state
            s0 = pl.multiple_of(j * sub, sub)
            kb = k_ref[0, pl.ds(s0, sub), :]
            vb = v_ref[0, pl.ds(s0, sub), :]
            carry = carry_scr[u]
            (pv, total), = _sb_tiles([(qs, kb, vb, None, carry)], tri_sub)
            acc_scr[u] += pv
            carry = carry + total
            carry_scr[u] = carry
            return j - 1, jnp.max(carry)

        lax.while_loop(cond, body, (j0, alive0))

    for u in range(n_sub):
        acc = acc_scr[u]
        o = jnp.where(lane < SB_HEAD_DIM, acc[0:sub], acc[sub:2 * sub])
        sq = o * o
        ss0 = jnp.sum(jnp.where(lane < SB_HEAD_DIM, sq, 0.0), axis=1, keepdims=True)
        ss1 = jnp.sum(jnp.where(lane >= SB_HEAD_DIM, sq, 0.0), axis=1, keepdims=True)
        ms = jnp.where(lane < SB_HEAD_DIM, ss0, ss1) * (1.0 / SB_HEAD_DIM)
        o_ref[0, u * sub:(u + 1) * sub, :] = (o * lax.rsqrt(ms + EPS) * g_ref[...]).astype(BF16)


def _stick_breaking(proj3, sb_row, tri):
    bsz, s_len, _ = proj3.shape
    t = SB_ROWS
    nq = s_len // t
    pairs = SB_HEADS // 2
    q_blk = 3 * DA_HEADS
    n_chains = SB_ROWS // SB_SUB
    return pl.pallas_call(
        _sb_kernel,
        grid=(bsz, pairs, nq),
        in_specs=[
            pl.BlockSpec((1, t, LANES), lambda b, h, i: (b, i, q_blk + h)),
            pl.BlockSpec((1, s_len, LANES), lambda b, h, i: (b, 0, q_blk + pairs + h)),
            pl.BlockSpec((1, s_len, LANES), lambda b, h, i: (b, 0, q_blk + 2 * pairs + h)),
            pl.BlockSpec((1, LANES), lambda b, h, i: (0, 0)),
            pl.BlockSpec((SB_WIN, SB_WIN), lambda b, h, i: (0, 0)),
        ],
        out_specs=pl.BlockSpec((1, t, LANES), lambda b, h, i: (b, i, h)),
        out_shape=jax.ShapeDtypeStruct((bsz, s_len, pairs * LANES), BF16),
        scratch_shapes=[
            pltpu.VMEM((n_chains, 2 * SB_SUB, LANES), F32),
            pltpu.VMEM((n_chains, 2 * SB_SUB, LANES), F32),
        ],
        compiler_params=_cparams(("parallel", "parallel", "arbitrary")),
        name="stick_breaking",
    )(proj3, proj3, proj3, sb_row, tri)


def _out_proj_kernel(x_ref, oda_ref, osb_ref, wo_ref, g_ref, wr_ref, br_ref, upper_ref,
                     x1_ref, h2_ref, route_i_ref, route_f_ref, counts_ref, base_scr):
    half = oda_ref.shape[1]
    x1 = (x_ref[...]
          + jnp.dot(oda_ref[...], wo_ref[0:half, :], preferred_element_type=F32)
          + jnp.dot(osb_ref[...], wo_ref[half:2 * half, :], preferred_element_type=F32))
    x1_ref[...] = x1
    ms = jnp.mean(x1 * x1, axis=-1, keepdims=True)
    h2 = x1 * lax.rsqrt(ms + EPS) * g_ref[...]
    d_half = h2.shape[1] // 2
    hi_bits = lax.bitcast_convert_type(h2[:, :d_half].astype(BF16).astype(F32), jnp.uint32)
    lo_bits = lax.bitcast_convert_type(h2[:, d_half:].astype(BF16).astype(F32), jnp.uint32)
    h2_ref[...] = hi_bits | (lo_bits >> 16)

    a_hi = h2.astype(BF16)
    a_mid = (h2 - a_hi.astype(F32)).astype(BF16)
    nt = lambda w, a: lax.dot_general(w, a, (((1,), (1,)), ((), ())), preferred_element_type=F32)
    lt = nt(wr_ref[0], a_hi) + (nt(wr_ref[0], a_mid) + nt(wr_ref[1], a_hi)) + br_ref[...]
    n_rows, tm = lt.shape
    ridx = lax.broadcasted_iota(jnp.int32, (n_rows, tm), 0)
    big = jnp.int32(n_rows)
    is_g = ridx < N_GROUPS
    lg = jnp.where(is_g, lt, NEG_BIG)
    mg = jnp.max(lg, axis=0, keepdims=True)
    gsel = jnp.min(jnp.where(lg == mg, ridx, big), axis=0, keepdims=True)
    pg_sel = 1.0 / jnp.sum(jnp.where(is_g, jnp.exp(lg - mg), 0.0), axis=0, keepdims=True)
    lo_row = N_GROUPS + gsel * EXPERTS_PER_GROUP
    le = jnp.where(ridx >= lo_row, jnp.where(ridx < lo_row + EXPERTS_PER_GROUP, lt, NEG_BIG), NEG_BIG)
    l1 = jnp.max(le, axis=0, keepdims=True)
    i1 = jnp.min(jnp.where(le == l1, ridx, big), axis=0, keepdims=True)
    le2 = jnp.where(ridx == i1, NEG_BIG, le)
    l2 = jnp.max(le2, axis=0, keepdims=True)
    i2 = jnp.min(jnp.where(le2 == l2, ridx, big), axis=0, keepdims=True)
    e2 = jnp.exp(l2 - l1)
    g1 = pg_sel / (1.0 + e2)
    g2 = pg_sel * e2 / (1.0 + e2)

    @pl.when(pl.program_id(0) == 0)
    def _():
        base_scr[...] = jnp.zeros(base_scr.shape, F32)

    sel1 = ridx == i1
    sel2 = ridx == i2
    chosen = jnp.where(sel1, 1.0, jnp.where(sel2, 1.0, 0.0))
    before = jnp.dot(chosen.astype(BF16), upper_ref[...], preferred_element_type=F32)
    base = base_scr[...]
    before = before + jnp.concatenate([base] * (tm // LANES), axis=1)
    rank1 = jnp.sum(jnp.where(sel1, before, 0.0), axis=0, keepdims=True)
    rank2 = jnp.sum(jnp.where(sel2, before, 0.0), axis=0, keepdims=True)
    base = base + jnp.sum(chosen, axis=1, keepdims=True)
    base_scr[...] = base
    counts_ref[...] = base.astype(jnp.int32)

    r8 = lax.broadcasted_iota(jnp.int32, (8, tm), 0)
    route_i_ref[...] = jnp.where(r8 == 0, i1 - N_GROUPS, jnp.where(r8 == 1, i2 - N_GROUPS, jnp.where(
        r8 == 2, rank1.astype(jnp.int32), jnp.where(r8 == 3, rank2.astype(jnp.int32), 0))))
    route_f_ref[...] = jnp.where(r8 == 0, g1, jnp.where(r8 == 1, g2, 0.0))


def _out_proj(x2, o_da, o_sb, wo_bf, g_ffn, wr2, br, upper):
    n, d = x2.shape
    half = o_da.shape[1]
    tm = PROJ_ROWS
    r = ROUTE_ROWS
    return pl.pallas_call(
        _out_proj_kernel,
        grid=(n // tm,),
        in_specs=[
            pl.BlockSpec((tm, d), lambda i: (i, 0)),
            pl.BlockSpec((tm, half), lambda i: (i, 0)),
            pl.BlockSpec((tm, half), lambda i: (i, 0)),
            pl.BlockSpec((2 * half, d), lambda i: (0, 0)),
            pl.BlockSpec((1, d), lambda i: (0, 0)),
            pl.BlockSpec((2, r, d), lambda i: (0, 0, 0)),
            pl.BlockSpec((r, tm), lambda i: (0, 0)),
            pl.BlockSpec((tm, tm), lambda i: (0, 0)),
        ],
        out_specs=[
            pl.BlockSpec((tm, d), lambda i: (i, 0)),
            pl.BlockSpec((tm, d // 2), lambda i: (i, 0)),
            pl.BlockSpec((8, tm), lambda i: (0, i)),
            pl.BlockSpec((8, tm), lambda i: (0, i)),
            pl.BlockSpec((r, LANES), lambda i: (0, 0)),
        ],
        out_shape=[
            jax.ShapeDtypeStruct((n, d), F32),
            jax.ShapeDtypeStruct((n, d // 2), jnp.uint32),
            jax.ShapeDtypeStruct((8, n), jnp.int32),
            jax.ShapeDtypeStruct((8, n), F32),
            jax.ShapeDtypeStruct((r, LANES), jnp.int32),
        ],
        scratch_shapes=[pltpu.VMEM((r, LANES), F32)],
        compiler_params=_cparams(("arbitrary",)),
        name="out_proj_router",
    )(x2, o_da, o_sb, wo_bf, g_ffn, wr2, br, upper)


def _dispatch_kernel(dest_ref, h_ref, xs_in_ref, xs_ref, sem):
    del xs_in_ref
    tt = h_ref.shape[0]

    def copy(r, k):
        return pltpu.make_async_copy(h_ref.at[pl.ds(r, 1)], xs_ref.at[pl.ds(dest_ref[0, 0, 2 * r + k], 1)], sem)

    for r in range(tt):
        copy(r, 0).start(priority=0)
        copy(r, 1).start(priority=1)

    def drain(r, c):
        copy(r, 0).wait()
        copy(r, 1).wait()
        return c

    lax.fori_loop(0, tt, drain, 0, unroll=True)


def _dispatch(dest2, h2, xs_init):
    n, d = h2.shape
    tt = TOK_TILE
    return pl.pallas_call(
        _dispatch_kernel,
        grid=(n // tt,),
        in_specs=[
            pl.BlockSpec((1, 1, 2 * tt), lambda i: (i, 0, 0), memory_space=pltpu.SMEM),
            pl.BlockSpec((tt, d), lambda i: (i, 0)),
            pl.BlockSpec(memory_space=pl.ANY),
        ],
        out_specs=pl.BlockSpec(memory_space=pl.ANY),
        out_shape=jax.ShapeDtypeStruct(xs_init.shape, xs_init.dtype),
        scratch_shapes=[pltpu.SemaphoreType.DMA(())],
        input_output_aliases={2: 0},
        compiler_params=_cparams(("arbitrary",)),
        name="moe_dispatch",
    )(dest2, h2, xs_init)


def _expert_kernel(blk_e_ref, n_used_ref, xs_ref, w1_ref, w3_ref, w2_ref, ys_ref, w1_scr, w3_scr, w2_scr):
    i = pl.program_id(0)
    used = i < n_used_ref[0]

    @pl.when(used & ((i == 0) | (blk_e_ref[i] != blk_e_ref[jnp.maximum(i - 1, 0)])))
    def _():
        w1_scr[...] = w1_ref[0].astype(BF16)
        w3_scr[...] = w3_ref[0].astype(BF16)
        w2_scr[...] = w2_ref[0].astype(BF16)

    @pl.when(used)
    def _():
        rows = MOE_BLOCK // MOE_CHUNKS
        xbs = []
        for c in range(MOE_CHUNKS):
            words = xs_ref[c * rows:(c + 1) * rows, :]
            xbs.append(jnp.concatenate([
                lax.bitcast_convert_type(words & jnp.uint32(0xFFFF0000), F32).astype(BF16),
                lax.bitcast_convert_type(words << 16, F32).astype(BF16)], axis=1))
        gated = [(jnp.dot(xb, w1_scr[...], preferred_element_type=F32), jnp.dot(xb, w3_scr[...], preferred_element_type=F32))
                 for xb in xbs]
        hmids = [(a * jax.nn.sigmoid(a) * b).astype(BF16) for a, b in gated]
        for c, hmid in enumerate(hmids):
            ys_ref[c * rows:(c + 1) * rows, :] = jnp.dot(hmid, w2_scr[...], preferred_element_type=F32)

    @pl.when(jnp.logical_not(used))
    def _():
        ys_ref[...] = jnp.zeros(ys_ref.shape, F32)


def _experts(blk_e, n_used, xs, w1, w3, w2):
    cap = xs.shape[0]
    d, hid = w1.shape[1], w1.shape[2]
    nb = cap // MOE_BLOCK

    def row_map(i, blk_e_ref, n_used_ref):
        return (jnp.minimum(i, n_used_ref[0] - 1), 0)

    def w_map(i, blk_e_ref, n_used_ref):
        return (blk_e_ref[jnp.minimum(i, n_used_ref[0] - 1)], 0, 0)

    grid_spec = pltpu.PrefetchScalarGridSpec(
        num_scalar_prefetch=2,
        grid=(nb,),
        in_specs=[
            pl.BlockSpec((MOE_BLOCK, d // 2), row_map),
            pl.BlockSpec((1, d, hid), w_map),
            pl.BlockSpec((1, d, hid), w_map),
            pl.BlockSpec((1, hid, d), w_map),
        ],
        out_specs=pl.BlockSpec((MOE_BLOCK, d), lambda i, blk_e_ref, n_used_ref: (i, 0)),
        scratch_shapes=[pltpu.VMEM((d, hid), BF16), pltpu.VMEM((d, hid), BF16), pltpu.VMEM((hid, d), BF16)],
    )
    return pl.pallas_call(
        _expert_kernel,
        grid_spec=grid_spec,
        out_shape=jax.ShapeDtypeStruct((cap, d), F32),
        compiler_params=_cparams(("arbitrary",)),
        name="moe_experts",
    )(blk_e, n_used, xs, w1, w3, w2)


def _combine_kernel(dest_ref, x1_ref, gates_ref, ys_ref, o_ref, y0_scr, y1_scr, sem):
    tt = x1_ref.shape[0]

    def copy(r, k):
        dst = y0_scr if k == 0 else y1_scr
        return pltpu.make_async_copy(ys_ref.at[pl.ds(dest_ref[0, 0, 2 * r + k], 1)], dst.at[pl.ds(r, 1)], sem)

    for r in range(tt):
        copy(r, 0).start(priority=0)
        copy(r, 1).start(priority=1)

    def drain(r, c):
        copy(r, 0).wait()
        copy(r, 1).wait()
        return c

    lax.fori_loop(0, tt, drain, 0, unroll=True)
    g = gates_ref[...]
    o_ref[...] = x1_ref[...] + y0_scr[...] * g[:, 0:1] + y1_scr[...] * g[:, 1:2]


def _combine(dest2, x1, gates, ys):
    n, d = x1.shape
    tt = TOK_TILE
    return pl.pallas_call(
        _combine_kernel,
        grid=(n // tt,),
        in_specs=[
            pl.BlockSpec((1, 1, 2 * tt), lambda i: (i, 0, 0), memory_space=pltpu.SMEM),
            pl.BlockSpec((tt, d), lambda i: (i, 0)),
            pl.BlockSpec((tt, LANES), lambda i: (i, 0)),
            pl.BlockSpec(memory_space=pl.ANY),
        ],
        out_specs=pl.BlockSpec((tt, d), lambda i: (i, 0)),
        out_shape=jax.ShapeDtypeStruct((n, d), F32),
        scratch_shapes=[
            pltpu.VMEM((tt, d), F32),
            pltpu.VMEM((tt, d), F32),
            pltpu.SemaphoreType.DMA(()),
        ],
        compiler_params=_cparams(("arbitrary",)),
        name="moe_combine",
    )(dest2, x1, gates, ys)


def _rel_bucket_np(rel):
    nb = REL_BUCKETS // 2
    max_exact = nb // 2
    base = np.where(rel > 0, nb, 0)
    n = np.abs(rel)
    nf = np.maximum(n, 1).astype(np.float64)
    large = max_exact + (np.log(nf / max_exact) / math.log(REL_MAX_DIST / max_exact) * (nb - max_exact)).astype(np.int64)
    large = np.minimum(large, nb - 1)
    return (base + np.where(n < max_exact, n, large)).astype(np.int32)


def _bias_bucket_rows(t):
    far = _rel_bucket_np(-np.arange(t + 1, 4 * t))
    far_bucket = int(far[0])
    assert (far == far_bucket).all()
    m = np.arange(2 * t)
    wrap = np.where(m < t, m, m - 2 * t)
    return np.stack([_rel_bucket_np(wrap), _rel_bucket_np(wrap - t)]), far_bucket


def _toeplitz(rows, t):
    lead = rows.shape[:-1]
    flat = jnp.tile(rows, (1,) * len(lead) + (t,))[..., : t * (2 * t - 1)]
    return flat.reshape(lead + (t, 2 * t - 1))[..., :t]


def kernel(x, g_attn, w_in, qn_g, kn_g, lam_q1, lam_k1, lam_q2, lam_k2, subln_g, sb_out_g, rel_bias, w_o,
           g_ffn, w_router_g, b_router_g, w_router_e, b_router_e, w1, w3, w2):
    bsz, s_len, d = x.shape
    n = bsz * s_len
    depth = g_attn.shape[0]
    da_width = DA_HEADS * 2 * DA_HEAD_DIM
    sb_width = SB_HEADS * SB_HEAD_DIM

    bd = jnp.asarray(np.kron(np.eye(256 // DA_HEAD_DIM), np.ones((DA_HEAD_DIM, DA_HEAD_DIM))), BF16)
    tri = jnp.asarray(np.tril(np.ones((SB_WIN, SB_WIN)), -1), BF16)
    bucket_rows, far_bucket = _bias_bucket_rows(DA_BLOCK)
    upper = jnp.asarray(np.triu(np.ones((PROJ_ROWS, PROJ_ROWS)), 1), BF16)

    for l in range(depth):
        lambda_init = 0.8 - 0.6 * math.exp(-0.3 * l)
        x2 = x.reshape(n, d)

        q_scale = DA_HEAD_DIM ** -0.5 * LOG2E
        gain_row = jnp.concatenate([
            jnp.tile(qn_g[l] * q_scale, 2 * DA_HEADS),
            jnp.tile(kn_g[l], 2 * DA_HEADS),
            jnp.ones((da_width,), F32),
            jnp.full((sb_width,), SB_HEAD_DIM ** -0.5, F32),
            jnp.ones((2 * sb_width,), F32),
        ])[None, :]
        proj = _in_proj(x2, g_attn[l][None, :], w_in[l].astype(BF16), gain_row, bd)
        proj3 = proj.reshape(bsz, s_len, -1)

        lam = (jnp.exp(jnp.sum(lam_q1[l] * lam_k1[l])) - jnp.exp(jnp.sum(lam_q2[l] * lam_k2[l]))
               + lambda_init).astype(F32).reshape(1)
        rb = (rel_bias - rel_bias[far_bucket][None, :]) * LOG2E
        bias_rows = jnp.transpose(rb[bucket_rows], (2, 0, 1)).astype(F32)
        bias_tiles = _toeplitz(bias_rows, DA_BLOCK)
        q_reach = jnp.full((DA_HEADS,), DA_HEAD_DIM ** 0.5 * 1.02) * jnp.max(jnp.abs(qn_g[l] * q_scale))
        bias_range = jnp.stack([jnp.maximum(jnp.max(rb, axis=0), 0.0), jnp.minimum(jnp.min(rb, axis=0), 0.0),
                                q_reach])
        o_da = _diff_attention(proj3, lam, bias_range.astype(F32), bias_tiles, subln_g[l][None, :],
                               1.0 - lambda_init)
        o_sb = _stick_breaking(proj3, jnp.tile(sb_out_g[l], 2)[None, :], tri)

        wr = jnp.concatenate([w_router_g[l], w_router_e[l],
                              jnp.zeros((d, ROUTE_ROWS - N_GROUPS - N_EXPERTS), F32)], axis=1).T
        wr_hi = wr.astype(BF16)
        wr_mid = (wr - wr_hi.astype(F32)).astype(BF16)
        br = jnp.concatenate([b_router_g[l], b_router_e[l],
                              jnp.zeros((ROUTE_ROWS - N_GROUPS - N_EXPERTS,), F32)])
        br = jnp.broadcast_to(br[:, None], (ROUTE_ROWS, PROJ_ROWS))
        x1, h2, route_i, route_f, counts_all = _out_proj(
            x2, o_da.reshape(n, -1), o_sb.reshape(n, -1), w_o[l].astype(BF16), g_ffn[l][None, :],
            jnp.stack([wr_hi, wr_mid]), br, upper)

        counts = counts_all[N_GROUPS:N_GROUPS + N_EXPERTS, 0]
        padded = (counts + MOE_BLOCK - 1) // MOE_BLOCK * MOE_BLOCK
        pad_ends = jnp.cumsum(padded)
        pad_starts = pad_ends - padded
        experts = jnp.arange(N_EXPERTS, dtype=jnp.int32)
        eid, rank = route_i[0:2], route_i[2:4]
        start_of = jnp.sum(jnp.where(eid[:, :, None] == experts, pad_starts, 0), axis=-1)
        dest2 = (start_of + rank).T.reshape(n // TOK_TILE, 1, 2 * TOK_TILE)
        gates = jnp.pad(route_f[0:2].T, ((0, 0), (0, LANES - 2)))
        cap = 2 * n + N_EXPERTS * MOE_BLOCK
        nb = cap // MOE_BLOCK
        blk_start = jnp.arange(nb, dtype=jnp.int32) * MOE_BLOCK
        blk_e = jnp.minimum(jnp.sum(pad_ends[None, :] <= blk_start[:, None], axis=1), N_EXPERTS - 1).astype(jnp.int32)
        n_used = (pad_ends[-1] // MOE_BLOCK).astype(jnp.int32).reshape(1)

        xs = _dispatch(dest2, h2, jnp.zeros((cap, d // 2), jnp.uint32))
        ys = _experts(blk_e, n_used, xs, w1[l], w3[l], w2[l])
        x = _combine(dest2, x1, gates, ys).reshape(bsz, s_len, d)
    return x
```

```python
import functools
import math

import jax
import jax.numpy as jnp
import numpy as np
from jax import lax
from jax.experimental import pallas as pl
from jax.experimental.pallas import tpu as pltpu

F32 = jnp.float32
BF16 = jnp.bfloat16

EPS = 1e-6
LANES = 128
DA_HEADS = 4
DA_HEAD_DIM = 64
SB_HEADS = 8
SB_HEAD_DIM = 64
CHUNK = 64
REL_BUCKETS = 32
REL_MAX_DIST = 128
N_GROUPS = 4
EXPERTS_PER_GROUP = 8
N_EXPERTS = N_GROUPS * EXPERTS_PER_GROUP
NEG_BIG = -1e30
LOG2E = math.log2(math.e)
SB_DEAD = -87.5

VMEM_LIMIT = 48 * 1024 * 1024

PROJ_ROWS = 1024
ROUTE_ROWS = 48
DA_BLOCK = 256
DA_TILES = 4
DA_UNROLL = 4
DA_GROUPS_PER_ITER = 4
DA_SAFE_RANGE = 100.0
SB_SUB = 128
SB_ROWS = 1024
SB_WIN = 256
MOE_BLOCK = 512
MOE_CHUNKS = 2
TOK_TILE = 1024


def _cparams(sem):
    return pltpu.CompilerParams(dimension_semantics=sem, vmem_limit_bytes=VMEM_LIMIT)


def _in_proj_kernel(x_ref, g_ref, w_ref, gain_ref, bd_ref, o_ref):
    x = x_ref[...]
    ms = jnp.mean(x * x, axis=-1, keepdims=True)
    h = (x * lax.rsqrt(ms + EPS) * g_ref[...]).astype(BF16)
    n_chunks = o_ref.shape[1] // 512
    for c in range(n_chunks):
        cols = slice(c * 512, (c + 1) * 512)
        acc = jnp.dot(h, w_ref[:, cols], preferred_element_type=F32)
        if c < 2:
            sq = (acc * acc).astype(BF16)
            parts = []
            for s in range(2):
                ss = jnp.dot(sq[:, s * 256:(s + 1) * 256], bd_ref[...], preferred_element_type=F32)
                parts.append(ss)
            ss = jnp.concatenate(parts, axis=1)
            acc = acc * lax.rsqrt(ss * (1.0 / DA_HEAD_DIM) + EPS)
        o_ref[:, cols] = (acc * gain_ref[:, cols]).astype(BF16)


def _in_proj(x2, g_attn, w_in_bf, gain_row, bd):
    n, d = x2.shape
    width = w_in_bf.shape[1]
    return pl.pallas_call(
        _in_proj_kernel,
        grid=(n // PROJ_ROWS,),
        in_specs=[
            pl.BlockSpec((PROJ_ROWS, d), lambda i: (i, 0)),
            pl.BlockSpec((1, d), lambda i: (0, 0)),
            pl.BlockSpec((d, width), lambda i: (0, 0)),
            pl.BlockSpec((1, width), lambda i: (0, 0)),
            pl.BlockSpec((256, 256), lambda i: (0, 0)),
        ],
        out_specs=pl.BlockSpec((PROJ_ROWS, width), lambda i: (i, 0)),
        out_shape=jax.ShapeDtypeStruct((n, width), BF16),
        compiler_params=_cparams(("parallel",)),
        name="in_proj",
    )(x2, g_attn, w_in_bf, gain_row, bd)


def _da_kernel(*refs, out_scale):
    def tile_body(sub, carry):
        _da_tile(pl.program_id(2) * DA_TILES + sub, sub, *refs, out_scale=out_scale)
        return carry

    lax.fori_loop(0, DA_TILES, tile_body, 0)


def _da_tile(i, sub, lam_ref, brange_ref, q_ref, k_ref, v_ref, bias_ref, g_ref, o_ref, m_scr, acc_scr, stab_scr,
             fits_scr, pa_scr, pb_scr, *, out_scale):
    t = DA_BLOCK
    h = pl.program_id(1)
    s_len = k_ref.shape[1]
    rows_of_tile = pl.ds(pl.multiple_of(sub * t, t), t)
    ones_col = jnp.ones((LANES, LANES), BF16)

    @pl.when(i == 0)
    def _():
        lane_k = lax.broadcasted_iota(jnp.int32, (1, LANES), 1)
        rows = 512

        def kbody(c, best):
            kc = k_ref[0, pl.ds(pl.multiple_of(c * rows, rows), rows), :].astype(F32)
            sq = kc * kc
            n1 = jnp.dot(jnp.where(lane_k < DA_HEAD_DIM, sq, 0.0).astype(BF16), ones_col, preferred_element_type=F32)
            n2 = jnp.dot(jnp.where(lane_k >= DA_HEAD_DIM, sq, 0.0).astype(BF16), ones_col, preferred_element_type=F32)
            return (jnp.maximum(best[0], jnp.max(n1, axis=0, keepdims=True)),
                    jnp.maximum(best[1], jnp.max(n2, axis=0, keepdims=True)))

        zero_row = jnp.zeros((1, LANES), F32)
        k1, k2 = lax.fori_loop(0, s_len // rows, kbody, (zero_row, zero_row))
        reach1 = jnp.max(jnp.sqrt(k1)) * (brange_ref[2, h] * 1.02)
        reach2 = jnp.max(jnp.sqrt(k2)) * (brange_ref[2, h] * 1.02)
        stab_scr[0] = reach1 + brange_ref[0, h]
        stab_scr[1] = reach2 + brange_ref[0, h]
        spread = 2.0 * jnp.maximum(reach1, reach2) + (brange_ref[0, h] - brange_ref[1, h])
        fits_scr[0] = (spread <= DA_SAFE_RANGE).astype(jnp.int32)

    q = q_ref[0, rows_of_tile, :]
    lane = lax.broadcasted_iota(jnp.int32, (1, LANES), 1)
    zero = jnp.zeros_like(q)
    qq = jnp.concatenate([jnp.where(lane < DA_HEAD_DIM, q, zero),
                          jnp.where(lane >= DA_HEAD_DIM, q, zero)], axis=0)
    ones = jnp.ones((t, LANES), BF16)
    row = lax.broadcasted_iota(jnp.int32, (t, t), 0)
    col = lax.broadcasted_iota(jnp.int32, (t, t), 1)
    chunk_mask = (col // CHUNK) <= (row // CHUNK)
    n_far = jnp.maximum(i - 1, 0)

    def values(j):
        return jnp.concatenate([v_ref[0, pl.ds(pl.multiple_of(j * t, t), t), :], ones], axis=1)

    def scores(j, bias, mask):
        kb = k_ref[0, pl.ds(pl.multiple_of(j * t, t), t), :]
        s = lax.dot_general(qq, kb, (((1,), (1,)), ((), ())), preferred_element_type=F32)
        if bias is not None:
            s = s + jnp.concatenate([bias, bias], axis=0)
        if mask is not None:
            s = jnp.where(jnp.concatenate([mask, mask], axis=0) if mask.ndim else mask, s, NEG_BIG)
        return s, values(j)

    fits = fits_scr[0] != 0

    @pl.when(fits)
    def _():
        stab = jnp.concatenate([jnp.full((t, t), stab_scr[0], F32), jnp.full((t, t), stab_scr[1], F32)], axis=0)

        def probs_into(g, p_scr):
            for u in range(DA_UNROLL):
                s, _ = scores(g * DA_UNROLL + u, None, None)
                p_scr[u] = jnp.exp2(s - stab).astype(BF16)

        def weighted_from(g, p_scr):
            tot = jnp.dot(p_scr[0], values(g * DA_UNROLL), preferred_element_type=F32)
            for u in range(1, DA_UNROLL):
                tot = tot + jnp.dot(p_scr[u], values(g * DA_UNROLL + u), preferred_element_type=F32)
            acc_scr[...] += tot

        n_grp = n_far // DA_UNROLL

        left = n_far - n_grp * DA_UNROLL

        def head_region(n_left):
            tail = [(n_grp * DA_UNROLL + u, None, None) for u in range(n_left)]
            tail += [(n_far, bias_ref[0, 1], i > 0), (i, bias_ref[0, 0], chunk_mask)]
            tail_scores = [scores(j, bias, mask) for j, bias, mask in tail]
            probs_into(0, pa_scr)
            tot = None
            for s, vb in tail_scores:
                pv = jnp.dot(jnp.exp2(s - stab).astype(BF16), vb, preferred_element_type=F32)
                tot = pv if tot is None else tot + pv
            acc_scr[...] = tot

        for n_left in range(DA_UNROLL):
            pl.when(left == n_left)(functools.partial(head_region, n_left))

        bufs = (pa_scr, pb_scr)

        def drain(g0, count, produce_last):
            for c in range(count):
                weighted_from(g0 + c, bufs[c % 2])
                if c + 1 < count or produce_last:
                    probs_into(g0 + c + 1, bufs[(c + 1) % 2])

        def iter_body(it, carry):
            drain(it * DA_GROUPS_PER_ITER, DA_GROUPS_PER_ITER, True)
            return carry

        n_it = jnp.maximum(n_grp - 1, 0) // DA_GROUPS_PER_ITER
        lax.fori_loop(0, n_it, iter_body, 0)
        rest = n_grp - n_it * DA_GROUPS_PER_ITER
        for count in range(1, DA_GROUPS_PER_ITER + 1):
            pl.when(rest == count)(functools.partial(drain, n_it * DA_GROUPS_PER_ITER, count, False))

    @pl.when(jnp.logical_not(fits))
    def _():
        m_scr[...] = jnp.full(m_scr.shape, NEG_BIG, F32)
        acc_scr[...] = jnp.zeros(acc_scr.shape, F32)

        def step(j, bias=None, mask=None):
            s, vb = scores(j, bias, mask)
            m_prev = m_scr[...]
            m_next = jnp.maximum(m_prev, jnp.max(s, axis=1, keepdims=True))
            alpha = jnp.exp2(m_prev - m_next)
            p = jnp.exp2(s - jnp.concatenate([m_next] * (t // LANES), axis=1))
            pv = jnp.dot(p.astype(BF16), vb, preferred_element_type=F32)
            acc_scr[...] = jnp.concatenate([alpha, alpha], axis=1) * acc_scr[...] + pv
            m_scr[...] = m_next

        def far_body(j, carry):
            step(j)
            return carry

        lax.fori_loop(0, n_far, far_body, 0)

        @pl.when(i > 0)
        def _():
            step(i - 1, bias=bias_ref[0, 1])

        step(i, bias=bias_ref[0, 0], mask=chunk_mask)

    acc = acc_scr[...]
    o_all = acc[:, 0:LANES] / acc[:, LANES:2 * LANES]
    o = o_all[0:t] - lam_ref[0] * o_all[t:2 * t]
    ms = jnp.mean(o * o, axis=-1, keepdims=True)
    o_ref[0, rows_of_tile, :] = (o * lax.rsqrt(ms + EPS) * (g_ref[...] * out_scale)).astype(BF16)


def _diff_attention(proj3, lam, bias_range, bias_tiles, subln_row, out_scale):
    bsz, s_len, _ = proj3.shape
    t = DA_BLOCK
    tq = DA_TILES * t
    nq = s_len // tq
    kern = functools.partial(_da_kernel, out_scale=out_scale)
    return pl.pallas_call(
        kern,
        grid=(bsz, DA_HEADS, nq),
        in_specs=[
            pl.BlockSpec(memory_space=pltpu.SMEM),
            pl.BlockSpec(memory_space=pltpu.SMEM),
            pl.BlockSpec((1, tq, LANES), lambda b, h, i: (b, i, h)),
            pl.BlockSpec((1, s_len, LANES), lambda b, h, i: (b, 0, DA_HEADS + h)),
            pl.BlockSpec((1, s_len, LANES), lambda b, h, i: (b, 0, 2 * DA_HEADS + h)),
            pl.BlockSpec((1, 2, t, t), lambda b, h, i: (h, 0, 0, 0)),
            pl.BlockSpec((1, LANES), lambda b, h, i: (0, 0)),
        ],
        out_specs=pl.BlockSpec((1, tq, LANES), lambda b, h, i: (b, i, h)),
        out_shape=jax.ShapeDtypeStruct((bsz, s_len, DA_HEADS * LANES), BF16),
        scratch_shapes=[
            pltpu.VMEM((2 * t, LANES), F32),
            pltpu.VMEM((2 * t, 2 * LANES), F32),
            pltpu.SMEM((2,), F32),
            pltpu.SMEM((1,), jnp.int32),
            pltpu.VMEM((DA_UNROLL, 2 * t, t), BF16),
            pltpu.VMEM((DA_UNROLL, 2 * t, t), BF16),
        ],
        compiler_params=_cparams(("parallel", "parallel", "arbitrary")),
        name="diff_attention",
    )(lam, bias_range, proj3, proj3, proj3, bias_tiles, subln_row)


def _sb_tiles(tiles, tri):
    zs = [lax.dot_general(q, k, (((1,), (1,)), ((), ())), preferred_element_type=F32) for q, k, _, _, _ in tiles]
    mids = []
    for z, (_, _, _, causal, _) in zip(zs, tiles):
        log_1m = -(jnp.maximum(z, 0.0) + jnp.log(1.0 + jnp.exp(-jnp.abs(z))))
        lm = log_1m if causal is None else jnp.where(causal, log_1m, 0.0)
        inner = jnp.dot(lm.astype(BF16), tri, preferred_element_type=F32)
        mids.append((z + log_1m + inner, jnp.sum(lm, axis=1, keepdims=True)))
    outs = []
    for (logit, total), (_, _, v, causal, carry) in zip(mids, tiles):
        if carry is not None:
            logit = logit + carry
        a = jnp.exp(logit)
        if causal is not None:
            a = jnp.where(causal, a, 0.0)
        outs.append((jnp.dot(a.astype(BF16), v, preferred_element_type=F32), total))
    return outs


def _sb_kernel(q_ref, k_ref, v_ref, g_ref, tri_ref, o_ref, acc_scr, carry_scr):
    sub, win = SB_SUB, SB_WIN
    i = pl.program_id(2)
    lane = lax.broadcasted_iota(jnp.int32, (1, LANES), 1)
    tri = tri_ref[...]
    ahead = (lax.broadcasted_iota(jnp.int32, (sub, win), 1)
             - lax.broadcasted_iota(jnp.int32, (sub, win), 0))
    n_sub = SB_ROWS // sub

    tiles, starts = [], []
    for u in range(n_sub):
        r0 = i * SB_ROWS + u * sub
        start = pl.multiple_of(jnp.maximum(r0 - sub, 0), sub)
        qu = q_ref[0, u * sub:(u + 1) * sub, :]
        zero = jnp.zeros_like(qu)
        qs = jnp.concatenate([jnp.where(lane < SB_HEAD_DIM, qu, zero),
                              jnp.where(lane >= SB_HEAD_DIM, qu, zero)], axis=0)
        causal = ahead < r0 - start
        tiles.append((qs, k_ref[0, pl.ds(start, win), :], v_ref[0, pl.ds(start, win), :],
                      jnp.concatenate([causal, causal], axis=0), None))
        starts.append(start)
    chains = []
    for u, (pv, total) in enumerate(_sb_tiles(tiles, tri)):
        acc_scr[u] = pv
        carry_scr[u] = jnp.broadcast_to(total, (2 * sub, LANES))
        chains.append((u, tiles[u][0], starts[u] // sub - 1, jnp.max(total)))

    tri_sub = tri[0:sub, 0:sub]
    for u, qs, j0, alive0 in chains:
        def cond(state):
            j, alive = state
            return (j >= 0) & (alive > SB_DEAD)

        def body(state, u=u, qs=qs):
            j, _ = state
            s0 = pl.multiple_of(j * sub, sub)
            kb = k_ref[0, pl.ds(s0, sub), :]
            vb = v_ref[0, pl.ds(s0, sub), :]
            carry = carry_scr[u]
            (pv, total), = _sb_tiles([(qs, kb, vb, None, carry)], tri_sub)
            acc_scr[u] += pv
            carry = carry + total
            carry_scr[u] = carry
            return j - 1, jnp.max(carry)

        lax.while_loop(cond, body, (j0, alive0))

    for u in range(n_sub):
        acc = acc_scr[u]
        o = jnp.where(lane < SB_HEAD_DIM, acc[0:sub], acc[sub:2 * sub])
        sq = o * o
        ss0 = jnp.sum(jnp.where(lane < SB_HEAD_DIM, sq, 0.0), axis=1, keepdims=True)
        ss1 = jnp.sum(jnp.where(lane >= SB_HEAD_DIM, sq, 0.0), axis=1, keepdims=True)
        ms = jnp.where(lane < SB_HEAD_DIM, ss0, ss1) * (1.0 / SB_HEAD_DIM)
        o_ref[0, u * sub:(u + 1) * sub, :] = (o * lax.rsqrt(ms + EPS) * g_ref[...]).astype(BF16)


def _stick_breaking(proj3, sb_row, tri):
    bsz, s_len, _ = proj3.shape
    t = SB_ROWS
    nq = s_len // t
    pairs = SB_HEADS // 2
    q_blk = 3 * DA_HEADS
    n_chains = SB_ROWS // SB_SUB
    return pl.pallas_call(
        _sb_kernel,
        grid=(bsz, pairs, nq),
        in_specs=[
            pl.BlockSpec((1, t, LANES), lambda b, h, i: (b, i, q_blk + h)),
            pl.BlockSpec((1, s_len, LANES), lambda b, h, i: (b, 0, q_blk + pairs + h)),
            pl.BlockSpec((1, s_len, LANES), lambda b, h, i: (b, 0, q_blk + 2 * pairs + h)),
            pl.BlockSpec((1, LANES), lambda b, h, i: (0, 0)),
            pl.BlockSpec((SB_WIN, SB_WIN), lambda b, h, i: (0, 0)),
        ],
        out_specs=pl.BlockSpec((1, t, LANES), lambda b, h, i: (b, i, h)),
        out_shape=jax.ShapeDtypeStruct((bsz, s_len, pairs * LANES), BF16),
        scratch_shapes=[
            pltpu.VMEM((n_chains, 2 * SB_SUB, LANES), F32),
            pltpu.VMEM((n_chains, 2 * SB_SUB, LANES), F32),
        ],
        compiler_params=_cparams(("parallel", "parallel", "arbitrary")),
        name="stick_breaking",
    )(proj3, proj3, proj3, sb_row, tri)


def _out_proj_kernel(x_ref, oda_ref, osb_ref, wo_ref, g_ref, wr_ref, br_ref, upper_ref,
                     x1_ref, h2_ref, route_i_ref, route_f_ref, counts_ref, base_scr):
    half = oda_ref.shape[1]
    x1 = (x_ref[...]
          + jnp.dot(oda_ref[...], wo_ref[0:half, :], preferred_element_type=F32)
          + jnp.dot(osb_ref[...], wo_ref[half:2 * half, :], preferred_element_type=F32))
    x1_ref[...] = x1
    ms = jnp.mean(x1 * x1, axis=-1, keepdims=True)
    h2 = x1 * lax.rsqrt(ms + EPS) * g_ref[...]
    d_half = h2.shape[1] // 2
    hi_bits = lax.bitcast_convert_type(h2[:, :d_half].astype(BF16).astype(F32), jnp.uint32)
    lo_bits = lax.bitcast_convert_type(h2[:, d_half:].astype(BF16).astype(F32), jnp.uint32)
    h2_ref[...] = hi_bits | (lo_bits >> 16)

    a_hi = h2.astype(BF16)
    a_mid = (h2 - a_hi.astype(F32)).astype(BF16)
    nt = lambda w, a: lax.dot_general(w, a, (((1,), (1,)), ((), ())), preferred_element_type=F32)
    lt = nt(wr_ref[0], a_hi) + (nt(wr_ref[0], a_mid) + nt(wr_ref[1], a_hi)) + br_ref[...]
    n_rows, tm = lt.shape
    ridx = lax.broadcasted_iota(jnp.int32, (n_rows, tm), 0)
    big = jnp.int32(n_rows)
    is_g = ridx < N_GROUPS
    lg = jnp.where(is_g, lt, NEG_BIG)
    mg = jnp.max(lg, axis=0, keepdims=True)
    gsel = jnp.min(jnp.where(lg == mg, ridx, big), axis=0, keepdims=True)
    pg_sel = 1.0 / jnp.sum(jnp.where(is_g, jnp.exp(lg - mg), 0.0), axis=0, keepdims=True)
    lo_row = N_GROUPS + gsel * EXPERTS_PER_GROUP
    le = jnp.where(ridx >= lo_row, jnp.where(ridx < lo_row + EXPERTS_PER_GROUP, lt, NEG_BIG), NEG_BIG)
    l1 = jnp.max(le, axis=0, keepdims=True)
    i1 = jnp.min(jnp.where(le == l1, ridx, big), axis=0, keepdims=True)
    le2 = jnp.where(ridx == i1, NEG_BIG, le)
    l2 = jnp.max(le2, axis=0, keepdims=True)
    i2 = jnp.min(jnp.where(le2 == l2, ridx, big), axis=0, keepdims=True)
    e2 = jnp.exp(l2 - l1)
    g1 = pg_sel / (1.0 + e2)
    g2 = pg_sel * e2 / (1.0 + e2)

    @pl.when(pl.program_id(0) == 0)
    def _():
        base_scr[...] = jnp.zeros(base_scr.shape, F32)

    sel1 = ridx == i1
    sel2 = ridx == i2
    chosen = jnp.where(sel1, 1.0, jnp.where(sel2, 1.0, 0.0))
    before = jnp.dot(chosen.astype(BF16), upper_ref[...], preferred_element_type=F32)
    base = base_scr[...]
    before = before + jnp.concatenate([base] * (tm // LANES), axis=1)
    rank1 = jnp.sum(jnp.where(sel1, before, 0.0), axis=0, keepdims=True)
    rank2 = jnp.sum(jnp.where(sel2, before, 0.0), axis=0, keepdims=True)
    base = base + jnp.sum(chosen, axis=1, keepdims=True)
    base_scr[...] = base
    counts_ref[...] = base.astype(jnp.int32)

    r8 = lax.broadcasted_iota(jnp.int32, (8, tm), 0)
    route_i_ref[...] = jnp.where(r8 == 0, i1 - N_GROUPS, jnp.where(r8 == 1, i2 - N_GROUPS, jnp.where(
        r8 == 2, rank1.astype(jnp.int32), jnp.where(r8 == 3, rank2.astype(jnp.int32), 0))))
    route_f_ref[...] = jnp.where(r8 == 0, g1, jnp.where(r8 == 1, g2, 0.0))


def _out_proj(x2, o_da, o_sb, wo_bf, g_ffn, wr2, br, upper):
    n, d = x2.shape
    half = o_da.shape[1]
    tm = PROJ_ROWS
    r = ROUTE_ROWS
    return pl.pallas_call(
        _out_proj_kernel,
        grid=(n // tm,),
        in_specs=[
            pl.BlockSpec((tm, d), lambda i: (i, 0)),
            pl.BlockSpec((tm, half), lambda i: (i, 0)),
            pl.BlockSpec((tm, half), lambda i: (i, 0)),
            pl.BlockSpec((2 * half, d), lambda i: (0, 0)),
            pl.BlockSpec((1, d), lambda i: (0, 0)),
            pl.BlockSpec((2, r, d), lambda i: (0, 0, 0)),
            pl.BlockSpec((r, tm), lambda i: (0, 0)),
            pl.BlockSpec((tm, tm), lambda i: (0, 0)),
        ],
        out_specs=[
            pl.BlockSpec((tm, d), lambda i: (i, 0)),
            pl.BlockSpec((tm, d // 2), lambda i: (i, 0)),
            pl.BlockSpec((8, tm), lambda i: (0, i)),
            pl.BlockSpec((8, tm), lambda i: (0, i)),
            pl.BlockSpec((r, LANES), lambda i: (0, 0)),
        ],
        out_shape=[
            jax.ShapeDtypeStruct((n, d), F32),
            jax.ShapeDtypeStruct((n, d // 2), jnp.uint32),
            jax.ShapeDtypeStruct((8, n), jnp.int32),
            jax.ShapeDtypeStruct((8, n), F32),
            jax.ShapeDtypeStruct((r, LANES), jnp.int32),
        ],
        scratch_shapes=[pltpu.VMEM((r, LANES), F32)],
        compiler_params=_cparams(("arbitrary",)),
        name="out_proj_router",
    )(x2, o_da, o_sb, wo_bf, g_ffn, wr2, br, upper)


def _dispatch_kernel(dest_ref, h_ref, xs_in_ref, xs_ref, sem):
    del xs_in_ref
    tt = h_ref.shape[0]

    def copy(r, k):
        return pltpu.make_async_copy(h_ref.at[pl.ds(r, 1)], xs_ref.at[pl.ds(dest_ref[0, 0, 2 * r + k], 1)], sem)

    for r in range(tt):
        copy(r, 0).start(priority=0)
        copy(r, 1).start(priority=1)

    def drain(r, c):
        copy(r, 0).wait()
        copy(r, 1).wait()
        return c

    lax.fori_loop(0, tt, drain, 0, unroll=True)


def _dispatch(dest2, h2, xs_init):
    n, d = h2.shape
    tt = TOK_TILE
    return pl.pallas_call(
        _dispatch_kernel,
        grid=(n // tt,),
        in_specs=[
            pl.BlockSpec((1, 1, 2 * tt), lambda i: (i, 0, 0), memory_space=pltpu.SMEM),
            pl.BlockSpec((tt, d), lambda i: (i, 0)),
            pl.BlockSpec(memory_space=pl.ANY),
        ],
        out_specs=pl.BlockSpec(memory_space=pl.ANY),
        out_shape=jax.ShapeDtypeStruct(xs_init.shape, xs_init.dtype),
        scratch_shapes=[pltpu.SemaphoreType.DMA(())],
        input_output_aliases={2: 0},
        compiler_params=_cparams(("arbitrary",)),
        name="moe_dispatch",
    )(dest2, h2, xs_init)


def _expert_kernel(blk_e_ref, n_used_ref, xs_ref, w1_ref, w3_ref, w2_ref, ys_ref, w1_scr, w3_scr, w2_scr):
    i = pl.program_id(0)
    used = i < n_used_ref[0]

    @pl.when(used & ((i == 0) | (blk_e_ref[i] != blk_e_ref[jnp.maximum(i - 1, 0)])))
    def _():
        w1_scr[...] = w1_ref[0].astype(BF16)
        w3_scr[...] = w3_ref[0].astype(BF16)
        w2_scr[...] = w2_ref[0].astype(BF16)

    @pl.when(used)
    def _():
        rows = MOE_BLOCK // MOE_CHUNKS
        xbs = []
        for c in range(MOE_CHUNKS):
            words = xs_ref[c * rows:(c + 1) * rows, :]
            xbs.append(jnp.concatenate([
                lax.bitcast_convert_type(words & jnp.uint32(0xFFFF0000), F32).astype(BF16),
                lax.bitcast_convert_type(words << 16, F32).astype(BF16)], axis=1))
        gated = [(jnp.dot(xb, w1_scr[...], preferred_element_type=F32), jnp.dot(xb, w3_scr[...], preferred_element_type=F32))
                 for xb in xbs]
        hmids = [(a * jax.nn.sigmoid(a) * b).astype(BF16) for a, b in gated]
        for c, hmid in enumerate(hmids):
            ys_ref[c * rows:(c + 1) * rows, :] = jnp.dot(hmid, w2_scr[...], preferred_element_type=F32)

    @pl.when(jnp.logical_not(used))
    def _():
        ys_ref[...] = jnp.zeros(ys_ref.shape, F32)


def _experts(blk_e, n_used, xs, w1, w3, w2):
    cap = xs.shape[0]
    d, hid = w1.shape[1], w1.shape[2]
    nb = cap // MOE_BLOCK

    def row_map(i, blk_e_ref, n_used_ref):
        return (jnp.minimum(i, n_used_ref[0] - 1), 0)

    def w_map(i, blk_e_ref, n_used_ref):
        return (blk_e_ref[jnp.minimum(i, n_used_ref[0] - 1)], 0, 0)

    grid_spec = pltpu.PrefetchScalarGridSpec(
        num_scalar_prefetch=2,
        grid=(nb,),
        in_specs=[
            pl.BlockSpec((MOE_BLOCK, d // 2), row_map),
            pl.BlockSpec((1, d, hid), w_map),
            pl.BlockSpec((1, d, hid), w_map),
            pl.BlockSpec((1, hid, d), w_map),
        ],
        out_specs=pl.BlockSpec((MOE_BLOCK, d), lambda i, blk_e_ref, n_used_ref: (i, 0)),
        scratch_shapes=[pltpu.VMEM((d, hid), BF16), pltpu.VMEM((d, hid), BF16), pltpu.VMEM((hid, d), BF16)],
    )
    return pl.pallas_call(
        _expert_kernel,
        grid_spec=grid_spec,
        out_shape=jax.ShapeDtypeStruct((cap, d), F32),
        compiler_params=_cparams(("arbitrary",)),
        name="moe_experts",
    )(blk_e, n_used, xs, w1, w3, w2)


def _combine_kernel(dest_ref, x1_ref, gates_ref, ys_ref, o_ref, y0_scr, y1_scr, sem):
    tt = x1_ref.shape[0]

    def copy(r, k):
        dst = y0_scr if k == 0 else y1_scr
        return pltpu.make_async_copy(ys_ref.at[pl.ds(dest_ref[0, 0, 2 * r + k], 1)], dst.at[pl.ds(r, 1)], sem)

    for r in range(tt):
        copy(r, 0).start(priority=0)
        copy(r, 1).start(priority=1)

    def drain(r, c):
        copy(r, 0).wait()
        copy(r, 1).wait()
        return c

    lax.fori_loop(0, tt, drain, 0, unroll=True)
    g = gates_ref[...]
    o_ref[...] = x1_ref[...] + y0_scr[...] * g[:, 0:1] + y1_scr[...] * g[:, 1:2]


def _combine(dest2, x1, gates, ys):
    n, d = x1.shape
    tt = TOK_TILE
    return pl.pallas_call(
        _combine_kernel,
        grid=(n // tt,),
        in_specs=[
            pl.BlockSpec((1, 1, 2 * tt), lambda i: (i, 0, 0), memory_space=pltpu.SMEM),
            pl.BlockSpec((tt, d), lambda i: (i, 0)),
            pl.BlockSpec((tt, LANES), lambda i: (i, 0)),
            pl.BlockSpec(memory_space=pl.ANY),
        ],
        out_specs=pl.BlockSpec((tt, d), lambda i: (i, 0)),
        out_shape=jax.ShapeDtypeStruct((n, d), F32),
        scratch_shapes=[
            pltpu.VMEM((tt, d), F32),
            pltpu.VMEM((tt, d), F32),
            pltpu.SemaphoreType.DMA(()),
        ],
        compiler_params=_cparams(("arbitrary",)),
        name="moe_combine",
    )(dest2, x1, gates, ys)


def _rel_bucket_np(rel):
    nb = REL_BUCKETS // 2
    max_exact = nb // 2
    base = np.where(rel > 0, nb, 0)
    n = np.abs(rel)
    nf = np.maximum(n, 1).astype(np.float64)
    large = max_exact + (np.log(nf / max_exact) / math.log(REL_MAX_DIST / max_exact) * (nb - max_exact)).astype(np.int64)
    large = np.minimum(large, nb - 1)
    return (base + np.where(n < max_exact, n, large)).astype(np.int32)


def _bias_bucket_rows(t):
    far = _rel_bucket_np(-np.arange(t + 1, 4 * t))
    far_bucket = int(far[0])
    assert (far == far_bucket).all()
    m = np.arange(2 * t)
    wrap = np.where(m < t, m, m - 2 * t)
    return np.stack([_rel_bucket_np(wrap), _rel_bucket_np(wrap - t)]), far_bucket


def _toeplitz(rows, t):
    lead = rows.shape[:-1]
    flat = jnp.tile(rows, (1,) * len(lead) + (t,))[..., : t * (2 * t - 1)]
    return flat.reshape(lead + (t, 2 * t - 1))[..., :t]


def kernel(x, g_attn, w_in, qn_g, kn_g, lam_q1, lam_k1, lam_q2, lam_k2, subln_g, sb_out_g, rel_bias, w_o,
           g_ffn, w_router_g, b_router_g, w_router_e, b_router_e, w1, w3, w2):
    bsz, s_len, d = x.shape
    n = bsz * s_len
    depth = g_attn.shape[0]
    da_width = DA_HEADS * 2 * DA_HEAD_DIM
    sb_width = SB_HEADS * SB_HEAD_DIM

    bd = jnp.asarray(np.kron(np.eye(256 // DA_HEAD_DIM), np.ones((DA_HEAD_DIM, DA_HEAD_DIM))), BF16)
    tri = jnp.asarray(np.tril(np.ones((SB_WIN, SB_WIN)), -1), BF16)
    bucket_rows, far_bucket = _bias_bucket_rows(DA_BLOCK)
    upper = jnp.asarray(np.triu(np.ones((PROJ_ROWS, PROJ_ROWS)), 1), BF16)

    for l in range(depth):
        lambda_init = 0.8 - 0.6 * math.exp(-0.3 * l)
        x2 = x.reshape(n, d)

        q_scale = DA_HEAD_DIM ** -0.5 * LOG2E
        gain_row = jnp.concatenate([
            jnp.tile(qn_g[l] * q_scale, 2 * DA_HEADS),
            jnp.tile(kn_g[l], 2 * DA_HEADS),
            jnp.ones((da_width,), F32),
            jnp.full((sb_width,), SB_HEAD_DIM ** -0.5, F32),
            jnp.ones((2 * sb_width,), F32),
        ])[None, :]
        proj = _in_proj(x2, g_attn[l][None, :], w_in[l].astype(BF16), gain_row, bd)
        proj3 = proj.reshape(bsz, s_len, -1)

        lam = (jnp.exp(jnp.sum(lam_q1[l] * lam_k1[l])) - jnp.exp(jnp.sum(lam_q2[l] * lam_k2[l]))
               + lambda_init).astype(F32).reshape(1)
        rb = (rel_bias - rel_bias[far_bucket][None, :]) * LOG2E
        bias_rows = jnp.transpose(rb[bucket_rows], (2, 0, 1)).astype(F32)
        bias_tiles = _toeplitz(bias_rows, DA_BLOCK)
        q_reach = jnp.full((DA_HEADS,), DA_HEAD_DIM ** 0.5 * 1.02) * jnp.max(jnp.abs(qn_g[l] * q_scale))
        bias_range = jnp.stack([jnp.maximum(jnp.max(rb, axis=0), 0.0), jnp.minimum(jnp.min(rb, axis=0), 0.0),
                                q_reach])
        o_da = _diff_attention(proj3, lam, bias_range.astype(F32), bias_tiles, subln_g[l][None, :],
                               1.0 - lambda_init)
        o_sb = _stick_breaking(proj3, jnp.tile(sb_out_g[l], 2)[None, :], tri)

        wr = jnp.concatenate([w_router_g[l], w_router_e[l],
                              jnp.zeros((d, ROUTE_ROWS - N_GROUPS - N_EXPERTS), F32)], axis=1).T
        wr_hi = wr.astype(BF16)
        wr_mid = (wr - wr_hi.astype(F32)).astype(BF16)
        br = jnp.concatenate([b_router_g[l], b_router_e[l],
                              jnp.zeros((ROUTE_ROWS - N_GROUPS - N_EXPERTS,), F32)])
        br = jnp.broadcast_to(br[:, None], (ROUTE_ROWS, PROJ_ROWS))
        x1, h2, route_i, route_f, counts_all = _out_proj(
            x2, o_da.reshape(n, -1), o_sb.reshape(n, -1), w_o[l].astype(BF16), g_ffn[l][None, :],
            jnp.stack([wr_hi, wr_mid]), br, upper)

        counts = counts_all[N_GROUPS:N_GROUPS + N_EXPERTS, 0]
        padded = (counts + MOE_BLOCK - 1) // MOE_BLOCK * MOE_BLOCK
        pad_ends = jnp.cumsum(padded)
        pad_starts = pad_ends - padded
        experts = jnp.arange(N_EXPERTS, dtype=jnp.int32)
        eid, rank = route_i[0:2], route_i[2:4]
        start_of = jnp.sum(jnp.where(eid[:, :, None] == experts, pad_starts, 0), axis=-1)
        dest2 = (start_of + rank).T.reshape(n // TOK_TILE, 1, 2 * TOK_TILE)
        gates = jnp.pad(route_f[0:2].T, ((0, 0), (0, LANES - 2)))
        cap = 2 * n + N_EXPERTS * MOE_BLOCK
        nb = cap // MOE_BLOCK
        blk_start = jnp.arange(nb, dtype=jnp.int32) * MOE_BLOCK
        blk_e = jnp.minimum(jnp.sum(pad_ends[None, :] <= blk_start[:, None], axis=1), N_EXPERTS - 1).astype(jnp.int32)
        n_used = (pad_ends[-1] // MOE_BLOCK).astype(jnp.int32).reshape(1)

        xs = _dispatch(dest2, h2, jnp.zeros((cap, d // 2), jnp.uint32))
        ys = _experts(blk_e, n_used, xs, w1[l], w3[l], w2[l])
        x = _combine(dest2, x1, gates, ys).reshape(bsz, s_len, d)
    return x
```

```python
import functools
import math

import jax
import jax.numpy as jnp
import numpy as np
from jax import lax
from jax.experimental import pallas as pl
from jax.experimental.pallas import tpu as pltpu

F32 = jnp.float32
BF16 = jnp.bfloat16

EPS = 1e-6
LANES = 128
DA_HEADS = 4
DA_HEAD_DIM = 64
SB_HEADS = 8
SB_HEAD_DIM = 64
CHUNK = 64
REL_BUCKETS = 32
REL_MAX_DIST = 128
N_GROUPS = 4
EXPERTS_PER_GROUP = 8
N_EXPERTS = N_GROUPS * EXPERTS_PER_GROUP
NEG_BIG = -1e30
LOG2E = math.log2(math.e)
SB_DEAD = -87.5

VMEM_LIMIT = 48 * 1024 * 1024

PROJ_ROWS = 1024
ROUTE_ROWS = 48
DA_BLOCK = 256
DA_TILES = 4
DA_UNROLL = 4
DA_GROUPS_PER_ITER = 4
DA_SAFE_RANGE = 100.0
SB_SUB = 128
SB_ROWS = 2048
SB_WIN = 256
MOE_BLOCK = 512
MOE_CHUNKS = 2
TOK_TILE = 1024


def _cparams(sem):
    return pltpu.CompilerParams(dimension_semantics=sem, vmem_limit_bytes=VMEM_LIMIT)


def _in_proj_kernel(x_ref, g_ref, w_ref, gain_ref, bd_ref, o_ref):
    x = x_ref[...]
    ms = jnp.mean(x * x, axis=-1, keepdims=True)
    h = (x * lax.rsqrt(ms + EPS) * g_ref[...]).astype(BF16)
    n_chunks = o_ref.shape[1] // 512
    for c in range(n_chunks):
        cols = slice(c * 512, (c + 1) * 512)
        acc = jnp.dot(h, w_ref[:, cols], preferred_element_type=F32)
        if c < 2:
            sq = (acc * acc).astype(BF16)
            parts = []
            for s in range(2):
                ss = jnp.dot(sq[:, s * 256:(s + 1) * 256], bd_ref[...], preferred_element_type=F32)
                parts.append(ss)
            ss = jnp.concatenate(parts, axis=1)
            acc = acc * lax.rsqrt(ss * (1.0 / DA_HEAD_DIM) + EPS)
        o_ref[:, cols] = (acc * gain_ref[:, cols]).astype(BF16)


def _in_proj(x2, g_attn, w_in_bf, gain_row, bd):
    n, d = x2.shape
    width = w_in_bf.shape[1]
    return pl.pallas_call(
        _in_proj_kernel,
        grid=(n // PROJ_ROWS,),
        in_specs=[
            pl.BlockSpec((PROJ_ROWS, d), lambda i: (i, 0)),
            pl.BlockSpec((1, d), lambda i: (0, 0)),
            pl.BlockSpec((d, width), lambda i: (0, 0)),
            pl.BlockSpec((1, width), lambda i: (0, 0)),
            pl.BlockSpec((256, 256), lambda i: (0, 0)),
        ],
        out_specs=pl.BlockSpec((PROJ_ROWS, width), lambda i: (i, 0)),
        out_shape=jax.ShapeDtypeStruct((n, width), BF16),
        compiler_params=_cparams(("parallel",)),
        name="in_proj",
    )(x2, g_attn, w_in_bf, gain_row, bd)


def _da_kernel(*refs, out_scale):
    def tile_body(sub, carry):
        _da_tile(pl.program_id(2) * DA_TILES + sub, sub, *refs, out_scale=out_scale)
        return carry

    lax.fori_loop(0, DA_TILES, tile_body, 0)


def _da_tile(i, sub, lam_ref, brange_ref, q_ref, k_ref, v_ref, bias_ref, g_ref, o_ref, m_scr, acc_scr, stab_scr,
             fits_scr, pa_scr, pb_scr, *, out_scale):
    t = DA_BLOCK
    h = pl.program_id(1)
    s_len = k_ref.shape[1]
    rows_of_tile = pl.ds(pl.multiple_of(sub * t, t), t)
    ones_col = jnp.ones((LANES, LANES), BF16)

    @pl.when(i == 0)
    def _():
        lane_k = lax.broadcasted_iota(jnp.int32, (1, LANES), 1)
        rows = 512

        def kbody(c, best):
            kc = k_ref[0, pl.ds(pl.multiple_of(c * rows, rows), rows), :].astype(F32)
            sq = kc * kc
            n1 = jnp.dot(jnp.where(lane_k < DA_HEAD_DIM, sq, 0.0).astype(BF16), ones_col, preferred_element_type=F32)
            n2 = jnp.dot(jnp.where(lane_k >= DA_HEAD_DIM, sq, 0.0).astype(BF16), ones_col, preferred_element_type=F32)
            return (jnp.maximum(best[0], jnp.max(n1, axis=0, keepdims=True)),
                    jnp.maximum(best[1], jnp.max(n2, axis=0, keepdims=True)))

        zero_row = jnp.zeros((1, LANES), F32)
        k1, k2 = lax.fori_loop(0, s_len // rows, kbody, (zero_row, zero_row))
        reach1 = jnp.max(jnp.sqrt(k1)) * (brange_ref[2, h] * 1.02)
        reach2 = jnp.max(jnp.sqrt(k2)) * (brange_ref[2, h] * 1.02)
        stab_scr[0] = reach1 + brange_ref[0, h]
        stab_scr[1] = reach2 + brange_ref[0, h]
        spread = 2.0 * jnp.maximum(reach1, reach2) + (brange_ref[0, h] - brange_ref[1, h])
        fits_scr[0] = (spread <= DA_SAFE_RANGE).astype(jnp.int32)

    q = q_ref[0, rows_of_tile, :]
    lane = lax.broadcasted_iota(jnp.int32, (1, LANES), 1)
    zero = jnp.zeros_like(q)
    qq = jnp.concatenate([jnp.where(lane < DA_HEAD_DIM, q, zero),
                          jnp.where(lane >= DA_HEAD_DIM, q, zero)], axis=0)
    ones = jnp.ones((t, LANES), BF16)
    row = lax.broadcasted_iota(jnp.int32, (t, t), 0)
    col = lax.broadcasted_iota(jnp.int32, (t, t), 1)
    chunk_mask = (col // CHUNK) <= (row // CHUNK)
    n_far = jnp.maximum(i - 1, 0)

    def values(j):
        return jnp.concatenate([v_ref[0, pl.ds(pl.multiple_of(j * t, t), t), :], ones], axis=1)

    def scores(j, bias, mask):
        kb = k_ref[0, pl.ds(pl.multiple_of(j * t, t), t), :]
        s = lax.dot_general(qq, kb, (((1,), (1,)), ((), ())), preferred_element_type=F32)
        if bias is not None:
            s = s + jnp.concatenate([bias, bias], axis=0)
        if mask is not None:
            s = jnp.where(jnp.concatenate([mask, mask], axis=0) if mask.ndim else mask, s, NEG_BIG)
        return s, values(j)

    fits = fits_scr[0] != 0

    @pl.when(fits)
    def _():
        stab = jnp.concatenate([jnp.full((t, t), stab_scr[0], F32), jnp.full((t, t), stab_scr[1], F32)], axis=0)

        def probs_into(g, p_scr):
            for u in range(DA_UNROLL):
                s, _ = scores(g * DA_UNROLL + u, None, None)
                p_scr[u] = jnp.exp2(s - stab).astype(BF16)

        def weighted_from(g, p_scr):
            tot = jnp.dot(p_scr[0], values(g * DA_UNROLL), preferred_element_type=F32)
            for u in range(1, DA_UNROLL):
                tot = tot + jnp.dot(p_scr[u], values(g * DA_UNROLL + u), preferred_element_type=F32)
            acc_scr[...] += tot

        n_grp = n_far // DA_UNROLL

        left = n_far - n_grp * DA_UNROLL

        def head_region(n_left):
            tail = [(n_grp * DA_UNROLL + u, None, None) for u in range(n_left)]
            tail += [(n_far, bias_ref[0, 1], i > 0), (i, bias_ref[0, 0], chunk_mask)]
            tail_scores = [scores(j, bias, mask) for j, bias, mask in tail]
            probs_into(0, pa_scr)
            tot = None
            for s, vb in tail_scores:
                pv = jnp.dot(jnp.exp2(s - stab).astype(BF16), vb, preferred_element_type=F32)
                tot = pv if tot is None else tot + pv
            acc_scr[...] = tot

        for n_left in range(DA_UNROLL):
            pl.when(left == n_left)(functools.partial(head_region, n_left))

        bufs = (pa_scr, pb_scr)

        def drain(g0, count, produce_last):
            for c in range(count):
                weighted_from(g0 + c, bufs[c % 2])
                if c + 1 < count or produce_last:
                    probs_into(g0 + c + 1, bufs[(c + 1) % 2])

        def iter_body(it, carry):
            drain(it * DA_GROUPS_PER_ITER, DA_GROUPS_PER_ITER, True)
            return carry

        n_it = jnp.maximum(n_grp - 1, 0) // DA_GROUPS_PER_ITER
        lax.fori_loop(0, n_it, iter_body, 0)
        rest = n_grp - n_it * DA_GROUPS_PER_ITER
        for count in range(1, DA_GROUPS_PER_ITER + 1):
            pl.when(rest == count)(functools.partial(drain, n_it * DA_GROUPS_PER_ITER, count, False))

    @pl.when(jnp.logical_not(fits))
    def _():
        m_scr[...] = jnp.full(m_scr.shape, NEG_BIG, F32)
        acc_scr[...] = jnp.zeros(acc_scr.shape, F32)

        def step(j, bias=None, mask=None):
            s, vb = scores(j, bias, mask)
            m_prev = m_scr[...]
            m_next = jnp.maximum(m_prev, jnp.max(s, axis=1, keepdims=True))
            alpha = jnp.exp2(m_prev - m_next)
            p = jnp.exp2(s - jnp.concatenate([m_next] * (t // LANES), axis=1))
            pv = jnp.dot(p.astype(BF16), vb, preferred_element_type=F32)
            acc_scr[...] = jnp.concatenate([alpha, alpha], axis=1) * acc_scr[...] + pv
            m_scr[...] = m_next

        def far_body(j, carry):
            step(j)
            return carry

        lax.fori_loop(0, n_far, far_body, 0)

        @pl.when(i > 0)
        def _():
            step(i - 1, bias=bias_ref[0, 1])

        step(i, bias=bias_ref[0, 0], mask=chunk_mask)

    acc = acc_scr[...]
    o_all = acc[:, 0:LANES] / acc[:, LANES:2 * LANES]
    o = o_all[0:t] - lam_ref[0] * o_all[t:2 * t]
    ms = jnp.mean(o * o, axis=-1, keepdims=True)
    o_ref[0, rows_of_tile, :] = (o * lax.rsqrt(ms + EPS) * (g_ref[...] * out_scale)).astype(BF16)


def _diff_attention(proj3, lam, bias_range, bias_tiles, subln_row, out_scale):
    bsz, s_len, _ = proj3.shape
    t = DA_BLOCK
    tq = DA_TILES * t
    nq = s_len // tq
    kern = functools.partial(_da_kernel, out_scale=out_scale)
    return pl.pallas_call(
        kern,
        grid=(bsz, DA_HEADS, nq),
        in_specs=[
            pl.BlockSpec(memory_space=pltpu.SMEM),
            pl.BlockSpec(memory_space=pltpu.SMEM),
            pl.BlockSpec((1, tq, LANES), lambda b, h, i: (b, i, h)),
            pl.BlockSpec((1, s_len, LANES), lambda b, h, i: (b, 0, DA_HEADS + h)),
            pl.BlockSpec((1, s_len, LANES), lambda b, h, i: (b, 0, 2 * DA_HEADS + h)),
            pl.BlockSpec((1, 2, t, t), lambda b, h, i: (h, 0, 0, 0)),
            pl.BlockSpec((1, LANES), lambda b, h, i: (0, 0)),
        ],
        out_specs=pl.BlockSpec((1, tq, LANES), lambda b, h, i: (b, i, h)),
        out_shape=jax.ShapeDtypeStruct((bsz, s_len, DA_HEADS * LANES), BF16),
        scratch_shapes=[
            pltpu.VMEM((2 * t, LANES), F32),
            pltpu.VMEM((2 * t, 2 * LANES), F32),
            pltpu.SMEM((2,), F32),
            pltpu.SMEM((1,), jnp.int32),
            pltpu.VMEM((DA_UNROLL, 2 * t, t), BF16),
            pltpu.VMEM((DA_UNROLL, 2 * t, t), BF16),
        ],
        compiler_params=_cparams(("parallel", "parallel", "arbitrary")),
        name="diff_attention",
    )(lam, bias_range, proj3, proj3, proj3, bias_tiles, subln_row)


def _sb_tiles(tiles, tri):
    zs = [lax.dot_general(q, k, (((1,), (1,)), ((), ())), preferred_element_type=F32) for q, k, _, _, _ in tiles]
    mids = []
    for z, (_, _, _, causal, _) in zip(zs, tiles):
        log_1m = -(jnp.maximum(z, 0.0) + jnp.log(1.0 + jnp.exp(-jnp.abs(z))))
        lm = log_1m if causal is None else jnp.where(causal, log_1m, 0.0)
        inner = jnp.dot(lm.astype(BF16), tri, preferred_element_type=F32)
        mids.append((z + log_1m + inner, jnp.sum(lm, axis=1, keepdims=True)))
    outs = []
    for (logit, total), (_, _, v, causal, carry) in zip(mids, tiles):
        if carry is not None:
            logit = logit + carry
        a = jnp.exp(logit)
        if causal is not None:
            a = jnp.where(causal, a, 0.0)
        outs.append((jnp.dot(a.astype(BF16), v, preferred_element_type=F32), total))
    return outs


def _sb_kernel(q_ref, k_ref, v_ref, g_ref, tri_ref, o_ref, acc_scr, carry_scr):
    sub, win = SB_SUB, SB_WIN
    i = pl.program_id(2)
    lane = lax.broadcasted_iota(jnp.int32, (1, LANES), 1)
    tri = tri_ref[...]
    ahead = (lax.broadcasted_iota(jnp.int32, (sub, win), 1)
             - lax.broadcasted_iota(jnp.int32, (sub, win), 0))
    n_sub = SB_ROWS // sub

    tiles, starts = [], []
    for u in range(n_sub):
        r0 = i * SB_ROWS + u * sub
        start = pl.multiple_of(jnp.maximum(r0 - sub, 0), sub)
        qu = q_ref[0, u * sub:(u + 1) * sub, :]
        zero = jnp.zeros_like(qu)
        qs = jnp.concatenate([jnp.where(lane < SB_HEAD_DIM, qu, zero),
                              jnp.where(lane >= SB_HEAD_DIM, qu, zero)], axis=0)
        causal = ahead < r0 - start
        tiles.append((qs, k_ref[0, pl.ds(start, win), :], v_ref[0, pl.ds(start, win), :],
                      jnp.concatenate([causal, causal], axis=0), None))
        starts.append(start)
    chains = []
    for u, (pv, total) in enumerate(_sb_tiles(tiles, tri)):
        acc_scr[u] = pv
        carry_scr[u] = jnp.broadcast_to(total, (2 * sub, LANES))
        chains.append((u, tiles[u][0], starts[u] // sub - 1, jnp.max(total)))

    tri_sub = tri[0:sub, 0:sub]
    for u, qs, j0, alive0 in chains:
        def cond(state):
            j, alive = state
            return (j >= 0) & (alive > SB_DEAD)

        def body(state, u=u, qs=qs):
            j, _ = state
            s0 = pl.multiple_of(j * sub, sub)
            kb = k_ref[0, pl.ds(s0, sub), :]
            vb = v_ref[0, pl.ds(s0, sub), :]
            carry = carry_scr[u]
            (pv, total), = _sb_tiles([(qs, kb, vb, None, carry)], tri_sub)
            acc_scr[u] += pv
            carry = carry + total
            carry_scr[u] = carry
            return j - 1, jnp.max(carry)

        lax.while_loop(cond, body, (j0, alive0))

    for u in range(n_sub):
        acc = acc_scr[u]
        o = jnp.where(lane < SB_HEAD_DIM, acc[0:sub], acc[sub:2 * sub])
        sq = o * o
        ss0 = jnp.sum(jnp.where(lane < SB_HEAD_DIM, sq, 0.0), axis=1, keepdims=True)
        ss1 = jnp.sum(jnp.where(lane >= SB_HEAD_DIM, sq, 0.0), axis=1, keepdims=True)
        ms = jnp.where(lane < SB_HEAD_DIM, ss0, ss1) * (1.0 / SB_HEAD_DIM)
        o_ref[0, u * sub:(u + 1) * sub, :] = (o * lax.rsqrt(ms + EPS) * g_ref[...]).astype(BF16)


def _stick_breaking(proj3, sb_row, tri):
    bsz, s_len, _ = proj3.shape
    t = SB_ROWS
    nq = s_len // t
    pairs = SB_HEADS // 2
    q_blk = 3 * DA_HEADS
    n_chains = SB_ROWS // SB_SUB
    return pl.pallas_call(
        _sb_kernel,
        grid=(bsz, pairs, nq),
        in_specs=[
            pl.BlockSpec((1, t, LANES), lambda b, h, i: (b, i, q_blk + h)),
            pl.BlockSpec((1, s_len, LANES), lambda b, h, i: (b, 0, q_blk + pairs + h)),
            pl.BlockSpec((1, s_len, LANES), lambda b, h, i: (b, 0, q_blk + 2 * pairs + h)),
            pl.BlockSpec((1, LANES), lambda b, h, i: (0, 0)),
            pl.BlockSpec((SB_WIN, SB_WIN), lambda b, h, i: (0, 0)),
        ],
        out_specs=pl.BlockSpec((1, t, LANES), lambda b, h, i: (b, i, h)),
        out_shape=jax.ShapeDtypeStruct((bsz, s_len, pairs * LANES), BF16),
        scratch_shapes=[
            pltpu.VMEM((n_chains, 2 * SB_SUB, LANES), F32),
            pltpu.VMEM((n_chains, 2 * SB_SUB, LANES), F32),
        ],
        compiler_params=_cparams(("parallel", "parallel", "arbitrary")),
        name="stick_breaking",
    )(proj3, proj3, proj3, sb_row, tri)


def _out_proj_kernel(x_ref, oda_ref, osb_ref, wo_ref, g_ref, wr_ref, br_ref, upper_ref,
                     x1_ref, h2_ref, route_i_ref, route_f_ref, counts_ref, base_scr):
    half = oda_ref.shape[1]
    x1 = (x_ref[...]
          + jnp.dot(oda_ref[...], wo_ref[0:half, :], preferred_element_type=F32)
          + jnp.dot(osb_ref[...], wo_ref[half:2 * half, :], preferred_element_type=F32))
    x1_ref[...] = x1
    ms = jnp.mean(x1 * x1, axis=-1, keepdims=True)
    h2 = x1 * lax.rsqrt(ms + EPS) * g_ref[...]
    d_half = h2.shape[1] // 2
    hi_bits = lax.bitcast_convert_type(h2[:, :d_half].astype(BF16).astype(F32), jnp.uint32)
    lo_bits = lax.bitcast_convert_type(h2[:, d_half:].astype(BF16).astype(F32), jnp.uint32)
    h2_ref[...] = hi_bits | (lo_bits >> 16)

    a_hi = h2.astype(BF16)
    a_mid = (h2 - a_hi.astype(F32)).astype(BF16)
    nt = lambda w, a: lax.dot_general(w, a, (((1,), (1,)), ((), ())), preferred_element_type=F32)
    lt = nt(wr_ref[0], a_hi) + (nt(wr_ref[0], a_mid) + nt(wr_ref[1], a_hi)) + br_ref[...]
    n_rows, tm = lt.shape
    ridx = lax.broadcasted_iota(jnp.int32, (n_rows, tm), 0)
    big = jnp.int32(n_rows)
    is_g = ridx < N_GROUPS
    lg = jnp.where(is_g, lt, NEG_BIG)
    mg = jnp.max(lg, axis=0, keepdims=True)
    gsel = jnp.min(jnp.where(lg == mg, ridx, big), axis=0, keepdims=True)
    pg_sel = 1.0 / jnp.sum(jnp.where(is_g, jnp.exp(lg - mg), 0.0), axis=0, keepdims=True)
    lo_row = N_GROUPS + gsel * EXPERTS_PER_GROUP
    le = jnp.where(ridx >= lo_row, jnp.where(ridx < lo_row + EXPERTS_PER_GROUP, lt, NEG_BIG), NEG_BIG)
    l1 = jnp.max(le, axis=0, keepdims=True)
    i1 = jnp.min(jnp.where(le == l1, ridx, big), axis=0, keepdims=True)
    le2 = jnp.where(ridx == i1, NEG_BIG, le)
    l2 = jnp.max(le2, axis=0, keepdims=True)
    i2 = jnp.min(jnp.where(le2 == l2, ridx, big), axis=0, keepdims=True)
    e2 = jnp.exp(l2 - l1)
    g1 = pg_sel / (1.0 + e2)
    g2 = pg_sel * e2 / (1.0 + e2)

    @pl.when(pl.program_id(0) == 0)
    def _():
        base_scr[...] = jnp.zeros(base_scr.shape, F32)

    sel1 = ridx == i1
    sel2 = ridx == i2
    chosen = jnp.where(sel1, 1.0, jnp.where(sel2, 1.0, 0.0))
    before = jnp.dot(chosen.astype(BF16), upper_ref[...], preferred_element_type=F32)
    base = base_scr[...]
    before = before + jnp.concatenate([base] * (tm // LANES), axis=1)
    rank1 = jnp.sum(jnp.where(sel1, before, 0.0), axis=0, keepdims=True)
    rank2 = jnp.sum(jnp.where(sel2, before, 0.0), axis=0, keepdims=True)
    base = base + jnp.sum(chosen, axis=1, keepdims=True)
    base_scr[...] = base
    counts_ref[...] = base.astype(jnp.int32)

    r8 = lax.broadcasted_iota(jnp.int32, (8, tm), 0)
    route_i_ref[...] = jnp.where(r8 == 0, i1 - N_GROUPS, jnp.where(r8 == 1, i2 - N_GROUPS, jnp.where(
        r8 == 2, rank1.astype(jnp.int32), jnp.where(r8 == 3, rank2.astype(jnp.int32), 0))))
    route_f_ref[...] = jnp.where(r8 == 0, g1, jnp.where(r8 == 1, g2, 0.0))


def _out_proj(x2, o_da, o_sb, wo_bf, g_ffn, wr2, br, upper):
    n, d = x2.shape
    half = o_da.shape[1]
    tm = PROJ_ROWS
    r = ROUTE_ROWS
    return pl.pallas_call(
        _out_proj_kernel,
        grid=(n // tm,),
        in_specs=[
            pl.BlockSpec((tm, d), lambda i: (i, 0)),
            pl.BlockSpec((tm, half), lambda i: (i, 0)),
            pl.BlockSpec((tm, half), lambda i: (i, 0)),
            pl.BlockSpec((2 * half, d), lambda i: (0, 0)),
            pl.BlockSpec((1, d), lambda i: (0, 0)),
            pl.BlockSpec((2, r, d), lambda i: (0, 0, 0)),
            pl.BlockSpec((r, tm), lambda i: (0, 0)),
            pl.BlockSpec((tm, tm), lambda i: (0, 0)),
        ],
        out_specs=[
            pl.BlockSpec((tm, d), lambda i: (i, 0)),
            pl.BlockSpec((tm, d // 2), lambda i: (i, 0)),
            pl.BlockSpec((8, tm), lambda i: (0, i)),
            pl.BlockSpec((8, tm), lambda i: (0, i)),
            pl.BlockSpec((r, LANES), lambda i: (0, 0)),
        ],
        out_shape=[
            jax.ShapeDtypeStruct((n, d), F32),
            jax.ShapeDtypeStruct((n, d // 2), jnp.uint32),
            jax.ShapeDtypeStruct((8, n), jnp.int32),
            jax.ShapeDtypeStruct((8, n), F32),
            jax.ShapeDtypeStruct((r, LANES), jnp.int32),
        ],
        scratch_shapes=[pltpu.VMEM((r, LANES), F32)],
        compiler_params=_cparams(("arbitrary",)),
        name="out_proj_router",
    )(x2, o_da, o_sb, wo_bf, g_ffn, wr2, br, upper)


def _dispatch_kernel(dest_ref, h_ref, xs_in_ref, xs_ref, sem):
    del xs_in_ref
    tt = h_ref.shape[0]

    def copy(r, k):
        return pltpu.make_async_copy(h_ref.at[pl.ds(r, 1)], xs_ref.at[pl.ds(dest_ref[0, 0, 2 * r + k], 1)], sem)

    for r in range(tt):
        copy(r, 0).start(priority=0)
        copy(r, 1).start(priority=1)

    def drain(r, c):
        copy(r, 0).wait()
        copy(r, 1).wait()
        return c

    lax.fori_loop(0, tt, drain, 0, unroll=True)


def _dispatch(dest2, h2, xs_init):
    n, d = h2.shape
    tt = TOK_TILE
    return pl.pallas_call(
        _dispatch_kernel,
        grid=(n // tt,),
        in_specs=[
            pl.BlockSpec((1, 1, 2 * tt), lambda i: (i, 0, 0), memory_space=pltpu.SMEM),
            pl.BlockSpec((tt, d), lambda i: (i, 0)),
            pl.BlockSpec(memory_space=pl.ANY),
        ],
        out_specs=pl.BlockSpec(memory_space=pl.ANY),
        out_shape=jax.ShapeDtypeStruct(xs_init.shape, xs_init.dtype),
        scratch_shapes=[pltpu.SemaphoreType.DMA(())],
        input_output_aliases={2: 0},
        compiler_params=_cparams(("arbitrary",)),
        name="moe_dispatch",
    )(dest2, h2, xs_init)


def _expert_kernel(blk_e_ref, n_used_ref, xs_ref, w1_ref, w3_ref, w2_ref, ys_ref, w1_scr, w3_scr, w2_scr):
    i = pl.program_id(0)
    used = i < n_used_ref[0]

    @pl.when(used & ((i == 0) | (blk_e_ref[i] != blk_e_ref[jnp.maximum(i - 1, 0)])))
    def _():
        w1_scr[...] = w1_ref[0].astype(BF16)
        w3_scr[...] = w3_ref[0].astype(BF16)
        w2_scr[...] = w2_ref[0].astype(BF16)

    @pl.when(used)
    def _():
        rows = MOE_BLOCK // MOE_CHUNKS
        xbs = []
        for c in range(MOE_CHUNKS):
            words = xs_ref[c * rows:(c + 1) * rows, :]
            xbs.append(jnp.concatenate([
                lax.bitcast_convert_type(words & jnp.uint32(0xFFFF0000), F32).astype(BF16),
                lax.bitcast_convert_type(words << 16, F32).astype(BF16)], axis=1))
        gated = [(jnp.dot(xb, w1_scr[...], preferred_element_type=F32), jnp.dot(xb, w3_scr[...], preferred_element_type=F32))
                 for xb in xbs]
        hmids = [(a * jax.nn.sigmoid(a) * b).astype(BF16) for a, b in gated]
        for c, hmid in enumerate(hmids):
            ys_ref[c * rows:(c + 1) * rows, :] = jnp.dot(hmid, w2_scr[...], preferred_element_type=F32)

    @pl.when(jnp.logical_not(used))
    def _():
        ys_ref[...] = jnp.zeros(ys_ref.shape, F32)


def _experts(blk_e, n_used, xs, w1, w3, w2):
    cap = xs.shape[0]
    d, hid = w1.shape[1], w1.shape[2]
    nb = cap // MOE_BLOCK

    def row_map(i, blk_e_ref, n_used_ref):
        return (jnp.minimum(i, n_used_ref[0] - 1), 0)

    def w_map(i, blk_e_ref, n_used_ref):
        return (blk_e_ref[jnp.minimum(i, n_used_ref[0] - 1)], 0, 0)

    grid_spec = pltpu.PrefetchScalarGridSpec(
        num_scalar_prefetch=2,
        grid=(nb,),
        in_specs=[
            pl.BlockSpec((MOE_BLOCK, d // 2), row_map),
            pl.BlockSpec((1, d, hid), w_map),
            pl.BlockSpec((1, d, hid), w_map),
            pl.BlockSpec((1, hid, d), w_map),
        ],
        out_specs=pl.BlockSpec((MOE_BLOCK, d), lambda i, blk_e_ref, n_used_ref: (i, 0)),
        scratch_shapes=[pltpu.VMEM((d, hid), BF16), pltpu.VMEM((d, hid), BF16), pltpu.VMEM((hid, d), BF16)],
    )
    return pl.pallas_call(
        _expert_kernel,
        grid_spec=grid_spec,
        out_shape=jax.ShapeDtypeStruct((cap, d), F32),
        compiler_params=_cparams(("arbitrary",)),
        name="moe_experts",
    )(blk_e, n_used, xs, w1, w3, w2)


def _combine_kernel(dest_ref, x1_ref, gates_ref, ys_ref, o_ref, y0_scr, y1_scr, sem):
    tt = x1_ref.shape[0]

    def copy(r, k):
        dst = y0_scr if k == 0 else y1_scr
        return pltpu.make_async_copy(ys_ref.at[pl.ds(dest_ref[0, 0, 2 * r + k], 1)], dst.at[pl.ds(r, 1)], sem)

    for r in range(tt):
        copy(r, 0).start(priority=0)
        copy(r, 1).start(priority=1)

    def drain(r, c):
        copy(r, 0).wait()
        copy(r, 1).wait()
        return c

    lax.fori_loop(0, tt, drain, 0, unroll=True)
    g = gates_ref[...]
    o_ref[...] = x1_ref[...] + y0_scr[...] * g[:, 0:1] + y1_scr[...] * g[:, 1:2]


def _combine(dest2, x1, gates, ys):
    n, d = x1.shape
    tt = TOK_TILE
    return pl.pallas_call(
        _combine_kernel,
        grid=(n // tt,),
        in_specs=[
            pl.BlockSpec((1, 1, 2 * tt), lambda i: (i, 0, 0), memory_space=pltpu.SMEM),
            pl.BlockSpec((tt, d), lambda i: (i, 0)),
            pl.BlockSpec((tt, LANES), lambda i: (i, 0)),
            pl.BlockSpec(memory_space=pl.ANY),
        ],
        out_specs=pl.BlockSpec((tt, d), lambda i: (i, 0)),
        out_shape=jax.ShapeDtypeStruct((n, d), F32),
        scratch_shapes=[
            pltpu.VMEM((tt, d), F32),
            pltpu.VMEM((tt, d), F32),
            pltpu.SemaphoreType.DMA(()),
        ],
        compiler_params=_cparams(("arbitrary",)),
        name="moe_combine",
    )(dest2, x1, gates, ys)


def _rel_bucket_np(rel):
    nb = REL_BUCKETS // 2
    max_exact = nb // 2
    base = np.where(rel > 0, nb, 0)
    n = np.abs(rel)
    nf = np.maximum(n, 1).astype(np.float64)
    large = max_exact + (np.log(nf / max_exact) / math.log(REL_MAX_DIST / max_exact) * (nb - max_exact)).astype(np.int64)
    large = np.minimum(large, nb - 1)
    return (base + np.where(n < max_exact, n, large)).astype(np.int32)


def _bias_bucket_rows(t):
    far = _rel_bucket_np(-np.arange(t + 1, 4 * t))
    far_bucket = int(far[0])
    assert (far == far_bucket).all()
    m = np.arange(2 * t)
    wrap = np.where(m < t, m, m - 2 * t)
    return np.stack([_rel_bucket_np(wrap), _rel_bucket_np(wrap - t)]), far_bucket


def _toeplitz(rows, t):
    lead = rows.shape[:-1]
    flat = jnp.tile(rows, (1,) * len(lead) + (t,))[..., : t * (2 * t - 1)]
    return flat.reshape(lead + (t, 2 * t - 1))[..., :t]


def kernel(x, g_attn, w_in, qn_g, kn_g, lam_q1, lam_k1, lam_q2, lam_k2, subln_g, sb_out_g, rel_bias, w_o,
           g_ffn, w_router_g, b_router_g, w_router_e, b_router_e, w1, w3, w2):
    bsz, s_len, d = x.shape
    n = bsz * s_len
    depth = g_attn.shape[0]
    da_width = DA_HEADS * 2 * DA_HEAD_DIM
    sb_width = SB_HEADS * SB_HEAD_DIM

    bd = jnp.asarray(np.kron(np.eye(256 // DA_HEAD_DIM), np.ones((DA_HEAD_DIM, DA_HEAD_DIM))), BF16)
    tri = jnp.asarray(np.tril(np.ones((SB_WIN, SB_WIN)), -1), BF16)
    bucket_rows, far_bucket = _bias_bucket_rows(DA_BLOCK)
    upper = jnp.asarray(np.triu(np.ones((PROJ_ROWS, PROJ_ROWS)), 1), BF16)

    for l in range(depth):
        lambda_init = 0.8 - 0.6 * math.exp(-0.3 * l)
        x2 = x.reshape(n, d)

        q_scale = DA_HEAD_DIM ** -0.5 * LOG2E
        gain_row = jnp.concatenate([
            jnp.tile(qn_g[l] * q_scale, 2 * DA_HEADS),
            jnp.tile(kn_g[l], 2 * DA_HEADS),
            jnp.ones((da_width,), F32),
            jnp.full((sb_width,), SB_HEAD_DIM ** -0.5, F32),
            jnp.ones((2 * sb_width,), F32),
        ])[None, :]
        proj = _in_proj(x2, g_attn[l][None, :], w_in[l].astype(BF16), gain_row, bd)
        proj3 = proj.reshape(bsz, s_len, -1)

        lam = (jnp.exp(jnp.sum(lam_q1[l] * lam_k1[l])) - jnp.exp(jnp.sum(lam_q2[l] * lam_k2[l]))
               + lambda_init).astype(F32).reshape(1)
        rb = (rel_bias - rel_bias[far_bucket][None, :]) * LOG2E
        bias_rows = jnp.transpose(rb[bucket_rows], (2, 0, 1)).astype(F32)
        bias_tiles = _toeplitz(bias_rows, DA_BLOCK)
        q_reach = jnp.full((DA_HEADS,), DA_HEAD_DIM ** 0.5 * 1.02) * jnp.max(jnp.abs(qn_g[l] * q_scale))
        bias_range = jnp.stack([jnp.maximum(jnp.max(rb, axis=0), 0.0), jnp.minimum(jnp.min(rb, axis=0), 0.0),
                                q_reach])
        o_da = _diff_attention(proj3, lam, bias_range.astype(F32), bias_tiles, subln_g[l][None, :],
                               1.0 - lambda_init)
        o_sb = _stick_breaking(proj3, jnp.tile(sb_out_g[l], 2)[None, :], tri)

        wr = jnp.concatenate([w_router_g[l], w_router_e[l],
                              jnp.zeros((d, ROUTE_ROWS - N_GROUPS - N_EXPERTS), F32)], axis=1).T
        wr_hi = wr.astype(BF16)
        wr_mid = (wr - wr_hi.astype(F32)).astype(BF16)
        br = jnp.concatenate([b_router_g[l], b_router_e[l],
                              jnp.zeros((ROUTE_ROWS - N_GROUPS - N_EXPERTS,), F32)])
        br = jnp.broadcast_to(br[:, None], (ROUTE_ROWS, PROJ_ROWS))
        x1, h2, route_i, route_f, counts_all = _out_proj(
            x2, o_da.reshape(n, -1), o_sb.reshape(n, -1), w_o[l].astype(BF16), g_ffn[l][None, :],
            jnp.stack([wr_hi, wr_mid]), br, upper)

        counts = counts_all[N_GROUPS:N_GROUPS + N_EXPERTS, 0]
        padded = (counts + MOE_BLOCK - 1) // MOE_BLOCK * MOE_BLOCK
        pad_ends = jnp.cumsum(padded)
        pad_starts = pad_ends - padded
        experts = jnp.arange(N_EXPERTS, dtype=jnp.int32)
        eid, rank = route_i[0:2], route_i[2:4]
        start_of = jnp.sum(jnp.where(eid[:, :, None] == experts, pad_starts, 0), axis=-1)
        dest2 = (start_of + rank).T.reshape(n // TOK_TILE, 1, 2 * TOK_TILE)
        gates = jnp.pad(route_f[0:2].T, ((0, 0), (0, LANES - 2)))
        cap = 2 * n + N_EXPERTS * MOE_BLOCK
        nb = cap // MOE_BLOCK
        blk_start = jnp.arange(nb, dtype=jnp.int32) * MOE_BLOCK
        blk_e = jnp.minimum(jnp.sum(pad_ends[None, :] <= blk_start[:, None], axis=1), N_EXPERTS - 1).astype(jnp.int32)
        n_used = (pad_ends[-1] // MOE_BLOCK).astype(jnp.int32).reshape(1)

        xs = _dispatch(dest2, h2, jnp.zeros((cap, d // 2), jnp.uint32))
        ys = _experts(blk_e, n_used, xs, w1[l], w3[l], w2[l])
        x = _combine(dest2, x1, gates, ys).reshape(bsz, s_len, d)
    return x
```
